```python
import math
import jax
import jax.numpy as jnp
from jax import lax
import numpy as np

D_MODEL = 1024
BATCH = 2
SEQ = 8192
DEPTH = 2
DEC_BATCH = 32
DEC_SEQ = 1
PAST_LEN = 16384
PAGE_SIZE = 128

N_EVEN = (DEPTH + 1) // 2
N_ODD = DEPTH // 2
ATT_HEADS = 8
HEAD_DIM = 64
ATT_WIDTH = ATT_HEADS * HEAD_DIM
ROT_DIM = HEAD_DIM // 4
ROPE_THETA = 500000.0
MOBA_BLOCK = 256
MOBA_TOPK = 3
Q_BLOCK = 128
SSD_HEADS = 8
SSD_HEADDIM = 64
SSD_INNER = SSD_HEADS * SSD_HEADDIM
SSD_GROUPS = 2
SSD_STATE = 128
SSD_CONV = 4
SSD_CONV_DIM = SSD_INNER + 2 * SSD_GROUPS * SSD_STATE
SSD_CHUNK = 128
MIX_IN_DIM = 3 * ATT_WIDTH + SSD_INNER + SSD_CONV_DIM + SSD_HEADS
MIX_OUT_DIM = ATT_WIDTH + SSD_INNER
MIX_SPLITS = (ATT_WIDTH, 2 * ATT_WIDTH, 3 * ATT_WIDTH, 3 * ATT_WIDTH + SSD_INNER, 3 * ATT_WIDTH + SSD_INNER + SSD_CONV_DIM)
POOL_WINDOWS = (2, 4, 8, 16)
POOL_GROUPS = 4
POOL_GROUP_DIM = D_MODEL // POOL_GROUPS
POOL_MAX = 16
D_FF = 2816
FFN_CONV = 3
EPS = 1e-6

kernel_name = "hybrid_moba_ssd_pool_convffn_step"


def rmsnorm(x, g):
    xf = x.astype(jnp.float32)
    y = xf * lax.rsqrt(jnp.mean(xf * xf, axis=-1, keepdims=True) + EPS)
    return (y * g.astype(jnp.float32)).astype(x.dtype)


def adaln(c, w, b):
    mod = (jax.nn.silu(c) @ w + b)[:, None, :]
    return jnp.split(mod, 3, axis=-1)


def rope_partial(x, pos):
    half = ROT_DIM // 2
    inv = jnp.exp(jnp.arange(half, dtype=jnp.float32) * (-2.0 * math.log(ROPE_THETA) / ROT_DIM))
    ang = pos.astype(jnp.float32)[:, None] * inv[None, :]
    cos = jnp.cos(ang)[None, :, None, :]
    sin = jnp.sin(ang)[None, :, None, :]
    xr = x[..., :ROT_DIM].astype(jnp.float32)
    x1, x2 = xr[..., :half], xr[..., half:]
    rot = jnp.concatenate([x1 * cos - x2 * sin, x2 * cos + x1 * sin], axis=-1).astype(x.dtype)
    return jnp.concatenate([rot, x[..., ROT_DIM:]], axis=-1)


def causal_dwconv(x, prev, w, b):
    xx = jnp.concatenate([prev.astype(x.dtype), x], axis=1)
    y = lax.conv_general_dilated(xx, w.astype(x.dtype)[:, None, :], window_strides=(1,), padding="VALID",
                                 dimension_numbers=("NWC", "WIO", "NWC"), feature_group_count=x.shape[-1])
    return y + b.astype(x.dtype), xx[:, xx.shape[1] - (w.shape[0] - 1):]


def moba_attention(q, k_parts, v_parts, q_pos0):
    bsz, t_q, n_h, hd = q.shape
    seq_k = sum(p.shape[1] for p in k_parts)
    nb = -(-seq_k // MOBA_BLOCK)
    pad = jnp.zeros((bsz, nb * MOBA_BLOCK - seq_k, n_h, hd), k_parts[0].dtype)
    kb = jnp.concatenate(list(k_parts) + [pad], axis=1).reshape(bsz, nb, MOBA_BLOCK, n_h, hd)
    vb = jnp.concatenate(list(v_parts) + [pad], axis=1).reshape(bsz, nb, MOBA_BLOCK, n_h, hd)
    kmean = jnp.mean(kb, axis=2, dtype=jnp.float32)
    qc = Q_BLOCK if t_q % Q_BLOCK == 0 else t_q
    nq = t_q // qc
    qs = q.reshape(bsz, nq, qc, n_h, hd).transpose(1, 0, 3, 2, 4)
    bi = jnp.arange(bsz)[:, None, None, None]
    hi = jnp.arange(n_h)[None, :, None, None]
    blk = jnp.arange(nb)
    offs = jnp.arange(MOBA_BLOCK)
    scale = 1.0 / math.sqrt(hd)

    def one_query_block(args):
        qi, ci = args
        tpos = q_pos0 + ci * qc + jnp.arange(qc)
        jown = tpos // MOBA_BLOCK
        gate = jnp.einsum("bhqd,bnhd->bhqn", qi, kmean, preferred_element_type=jnp.float32)
        gate = jnp.where(blk[None, None, None, :] < jown[None, None, :, None], gate, -jnp.inf)
        if nb < MOBA_TOPK:
            gate = jnp.pad(gate, ((0, 0), (0, 0), (0, 0), (0, MOBA_TOPK - nb)), constant_values=-jnp.inf)
        _, sel = lax.top_k(gate, MOBA_TOPK)
        sel = jnp.minimum(sel, nb - 1)
        own = jnp.broadcast_to(jown[None, None, :, None], (bsz, n_h, qc, 1)).astype(sel.dtype)
        blocks = jnp.concatenate([sel, own], axis=-1)
        valid = jnp.concatenate([jnp.arange(MOBA_TOPK)[None, :] < jown[:, None], jnp.ones((qc, 1), dtype=bool)], axis=-1)
        kg = kb[bi, blocks, :, hi]
        vg = vb[bi, blocks, :, hi]
        s = jnp.einsum("bhqd,bhqnkd->bhqnk", qi, kg, preferred_element_type=jnp.float32) * scale
        kpos = blocks[..., None] * MOBA_BLOCK + offs
        mask = (kpos <= tpos[None, None, :, None, None]) & valid[None, None, :, :, None]
        s = jnp.where(mask, s, -jnp.inf)
        p = jax.nn.softmax(s.reshape(bsz, n_h, qc, -1), axis=-1).reshape(s.shape)
        o = jnp.einsum("bhqnk,bhqnkd->bhqd", p.astype(vg.dtype), vg, preferred_element_type=jnp.float32)
        return o.astype(q.dtype)

    out = lax.map(one_query_block, (qs, jnp.arange(nq)))
    return out.transpose(1, 0, 3, 2, 4).reshape(bsz, t_q, n_h, hd)


def ssd_scan(x, dt, a, bm, cm, h0):
    bsz, t, nh, hp = x.shape
    ln = SSD_CHUNK if t % SSD_CHUNK == 0 else t
    nc = t // ln
    rep = nh // SSD_GROUPS
    bh = jnp.repeat(bm, rep, axis=2).reshape(bsz, nc, ln, nh, SSD_STATE)
    ch = jnp.repeat(cm, rep, axis=2).reshape(bsz, nc, ln, nh, SSD_STATE)
    xc = x.reshape(bsz, nc, ln, nh, hp)
    dtc = dt.reshape(bsz, nc, ln, nh)
    acum = jnp.cumsum(dtc * a, axis=2)
    causal = jnp.tril(jnp.ones((ln, ln), dtype=bool))[None, None, :, :, None]
    seg = acum[:, :, :, None, :] - acum[:, :, None, :, :]
    decay = jnp.exp(jnp.where(causal, seg, -jnp.inf))
    scores = jnp.einsum("bclhn,bcshn->bclsh", ch, bh) * decay * dtc[:, :, None, :, :]
    y = jnp.einsum("bclsh,bcshp->bclhp", scores, xc)
    w_end = jnp.exp(acum[:, :, -1:, :] - acum) * dtc
    states = jnp.einsum("bcshn,bcsh,bcshp->bchpn", bh, w_end, xc)
    chunk_decay = jnp.exp(acum[:, :, -1, :])

    def carry_state(h, inp):
        st, dec = inp
        return h * dec[:, :, None, None] + st, h

    h_last, h_start = lax.scan(carry_state, h0, (states.transpose(1, 0, 2, 3, 4), chunk_decay.transpose(1, 0, 2)))
    y = y + jnp.einsum("bclhn,cbhpn,bclh->bclhp", ch, h_start, jnp.exp(acum))
    return y.reshape(bsz, t, nh, hp), h_last


def attn_ssd_mixer(h, pos, pos0, kv_past, conv_prev, ssm_prev, in_w, out_w, conv_w, conv_b, dt_bias, a_log, d_skip, norm_w):
    bsz, t, _ = h.shape
    proj = h @ in_w
    q, k, v, z, xbc, dt_raw = jnp.split(proj, MIX_SPLITS, axis=-1)
    q = rope_partial(q.reshape(bsz, t, ATT_HEADS, HEAD_DIM), pos)
    k = rope_partial(k.reshape(bsz, t, ATT_HEADS, HEAD_DIM), pos)
    v = v.reshape(bsz, t, ATT_HEADS, HEAD_DIM)
    if kv_past is None:
        k_parts, v_parts = [k], [v]
    else:
        k_parts, v_parts = [kv_past[0], k], [kv_past[1], v]
    att = moba_attention(q, k_parts, v_parts, pos0).reshape(bsz, t, ATT_WIDTH)
    xbc, conv_new = causal_dwconv(xbc, conv_prev, conv_w, conv_b)
    xbc = jax.nn.silu(xbc)
    xs, bm, cm = jnp.split(xbc, [SSD_INNER, SSD_INNER + SSD_GROUPS * SSD_STATE], axis=-1)
    xs = xs.reshape(bsz, t, SSD_HEADS, SSD_HEADDIM).astype(jnp.float32)
    bm = bm.reshape(bsz, t, SSD_GROUPS, SSD_STATE).astype(jnp.float32)
    cm = cm.reshape(bsz, t, SSD_GROUPS, SSD_STATE).astype(jnp.float32)
    dt = jax.nn.softplus(dt_raw.astype(jnp.float32) + dt_bias.astype(jnp.float32))
    a = -jnp.exp(a_log.astype(jnp.float32))
    y, ssm_new = ssd_scan(xs, dt, a, bm, cm, ssm_prev)
    y = y + d_skip.astype(jnp.float32)[:, None] * xs
    y = rmsnorm(y.reshape(bsz, t, SSD_INNER) * jax.nn.silu(z.astype(jnp.float32)), norm_w).astype(h.dtype)
    out = jnp.concatenate([att, y], axis=-1) @ out_w
    return out, k, v, conv_new, ssm_new


def pool_mixer(u, prev, pos, w, b, scale):
    bsz, t, d = u.shape
    xx = jnp.concatenate([prev.astype(u.dtype), u], axis=1)
    cs = jnp.cumsum(xx.astype(jnp.float32), axis=1)
    cs = jnp.concatenate([jnp.zeros((bsz, 1, d), jnp.float32), cs], axis=1)
    uf = u.astype(jnp.float32)
    groups = []
    for g, win in enumerate(POOL_WINDOWS):
        lo, hi = g * POOL_GROUP_DIM, (g + 1) * POOL_GROUP_DIM
        wsum = cs[:, POOL_MAX:POOL_MAX + t, lo:hi] - cs[:, POOL_MAX - win:POOL_MAX - win + t, lo:hi]
        cnt = jnp.minimum(pos + 1, win).astype(jnp.float32)[None, :, None]
        groups.append(wsum / cnt - uf[:, :, lo:hi])
    pooled = jnp.stack(groups, axis=2)
    y = jnp.einsum("btgc,gcd->btgd", pooled, w.astype(jnp.float32)) + b.astype(jnp.float32)
    y = y.reshape(bsz, t, d) * scale.astype(jnp.float32)
    return y.astype(u.dtype), xx[:, xx.shape[1] - (POOL_MAX - 1):]


def conv_ffn(u, prev, up_w, conv_w, conv_b, down_w):
    hdn, new_prev = causal_dwconv(u @ up_w, prev, conv_w, conv_b)
    g, v = jnp.split(hdn, 2, axis=-1)
    return (jax.nn.silu(g) * v) @ down_w, new_prev


def run_trunk(x, c, pos0, past, w):
    bsz, t, _ = x.shape
    pos = pos0 + jnp.arange(t, dtype=jnp.int32)
    new_k, new_v, new_ssm, new_sconv, new_pool, new_fconv = [], [], [], [], [], []
    for layer in range(DEPTH):
        e = layer // 2
        shift, scale, gate = adaln(c, w["ada_w"][layer, 0], w["ada_b"][layer, 0])
        h = rmsnorm(x, w["norm_pre"][layer, 0]) * (1 + scale) + shift
        if layer % 2 == 0:
            if past is None:
                kv_past = None
                conv_prev = jnp.zeros((bsz, SSD_CONV - 1, SSD_CONV_DIM), x.dtype)
                ssm_prev = jnp.zeros((bsz, SSD_HEADS, SSD_HEADDIM, SSD_STATE), jnp.float32)
            else:
                pt = past["page_table"]
                kv_past = (past["cache_k"][e, pt].reshape(bsz, -1, ATT_HEADS, HEAD_DIM),
                           past["cache_v"][e, pt].reshape(bsz, -1, ATT_HEADS, HEAD_DIM))
                conv_prev = past["state_ssd_conv"][e]
                ssm_prev = past["state_ssm"][e].astype(jnp.float32)
            out, k_rows, v_rows, conv_new, ssm_new = attn_ssd_mixer(
                h, pos, pos0, kv_past, conv_prev, ssm_prev, w["mix_in_w"][e], w["mix_out_w"][e],
                w["ssd_conv_w"][e], w["ssd_conv_b"][e], w["ssd_dt_bias"][e], w["ssd_a_log"][e],
                w["ssd_d"][e], w["ssd_norm_w"][e])
            new_k.append(k_rows)
            new_v.append(v_rows)
            new_sconv.append(conv_new)
            new_ssm.append(ssm_new)
        else:
            prev = jnp.zeros((bsz, POOL_MAX - 1, D_MODEL), x.dtype) if past is None else past["state_pool"][e]
            out, pool_new = pool_mixer(h, prev, pos, w["pool_w"][e], w["pool_b"][e], w["pool_scale"][e])
            new_pool.append(pool_new)
        x = x + gate * rmsnorm(out, w["norm_post"][layer, 0])
        shift, scale, gate = adaln(c, w["ada_w"][layer, 1], w["ada_b"][layer, 1])
        h = rmsnorm(x, w["norm_pre"][layer, 1]) * (1 + scale) + shift
        prev = jnp.zeros((bsz, FFN_CONV - 1, 2 * D_FF), x.dtype) if past is None else past["state_ffn_conv"][layer]
        out, f_new = conv_ffn(h, prev, w["ffn_up_w"][layer], w["ffn_conv_w"][layer], w["ffn_conv_b"][layer], w["ffn_down_w"][layer])
        new_fconv.append(f_new)
        x = x + gate * rmsnorm(out, w["norm_post"][layer, 1])
    return (x, jnp.stack(new_k), jnp.stack(new_v), jnp.stack(new_ssm), jnp.stack(new_sconv),
            jnp.stack(new_pool), jnp.stack(new_fconv))


def setup_inputs(seed: int = 0) -> dict:
    key = jax.random.key(seed)
    ks = jax.random.split(key, 40)
    f32 = jnp.float32
    n_pages = PAST_LEN // PAGE_SIZE
    n_phys = (DEC_BATCH * n_pages * 5) // 4
    ne, no = N_EVEN, N_ODD

    def nrm(k, shape, s=1.0):
        return jax.random.normal(k, shape, f32) * s

    page_table = jax.random.permutation(ks[0], n_phys)[: DEC_BATCH * n_pages].reshape(DEC_BATCH, n_pages).astype(jnp.int32)
    dt0 = jnp.exp(jax.random.uniform(ks[1], (ne, SSD_HEADS), f32, math.log(1e-3), math.log(1e-1)))
    return {
        "x_prompt": nrm(ks[2], (BATCH, SEQ, D_MODEL)),
        "x_sample": nrm(ks[3], (DEC_BATCH, DEC_SEQ, D_MODEL)),
        "cache_k": nrm(ks[4], (ne, n_phys, PAGE_SIZE, ATT_HEADS, HEAD_DIM)),
        "cache_v": nrm(ks[5], (ne, n_phys, PAGE_SIZE, ATT_HEADS, HEAD_DIM)),
        "state_ssm": nrm(ks[6], (ne, DEC_BATCH, SSD_HEADS, SSD_HEADDIM, SSD_STATE), 0.5),
        "state_ssd_conv": nrm(ks[7], (ne, DEC_BATCH, SSD_CONV - 1, SSD_CONV_DIM)),
        "state_pool": nrm(ks[8], (no, DEC_BATCH, POOL_MAX - 1, D_MODEL)),
        "state_ffn_conv": nrm(ks[9], (DEPTH, DEC_BATCH, FFN_CONV - 1, 2 * D_FF)),
        "page_table": page_table,
        "c_prompt": nrm(ks[10], (BATCH, D_MODEL)),
        "c_sample": nrm(ks[11], (DEC_BATCH, D_MODEL)),
        "ada_w": nrm(ks[12], (DEPTH, 2, D_MODEL, 3 * D_MODEL), 0.5 * D_MODEL ** -0.5),
        "ada_b": nrm(ks[13], (DEPTH, 2, 3 * D_MODEL), 0.02),
        "norm_pre": 1.0 + nrm(ks[14], (DEPTH, 2, D_MODEL), 0.05),
        "norm_post": 1.0 + nrm(ks[15], (DEPTH, 2, D_MODEL), 0.05),
        "mix_in_w": nrm(ks[16], (ne, D_MODEL, MIX_IN_DIM), D_MODEL ** -0.5),
        "mix_out_w": nrm(ks[17], (ne, MIX_OUT_DIM, D_MODEL), MIX_OUT_DIM ** -0.5),
        "ssd_conv_w": nrm(ks[18], (ne, SSD_CONV, SSD_CONV_DIM), SSD_CONV ** -0.5),
        "ssd_conv_b": nrm(ks[19], (ne, SSD_CONV_DIM), 0.02),
        "ssd_dt_bias": dt0 + jnp.log(-jnp.expm1(-dt0)),
        "ssd_a_log": jnp.log(jax.random.uniform(ks[20], (ne, SSD_HEADS), f32, 1.0, 16.0)),
        "ssd_d": 1.0 + nrm(ks[21], (ne, SSD_HEADS), 0.1),
        "ssd_norm_w": 1.0 + nrm(ks[22], (ne, SSD_INNER), 0.05),
        "pool_w": nrm(ks[23], (no, POOL_GROUPS, POOL_GROUP_DIM, POOL_GROUP_DIM), POOL_GROUP_DIM ** -0.5),
        "pool_b": nrm(ks[24], (no, POOL_GROUPS, POOL_GROUP_DIM), 0.02),
        "pool_scale": 1.0 + nrm(ks[25], (no, D_MODEL), 0.1),
        "ffn_up_w": nrm(ks[26], (DEPTH, D_MODEL, 2 * D_FF), D_MODEL ** -0.5),
        "ffn_conv_w": nrm(ks[27], (DEPTH, FFN_CONV, 2 * D_FF), FFN_CONV ** -0.5),
        "ffn_conv_b": nrm(ks[28], (DEPTH, 2 * D_FF), 0.02),
        "ffn_down_w": nrm(ks[29], (DEPTH, D_FF, D_MODEL), D_FF ** -0.5),
    }


def reference(x_prompt, x_sample, cache_k, cache_v, state_ssm, state_ssd_conv, state_pool, state_ffn_conv, page_table,
              c_prompt, c_sample, ada_w, ada_b, norm_pre, norm_post, mix_in_w, mix_out_w, ssd_conv_w, ssd_conv_b,
              ssd_dt_bias, ssd_a_log, ssd_d, ssd_norm_w, pool_w, pool_b, pool_scale, ffn_up_w, ffn_conv_w,
              ffn_conv_b, ffn_down_w):
    weights = {"ada_w": ada_w, "ada_b": ada_b, "norm_pre": norm_pre, "norm_post": norm_post,
               "mix_in_w": mix_in_w, "mix_out_w": mix_out_w, "ssd_conv_w": ssd_conv_w, "ssd_conv_b": ssd_conv_b,
               "ssd_dt_bias": ssd_dt_bias, "ssd_a_log": ssd_a_log, "ssd_d": ssd_d, "ssd_norm_w": ssd_norm_w,
               "pool_w": pool_w, "pool_b": pool_b, "pool_scale": pool_scale, "ffn_up_w": ffn_up_w,
               "ffn_conv_w": ffn_conv_w, "ffn_conv_b": ffn_conv_b, "ffn_down_w": ffn_down_w}
    past = {"cache_k": cache_k, "cache_v": cache_v, "page_table": page_table, "state_ssm": state_ssm,
            "state_ssd_conv": state_ssd_conv, "state_pool": state_pool, "state_ffn_conv": state_ffn_conv}
    past_len = page_table.shape[1] * cache_k.shape[2]
    y_prompt, k_p, v_p, ssm_p, sconv_p, pool_p, fconv_p = run_trunk(x_prompt, c_prompt, 0, None, weights)
    y_sample, k_s, v_s, ssm_s, sconv_s, pool_s, fconv_s = run_trunk(x_sample, c_sample, past_len, past, weights)
    return (y_prompt, y_sample, k_p, v_p, ssm_p, sconv_p, pool_p, fconv_p, k_s, v_s, ssm_s, sconv_s, pool_s, fconv_s)
```

```python
import functools
import math

import jax
import jax.numpy as jnp
from jax import lax
from jax.experimental import pallas as pl
from jax.experimental.pallas import tpu as pltpu

F32 = jnp.float32
BF16 = jnp.bfloat16
HIGHEST = lax.Precision.HIGHEST

D_MODEL = 1024
ATT_HEADS = 8
HEAD_DIM = 64
ATT_WIDTH = ATT_HEADS * HEAD_DIM
ROT_DIM = HEAD_DIM // 4
ROPE_THETA = 500000.0
MOBA_BLOCK = 256
MOBA_TOPK = 3
SSD_HEADS = 8
SSD_HEADDIM = 64
SSD_INNER = SSD_HEADS * SSD_HEADDIM
SSD_GROUPS = 2
SSD_STATE = 128
SSD_CONV = 4
SSD_CONV_DIM = SSD_INNER + 2 * SSD_GROUPS * SSD_STATE
POOL_WINDOWS = (2, 4, 8, 16)
POOL_GROUP_DIM = D_MODEL // len(POOL_WINDOWS)
POOL_MAX = 16
D_FF = 2816
FFN_CONV = 3
EPS = 1e-6
PAGE_SIZE = 128
MAIN_COLS = 3 * ATT_WIDTH + SSD_INNER + SSD_CONV_DIM

LANES = 128
SUBLANES = 8
VMEM_LIMIT_BYTES = 56 * 1024 * 1024

ROW_TILE = 512
ATTN_Q_TILE = 256
SSD_CHUNK = 128
FFN_CHUNK = 256
PAGES_PER_STEP = 16

NEG_INF = float("-inf")


def _cparams(n_axes):
    return pltpu.CompilerParams(dimension_semantics=("arbitrary",) * n_axes,
                                vmem_limit_bytes=VMEM_LIMIT_BYTES)


def _silu(x):
    return x / (1.0 + jnp.exp(-x))


def _softplus(x):
    return jnp.maximum(x, 0.0) + jnp.log1p(jnp.exp(-jnp.abs(x)))


def _rms(x, g):
    return x * lax.rsqrt(jnp.mean(x * x, axis=-1, keepdims=True) + EPS) * g


def _norm_mod(x, g, shift, scale):
    return _rms(x, g) * (1.0 + scale) + shift


def _dot_nt(a, b, precision=None):
    return lax.dot_general(a, b, (((1,), (1,)), ((), ())), preferred_element_type=F32, precision=precision)


def _dot_tn(a, b):
    return lax.dot_general(a, b, (((0,), (0,)), ((), ())), preferred_element_type=F32)


def _rope_tables(pos, shape):
    lane = lax.broadcasted_iota(jnp.int32, shape, 1)
    j = lane & (HEAD_DIM - 1)
    half = ROT_DIM // 2
    inv = jnp.exp((lane & (half - 1)).astype(F32) * (-2.0 * math.log(ROPE_THETA) / ROT_DIM))
    ang = pos * inv
    c, s = jnp.cos(ang), jnp.sin(ang)
    cos_t = jnp.where(j < ROT_DIM, c, 1.0)
    sin_t = jnp.where(j < half, -s, jnp.where(j < ROT_DIM, s, 0.0))
    return cos_t, sin_t, j < half


def _rope_group(xg, cos_t, sin_t, low):
    half = ROT_DIM // 2
    partner = jnp.where(low, pltpu.roll(xg, LANES - half, 1), pltpu.roll(xg, half, 1))
    return xg * cos_t + partner * sin_t


def _ada_kernel(c_ref, w_ref, b_ref, o_ref):
    o_ref[...] = jnp.dot(_silu(c_ref[...]), w_ref[...], preferred_element_type=F32, precision=HIGHEST) + b_ref[...]


def _ada_call(c_all, ada_w, ada_b):
    nl, ns, d, d3 = ada_w.shape
    n = c_all.shape[0]
    tn = 1024
    return pl.pallas_call(
        _ada_kernel,
        grid=(nl * ns, d3 // tn),
        in_specs=[pl.BlockSpec((n, d), lambda i, j: (0, 0)),
                  pl.BlockSpec((None, None, d, tn), lambda i, j: (i // ns, i % ns, 0, j)),
                  pl.BlockSpec((None, 1, tn), lambda i, j: (i, 0, j))],
        out_specs=pl.BlockSpec((None, n, tn), lambda i, j: (i, 0, j)),
        out_shape=jax.ShapeDtypeStruct((nl * ns, n, d3), F32),
        compiler_params=_cparams(2), name="ada_mod",
    )(c_all, ada_w, ada_b.reshape(nl * ns, 1, d3))


def _mixin_prompt_kernel(x_ref, sh_ref, sc_ref, g_ref, w_ref, wdt_ref, dtb_ref, cw_ref, cb_ref,
                         q_ref, k_ref, v_ref, z_ref, xs_ref, bc_ref, dt_ref, km_ref, cst_ref,
                         buf_ref, *, tm):
    t = pl.program_id(1)
    h = _norm_mod(x_ref[...], g_ref[...], sh_ref[...], sc_ref[...])
    proj = jnp.dot(h.astype(BF16), w_ref[...], preferred_element_type=F32)

    row = lax.broadcasted_iota(jnp.int32, (tm, LANES), 0)
    cos_t, sin_t, low = _rope_tables((t * tm + row).astype(F32), (tm, LANES))
    for gi in range(ATT_WIDTH // LANES):
        lo = gi * LANES
        q_ref[:, lo:lo + LANES] = _rope_group(proj[:, lo:lo + LANES], cos_t, sin_t, low)
        k_ref[:, lo:lo + LANES] = _rope_group(proj[:, ATT_WIDTH + lo:ATT_WIDTH + lo + LANES], cos_t, sin_t, low)
    v_ref[...] = proj[:, 2 * ATT_WIDTH:3 * ATT_WIDTH]
    z_ref[...] = proj[:, 3 * ATT_WIDTH:3 * ATT_WIDTH + SSD_INNER]
    for blk in range(tm // MOBA_BLOCK):
        kb = k_ref[blk * MOBA_BLOCK:(blk + 1) * MOBA_BLOCK, :]
        km_ref[blk] = jnp.sum(kb, axis=0, keepdims=True) * (1.0 / MOBA_BLOCK)

    raw = proj[:, 3 * ATT_WIDTH + SSD_INNER:MAIN_COLS]

    @pl.when(t == 0)
    def _():
        buf_ref[0:SUBLANES, :] = jnp.zeros((SUBLANES, SSD_CONV_DIM), F32)

    buf_ref[SUBLANES:SUBLANES + tm, :] = raw
    conv = cb_ref[...] + cw_ref[SSD_CONV - 1:SSD_CONV, :] * raw
    for i in range(1, SSD_CONV):
        conv = conv + cw_ref[SSD_CONV - 1 - i:SSD_CONV - i, :] * buf_ref[SUBLANES - i:SUBLANES - i + tm, :]
    act = _silu(conv)
    xs_ref[...] = act[:, :SSD_INNER]
    bc_ref[...] = act[:, SSD_INNER:]
    cst_ref[...] = raw[tm - SUBLANES:tm, :]
    buf_ref[0:SUBLANES, :] = raw[tm - SUBLANES:tm, :]

    dt_raw = _dot_nt(wdt_ref[...], h, precision=HIGHEST)
    dt_ref[...] = _softplus(dt_raw + dtb_ref[...])


def _mixin_prompt_call(x, shift, scale, g, w_main, wdt_t, dtb_col, conv_w, conv_b):
    b, t, d = x.shape
    tm = min(ROW_TILE, t)
    nt = t // tm
    nbt = tm // MOBA_BLOCK
    row_spec = lambda c: pl.BlockSpec((None, tm, c), lambda bi, ti: (bi, ti, 0))
    mod_spec = pl.BlockSpec((None, 1, d), lambda bi, ti: (bi, 0, 0))
    full = lambda a: pl.BlockSpec(a.shape, lambda bi, ti: (0,) * a.ndim)
    outs = pl.pallas_call(
        functools.partial(_mixin_prompt_kernel, tm=tm),
        grid=(b, nt),
        in_specs=[row_spec(d), mod_spec, mod_spec, full(g), full(w_main), full(wdt_t), full(dtb_col),
                  full(conv_w), full(conv_b)],
        out_specs=[row_spec(ATT_WIDTH), row_spec(ATT_WIDTH), row_spec(ATT_WIDTH), row_spec(SSD_INNER),
                   row_spec(SSD_INNER), row_spec(SSD_CONV_DIM - SSD_INNER),
                   pl.BlockSpec((None, SSD_HEADS, tm), lambda bi, ti: (bi, 0, ti)),
                   pl.BlockSpec((nbt, 1, ATT_WIDTH), lambda bi, ti: (bi * nt + ti, 0, 0)),
                   pl.BlockSpec((None, SUBLANES, SSD_CONV_DIM), lambda bi, ti: (bi, 0, 0))],
        out_shape=[jax.ShapeDtypeStruct((b, t, ATT_WIDTH), F32)] * 3
        + [jax.ShapeDtypeStruct((b, t, SSD_INNER), F32)] * 2
        + [jax.ShapeDtypeStruct((b, t, SSD_CONV_DIM - SSD_INNER), F32),
           jax.ShapeDtypeStruct((b, SSD_HEADS, t), F32),
           jax.ShapeDtypeStruct((b * nt * nbt, 1, ATT_WIDTH), F32),
           jax.ShapeDtypeStruct((b, SUBLANES, SSD_CONV_DIM), F32)],
        scratch_shapes=[pltpu.VMEM((tm + SUBLANES, SSD_CONV_DIM), F32)],
        compiler_params=_cparams(2), name="mixin_prompt",
    )(x, shift, scale, g, w_main, wdt_t, dtb_col, conv_w, conv_b)
    q, k, v, z, xs, bc, dt_t, kmean, cst = outs
    return q, k, v, z, xs, bc, dt_t, kmean.reshape(b, t // MOBA_BLOCK, ATT_WIDTH), cst


def _topk_first(gate, ids, n_valid, axis):
    picks = []
    for r in range(MOBA_TOPK):
        mx = jnp.max(gate, axis=axis, keepdims=True)
        idx = jnp.min(jnp.where(gate == mx, ids, jnp.int32(2 ** 30)), axis=axis, keepdims=True)
        idx = jnp.where(r < n_valid, idx, -1)
        picks.append(idx)
        gate = jnp.where(ids == idx, NEG_INF, gate)
    return picks


def _attn_prompt_kernel(q_ref, k_ref, v_ref, km_ref, o_ref, *, tq, nb):
    i = pl.program_id(2)
    jown = (i * tq) // MOBA_BLOCK
    scale = 1.0 / math.sqrt(HEAD_DIM)
    q = q_ref[...]
    lane = lax.broadcasted_iota(jnp.int32, (tq, LANES), 1)
    head_lanes = [lane < HEAD_DIM, lane >= HEAD_DIM]
    blk_ids = lax.broadcasted_iota(jnp.int32, (tq, nb), 1)
    km = km_ref[...]

    qs, sels = [], []
    for hm in head_lanes:
        qh = jnp.where(hm, q, 0.0)
        gate = _dot_nt(qh, km, precision=HIGHEST)
        gate = jnp.where(blk_ids < jown, gate, NEG_INF)
        sels.append(_topk_first(gate, blk_ids, jown, 1))
        qs.append((qh * scale).astype(BF16))

    r0 = pl.multiple_of(i * tq, tq)
    kb = k_ref[pl.ds(r0, tq), :].astype(BF16)
    vb = v_ref[pl.ds(r0, tq), :].astype(BF16)
    ri = lax.broadcasted_iota(jnp.int32, (tq, tq), 0)
    ci = lax.broadcasted_iota(jnp.int32, (tq, tq), 1)
    causal = ci <= ri
    carry = []
    for hq in qs:
        s = jnp.where(causal, _dot_nt(hq, kb), NEG_INF)
        m = jnp.max(s, axis=1, keepdims=True)
        p = jnp.exp(s - m)
        carry += [m, jnp.sum(p, axis=1, keepdims=True), jnp.dot(p.astype(BF16), vb, preferred_element_type=F32)]

    def body(j, carry):
        c0 = pl.multiple_of(j * MOBA_BLOCK, MOBA_BLOCK)
        kj = k_ref[pl.ds(c0, MOBA_BLOCK), :].astype(BF16)
        vj = v_ref[pl.ds(c0, MOBA_BLOCK), :].astype(BF16)
        out = []
        for hi, hq in enumerate(qs):
            m, l, acc = carry[3 * hi:3 * hi + 3]
            s0, s1, s2 = sels[hi]
            chosen = (s0 == j) | (s1 == j) | (s2 == j)
            s = jnp.where(chosen, _dot_nt(hq, kj), NEG_INF)
            m_new = jnp.maximum(m, jnp.max(s, axis=1, keepdims=True))
            alpha = jnp.exp(m - m_new)
            p = jnp.exp(s - m_new)
            out += [m_new, alpha * l + jnp.sum(p, axis=1, keepdims=True),
                    alpha * acc + jnp.dot(p.astype(BF16), vj, preferred_element_type=F32)]
        return tuple(out)

    carry = lax.fori_loop(0, jown, body, tuple(carry))
    o0 = carry[2] / carry[1]
    o1 = carry[5] / carry[4]
    o_ref[...] = jnp.where(head_lanes[0], o0, o1)


def _attn_prompt_call(q, k, v, kmean):
    b, t, _ = q.shape
    tq = min(ATTN_Q_TILE, t)
    nb = kmean.shape[1]
    npair = ATT_WIDTH // LANES
    return pl.pallas_call(
        functools.partial(_attn_prompt_kernel, tq=tq, nb=nb),
        grid=(b, npair, t // tq),
        in_specs=[pl.BlockSpec((None, tq, LANES), lambda bi, hp, i: (bi, i, hp)),
                  pl.BlockSpec((None, t, LANES), lambda bi, hp, i: (bi, 0, hp)),
                  pl.BlockSpec((None, t, LANES), lambda bi, hp, i: (bi, 0, hp)),
                  pl.BlockSpec((None, nb, LANES), lambda bi, hp, i: (bi, 0, hp))],
        out_specs=pl.BlockSpec((None, tq, LANES), lambda bi, hp, i: (bi, i, hp)),
        out_shape=jax.ShapeDtypeStruct((b, t, ATT_WIDTH), F32),
        compiler_params=_cparams(3), name="moba_prompt",
    )(q, k, v, kmean)


def _gated_norm(y, xs, z, dsk, nw):
    yg = (y + dsk * xs) * _silu(z)
    return _rms(yg, nw)


def _ssd_prompt_kernel(xs_ref, bc_ref, dt_ref, z_ref, alog_ref, dsk_ref, nw_ref, y_ref, st_ref, h_scr, *, L):
    c = pl.program_id(1)

    @pl.when(c == 0)
    def _():
        h_scr[...] = jnp.zeros(h_scr.shape, F32)

    dt = dt_ref[...]
    a = -jnp.exp(alog_ref[...])
    lane = lax.broadcasted_iota(jnp.int32, dt.shape, 1)
    acum = dt * a
    s = 1
    while s < L:
        acum = acum + jnp.where(lane >= s, pltpu.roll(acum, s, 1), 0.0)
        s *= 2
    rows = jnp.concatenate([acum, dt, jnp.zeros((L - 2 * SSD_HEADS, L), F32)], axis=0)
    cols = rows.T

    xs = xs_ref[...]
    bc = bc_ref[...]
    ri = lax.broadcasted_iota(jnp.int32, (L, L), 0)
    ci = lax.broadcasted_iota(jnp.int32, (L, L), 1)
    causal = ri >= ci
    low = lax.broadcasted_iota(jnp.int32, (L, LANES), 1) < SSD_HEADDIM
    low_rows = lax.broadcasted_iota(jnp.int32, (LANES, SSD_STATE), 0) < SSD_HEADDIM
    gs = SSD_GROUPS * SSD_STATE
    heads_per_group = SSD_HEADS // SSD_GROUPS
    ys = []
    for g in range(SSD_GROUPS):
        bg = bc[:, g * SSD_STATE:(g + 1) * SSD_STATE].astype(BF16)
        cg = bc[:, gs + g * SSD_STATE:gs + (g + 1) * SSD_STATE].astype(BF16)
        cbt = _dot_nt(cg, bg)
        for pr in range(heads_per_group // 2):
            hp = g * (heads_per_group // 2) + pr
            xp = xs[:, hp * LANES:(hp + 1) * LANES]
            xpb = xp.astype(BF16)
            hst = h_scr[hp * LANES:(hp + 1) * LANES, :]
            inter = _dot_nt(cg, hst.astype(BF16))
            intra, ecols, wcols, cds = [], [], [], []
            for hh in range(2):
                hd = 2 * hp + hh
                col = cols[:, hd:hd + 1]
                row = acum[hd:hd + 1, :]
                decay = jnp.exp(jnp.where(causal, col - row, NEG_INF))
                m = cbt * decay * dt[hd:hd + 1, :]
                intra.append(jnp.dot(m.astype(BF16), xpb, preferred_element_type=F32))
                last = acum[hd:hd + 1, L - 1:L]
                ecols.append(jnp.exp(col))
                wcols.append(jnp.exp(last - col) * cols[:, SSD_HEADS + hd:SSD_HEADS + hd + 1])
                cds.append(jnp.exp(last))
            ys.append(jnp.where(low, intra[0], intra[1]) + inter * jnp.where(low, ecols[0], ecols[1]))
            xw = (xp * jnp.where(low, wcols[0], wcols[1])).astype(BF16)
            upd = _dot_tn(xw, bg)
            h_scr[hp * LANES:(hp + 1) * LANES, :] = hst * jnp.where(low_rows, cds[0], cds[1]) + upd
    y = jnp.concatenate(ys, axis=1)
    y_ref[...] = _gated_norm(y, xs, z_ref[...], dsk_ref[...], nw_ref[...])
    st_ref[...] = h_scr[...]


def _ssd_prompt_call(xs, bc, dt_t, z, alog_col, dsk_row, nw_row):
    b, t, _ = xs.shape
    L = SSD_CHUNK
    row_spec = lambda c: pl.BlockSpec((None, L, c), lambda bi, ci: (bi, ci, 0))
    full = lambda a: pl.BlockSpec(a.shape, lambda bi, ci: (0,) * a.ndim)
    return pl.pallas_call(
        functools.partial(_ssd_prompt_kernel, L=L),
        grid=(b, t // L),
        in_specs=[row_spec(SSD_INNER), row_spec(SSD_CONV_DIM - SSD_INNER),
                  pl.BlockSpec((None, SSD_HEADS, L), lambda bi, ci: (bi, 0, ci)),
                  row_spec(SSD_INNER), full(alog_col), full(dsk_row), full(nw_row)],
        out_specs=[row_spec(SSD_INNER),
                   pl.BlockSpec((None, SSD_INNER, SSD_STATE), lambda bi, ci: (bi, 0, 0))],
        out_shape=[jax.ShapeDtypeStruct((b, t, SSD_INNER), F32),
                   jax.ShapeDtypeStruct((b, SSD_INNER, SSD_STATE), F32)],
        scratch_shapes=[pltpu.VMEM((SSD_INNER, SSD_STATE), F32)],
        compiler_params=_cparams(2), name="ssd_prompt",
    )(xs, bc, dt_t, z, alog_col, dsk_row, nw_row)


def _mixout_kernel(att_ref, y_ref, x_ref, gt_ref, gp_ref, wa_ref, wy_ref, o_ref):
    out = (jnp.dot(att_ref[...].astype(BF16), wa_ref[...], preferred_element_type=F32)
           + jnp.dot(y_ref[...].astype(BF16), wy_ref[...], preferred_element_type=F32))
    o_ref[...] = x_ref[...] + gt_ref[...] * _rms(out, gp_ref[...])


def _mixout_call(att, y, x, gate, gp, wa, wy, *, per_seq_mod):
    b, t, d = x.shape
    tm = min(ROW_TILE, t)
    row_spec = lambda c: pl.BlockSpec((None, tm, c), lambda bi, ti: (bi, ti, 0))
    mod_spec = (pl.BlockSpec((None, 1, d), lambda bi, ti: (bi, 0, 0)) if per_seq_mod
                else pl.BlockSpec((None, tm, d), lambda bi, ti: (bi, ti, 0)))
    full = lambda a: pl.BlockSpec(a.shape, lambda bi, ti: (0,) * a.ndim)
    return pl.pallas_call(
        _mixout_kernel,
        grid=(b, t // tm),
        in_specs=[row_spec(ATT_WIDTH), row_spec(SSD_INNER), row_spec(d), mod_spec, full(gp), full(wa), full(wy)],
        out_specs=row_spec(d),
        out_shape=jax.ShapeDtypeStruct((b, t, d), F32),
        compiler_params=_cparams(2), name="mix_out",
    )(att, y, x, gate, gp, wa, wy)


def _ffn_kernel(*refs, tm, decode):
    if decode:
        (x_ref, sh_ref, sc_ref, gt_ref, g_ref, gp_ref, up_ref, cw_ref, cb_ref, dn_ref, p0_ref, p1_ref,
         o_ref, u_ref, acc_ref, hb_ref) = refs
    else:
        (x_ref, sh_ref, sc_ref, gt_ref, g_ref, gp_ref, up_ref, cw_ref, cb_ref, dn_ref,
         o_ref, cst_ref, carry_ref, ubuf_ref, acc_ref, hb_ref) = refs
        t = pl.program_id(1)

        @pl.when(t == 0)
        def _():
            carry_ref[...] = jnp.zeros(carry_ref.shape, F32)

    x = x_ref[...]
    hb_ref[...] = _norm_mod(x, g_ref[...], sh_ref[...], sc_ref[...]).astype(BF16)
    acc_ref[...] = jnp.zeros(acc_ref.shape, F32)
    fc = FFN_CHUNK

    def chunk(j, _):
        hb = hb_ref[...]
        halves = []
        for half in range(2):
            off = pl.multiple_of(half * D_FF + j * fc, LANES)
            raw = jnp.dot(hb, up_ref[:, pl.ds(off, fc)], preferred_element_type=F32)
            cw = cw_ref[:, pl.ds(off, fc)]
            conv = cb_ref[:, pl.ds(off, fc)] + cw[FFN_CONV - 1:FFN_CONV, :] * raw
            if decode:
                conv = conv + cw[1:2, :] * p1_ref[:, pl.ds(off, fc)] + cw[0:1, :] * p0_ref[:, pl.ds(off, fc)]
                u_ref[:, pl.ds(off, fc)] = raw
            else:
                lo = half * fc
                ubuf_ref[0:SUBLANES, lo:lo + fc] = carry_ref[:, pl.ds(off, fc)]
                ubuf_ref[SUBLANES:SUBLANES + tm, lo:lo + fc] = raw
                carry_ref[:, pl.ds(off, fc)] = raw[tm - SUBLANES:tm, :]
                for i in range(1, FFN_CONV):
                    conv = conv + (cw[FFN_CONV - 1 - i:FFN_CONV - i, :]
                                   * ubuf_ref[SUBLANES - i:SUBLANES - i + tm, lo:lo + fc])
            halves.append(conv)
        act = _silu(halves[0]) * halves[1]
        r0 = pl.multiple_of(j * fc, fc)
        acc_ref[...] += jnp.dot(act.astype(BF16), dn_ref[pl.ds(r0, fc), :], preferred_element_type=F32)
        return 0

    lax.fori_loop(0, D_FF // fc, chunk, 0)
    o_ref[...] = x + gt_ref[...] * _rms(acc_ref[...], gp_ref[...])
    if not decode:
        cst_ref[...] = carry_ref[...]


def _ffn_call(x, shift, scale, gate, g, gp, up_w, conv_w, conv_b, dn_w, prev=None):
    b, t, d = x.shape
    decode = prev is not None
    tm = min(ROW_TILE, t)
    row_spec = lambda c: pl.BlockSpec((None, tm, c), lambda bi, ti: (bi, ti, 0))
    mod_spec = (pl.BlockSpec((None, tm, d), lambda bi, ti: (bi, ti, 0)) if decode
                else pl.BlockSpec((None, 1, d), lambda bi, ti: (bi, 0, 0)))
    full = lambda a: pl.BlockSpec(a.shape, lambda bi, ti: (0,) * a.ndim)
    in_specs = [row_spec(d), mod_spec, mod_spec, mod_spec, full(g), full(gp), full(up_w), full(conv_w),
                full(conv_b), full(dn_w)]
    args = [x, shift, scale, gate, g, gp, up_w, conv_w, conv_b, dn_w]
    scratch = [pltpu.VMEM((tm, d), F32), pltpu.VMEM((tm, d), BF16)]
    if decode:
        assert b == 1 and t == tm
        in_specs += [pl.BlockSpec((tm, 2 * D_FF), lambda bi, ti: (ti, 0))] * 2
        args += list(prev)
        out_specs = [row_spec(d), pl.BlockSpec((tm, 2 * D_FF), lambda bi, ti: (ti, 0))]
        out_shape = [jax.ShapeDtypeStruct((b, t, d), F32), jax.ShapeDtypeStruct((t, 2 * D_FF), F32)]
    else:
        out_specs = [row_spec(d), pl.BlockSpec((None, SUBLANES, 2 * D_FF), lambda bi, ti: (bi, 0, 0))]
        out_shape = [jax.ShapeDtypeStruct((b, t, d), F32), jax.ShapeDtypeStruct((b, SUBLANES, 2 * D_FF), F32)]
        scratch = [pltpu.VMEM((SUBLANES, 2 * D_FF), F32), pltpu.VMEM((tm + SUBLANES, 2 * FFN_CHUNK), F32)] + scratch
    return pl.pallas_call(
        functools.partial(_ffn_kernel, tm=tm, decode=decode),
        grid=(b, t // tm), in_specs=in_specs, out_specs=out_specs, out_shape=out_shape,
        scratch_shapes=scratch, compiler_params=_cparams(2),
        name="conv_ffn_sample" if decode else "conv_ffn_prompt",
    )(*args)


def _pool_kernel(*refs, tm, decode, pos0):
    if decode:
        (x_ref, sh_ref, sc_ref, gt_ref, g_ref, gp_ref, pw_ref, pb_ref, ps_ref, prev_ref, o_ref, h_ref) = refs
    else:
        (x_ref, sh_ref, sc_ref, gt_ref, g_ref, gp_ref, pw_ref, pb_ref, ps_ref, o_ref, last_ref, buf_ref) = refs
        t = pl.program_id(1)

        @pl.when(t == 0)
        def _():
            buf_ref[0:POOL_MAX, :] = jnp.zeros((POOL_MAX, D_MODEL), F32)

    x = x_ref[...]
    h = _norm_mod(x, g_ref[...], sh_ref[...], sc_ref[...])
    if decode:
        h_ref[...] = h
    else:
        buf_ref[POOL_MAX:POOL_MAX + tm, :] = h
        row = lax.broadcasted_iota(jnp.int32, (tm, 1), 0)
        pos = pos0 + t * tm + row
    ys = []
    for gi, win in enumerate(POOL_WINDOWS):
        lo = gi * POOL_GROUP_DIM
        hg = h[:, lo:lo + POOL_GROUP_DIM]
        wsum = hg
        for i in range(1, win):
            if decode:
                wsum = wsum + prev_ref[POOL_MAX - 1 - i, :, lo:lo + POOL_GROUP_DIM]
            else:
                wsum = wsum + buf_ref[POOL_MAX - i:POOL_MAX - i + tm, lo:lo + POOL_GROUP_DIM]
        if decode:
            cnt = float(min(pos0 + 1, win))
        else:
            cnt = jnp.minimum(pos + 1, win).astype(F32)
        pooled = wsum / cnt - hg
        ys.append(jnp.dot(pooled.astype(BF16), pw_ref[gi], preferred_element_type=F32))
    y = (jnp.concatenate(ys, axis=1) + pb_ref[...]) * ps_ref[...]
    o_ref[...] = x + gt_ref[...] * _rms(y, gp_ref[...])
    if not decode:
        last_ref[...] = h[tm - POOL_MAX:tm, :]
        buf_ref[0:POOL_MAX, :] = h[tm - POOL_MAX:tm, :]


def _pool_call(x, shift, scale, gate, g, gp, pw, pb, ps, prev_t=None, pos0=0):
    b, t, d = x.shape
    decode = prev_t is not None
    tm = min(ROW_TILE, t)
    row_spec = lambda c: pl.BlockSpec((None, tm, c), lambda bi, ti: (bi, ti, 0))
    mod_spec = (pl.BlockSpec((None, tm, d), lambda bi, ti: (bi, ti, 0)) if decode
                else pl.BlockSpec((None, 1, d), lambda bi, ti: (bi, 0, 0)))
    full = lambda a: pl.BlockSpec(a.shape, lambda bi, ti: (0,) * a.ndim)
    in_specs = [row_spec(d), mod_spec, mod_spec, mod_spec, full(g), full(gp), full(pw), full(pb), full(ps)]
    args = [x, shift, scale, gate, g, gp, pw, pb, ps]
    if decode:
        assert b == 1 and t == tm
        in_specs.append(full(prev_t))
        args.append(prev_t)
        out_specs = [row_spec(d), row_spec(d)]
        out_shape = [jax.ShapeDtypeStruct((b, t, d), F32)] * 2
        scratch = []
    else:
        out_specs = [row_spec(d), pl.BlockSpec((None, POOL_MAX, d), lambda bi, ti: (bi, 0, 0))]
        out_shape = [jax.ShapeDtypeStruct((b, t, d), F32), jax.ShapeDtypeStruct((b, POOL_MAX, d), F32)]
        scratch = [pltpu.VMEM((tm + POOL_MAX, d), F32)]
    return pl.pallas_call(
        functools.partial(_pool_kernel, tm=tm, decode=decode, pos0=pos0),
        grid=(b, t // tm), in_specs=in_specs, out_specs=out_specs, out_shape=out_shape,
        scratch_shapes=scratch, compiler_params=_cparams(2),
        name="pool_sample" if decode else "pool_prompt",
    )(*args)


def _mixin_sample_kernel(x_ref, sh_ref, sc_ref, g_ref, w_ref, wdtx_ref, dtbx_ref, cw_ref, cb_ref, prev_ref,
                         q_ref, k_ref, v_ref, z_ref, xs_ref, bc_ref, dtx_ref, raw_ref, *, pos0):
    h = _norm_mod(x_ref[...], g_ref[...], sh_ref[...], sc_ref[...])
    proj = jnp.dot(h.astype(BF16), w_ref[...], preferred_element_type=F32)
    n = h.shape[0]
    cos_t, sin_t, low = _rope_tables(jnp.float32(pos0), (n, LANES))
    for gi in range(ATT_WIDTH // LANES):
        lo = gi * LANES
        q_ref[:, lo:lo + LANES] = _rope_group(proj[:, lo:lo + LANES], cos_t, sin_t, low)
        k_ref[:, lo:lo + LANES] = _rope_group(proj[:, ATT_WIDTH + lo:ATT_WIDTH + lo + LANES], cos_t, sin_t, low)
    v_ref[...] = proj[:, 2 * ATT_WIDTH:3 * ATT_WIDTH]
    z_ref[...] = proj[:, 3 * ATT_WIDTH:3 * ATT_WIDTH + SSD_INNER]
    raw = proj[:, 3 * ATT_WIDTH + SSD_INNER:MAIN_COLS]
    raw_ref[...] = raw
    conv = cb_ref[...] + cw_ref[SSD_CONV - 1:SSD_CONV, :] * raw
    for i in range(SSD_CONV - 1):
        conv = conv + cw_ref[i:i + 1, :] * prev_ref[i]
    act = _silu(conv)
    xs_ref[...] = act[:, :SSD_INNER]
    bc_ref[...] = act[:, SSD_INNER:]
    dtx_ref[...] = _softplus(jnp.dot(h, wdtx_ref[...], preferred_element_type=F32, precision=HIGHEST) + dtbx_ref[...])


def _mixin_sample_call(x, shift, scale, g, w_main, wdt_x, dtb_x, conv_w, conv_b, prev_t, pos0):
    n, d = x.shape
    full = lambda a: pl.BlockSpec(a.shape, lambda i: (0,) * a.ndim)
    args = [x, shift, scale, g, w_main, wdt_x, dtb_x, conv_w, conv_b, prev_t]
    widths = [ATT_WIDTH] * 3 + [SSD_INNER] * 2 + [SSD_CONV_DIM - SSD_INNER, SSD_INNER, SSD_CONV_DIM]
    return pl.pallas_call(
        functools.partial(_mixin_sample_kernel, pos0=pos0),
        grid=(1,),
        in_specs=[full(a) for a in args],
        out_specs=[pl.BlockSpec((n, c), lambda i: (0, 0)) for c in widths],
        out_shape=[jax.ShapeDtypeStruct((n, c), F32) for c in widths],
        compiler_params=_cparams(1), name="mixin_sample",
    )(*args)


def _past_kmean_kernel(pt_ref, *refs):
    pages, o_ref = refs[:PAGES_PER_STEP], refs[PAGES_PER_STEP]
    pages_per_block = MOBA_BLOCK // PAGE_SIZE
    for blk in range(PAGES_PER_STEP // pages_per_block):
        s = jnp.sum(pages[blk * pages_per_block][...], axis=0)
        for r in range(1, pages_per_block):
            s = s + jnp.sum(pages[blk * pages_per_block + r][...], axis=0)
        o_ref[blk] = s * (1.0 / MOBA_BLOCK)


def _past_kmean_call(pt_flat, cache_k, n_seq, n_pages):
    steps = n_pages // PAGES_PER_STEP
    blocks_per_step = PAGES_PER_STEP * PAGE_SIZE // MOBA_BLOCK

    def page_map(b, s, pt, *, r):
        return (0, pt[b * n_pages + s * PAGES_PER_STEP + r], 0, 0, 0)

    grid_spec = pltpu.PrefetchScalarGridSpec(
        num_scalar_prefetch=1, grid=(n_seq, steps),
        in_specs=[pl.BlockSpec((None, None, PAGE_SIZE, ATT_HEADS, HEAD_DIM), functools.partial(page_map, r=r))
                  for r in range(PAGES_PER_STEP)],
        out_specs=pl.BlockSpec((None, blocks_per_step, ATT_HEADS, HEAD_DIM), lambda b, s, pt: (b, s, 0, 0)))
    return pl.pallas_call(
        _past_kmean_kernel, grid_spec=grid_spec,
        out_shape=jax.ShapeDtypeStruct((n_seq, steps * blocks_per_step, ATT_HEADS, HEAD_DIM), F32),
        compiler_params=_cparams(2), name="past_kmean",
    )(pt_flat, *([cache_k] * PAGES_PER_STEP))


def _sample_select_kernel(q_ref, km_ref, sel_ref, *, jown):
    nb = km_ref.shape[0]
    gate = jnp.sum(km_ref[...] * q_ref[...][None], axis=-1, keepdims=True)
    ids = lax.broadcasted_iota(jnp.int32, (nb, ATT_HEADS, 1), 0)
    gate = jnp.where(ids < jown, gate, NEG_INF)
    picks = _topk_first(gate, ids, jown, 0)
    for r in range(MOBA_TOPK):
        sel_ref[r] = picks[r][0]


def _sample_select_call(q3, kmean, jown):
    n, nb = kmean.shape[:2]
    return pl.pallas_call(
        functools.partial(_sample_select_kernel, jown=jown),
        grid=(n,),
        in_specs=[pl.BlockSpec((None, ATT_HEADS, HEAD_DIM), lambda b: (b, 0, 0)),
                  pl.BlockSpec((None, nb, ATT_HEADS, HEAD_DIM), lambda b: (b, 0, 0, 0))],
        out_specs=pl.BlockSpec((None, MOBA_TOPK, ATT_HEADS, 1), lambda b: (b, 0, 0, 0)),
        out_shape=jax.ShapeDtypeStruct((n, MOBA_TOPK, ATT_HEADS, 1), jnp.int32),
        compiler_params=_cparams(1), name="sample_select",
    )(q3, kmean)


_PAGES_PER_BLOCK = MOBA_BLOCK // PAGE_SIZE
_SEL_PAGES = MOBA_TOPK * _PAGES_PER_BLOCK


def _sample_attn_kernel(pt_ref, sel_ref, q_ref, kn_ref, vn_ref, *refs):
    kp, vp, o_ref = refs[:_SEL_PAGES], refs[_SEL_PAGES:2 * _SEL_PAGES], refs[2 * _SEL_PAGES]
    b, h = pl.program_id(0), pl.program_id(1)
    scale = 1.0 / math.sqrt(HEAD_DIM)
    q = q_ref[...] * scale
    s_self = jnp.sum(q * kn_ref[...], axis=-1, keepdims=True)
    scores = []
    for r in range(MOBA_TOPK):
        valid = sel_ref[(b * MOBA_TOPK + r) * ATT_HEADS + h] >= 0
        for pg in range(_PAGES_PER_BLOCK):
            s = jnp.sum(kp[r * _PAGES_PER_BLOCK + pg][...] * q[None], axis=-1, keepdims=True)
            scores.append(jnp.where(valid, s, NEG_INF))
    m = s_self
    for s in scores:
        m = jnp.maximum(m, jnp.max(s, axis=0))
    p_self = jnp.exp(s_self - m)
    l = p_self
    acc = p_self * vn_ref[...]
    for idx, s in enumerate(scores):
        p = jnp.exp(s - m[None])
        l = l + jnp.sum(p, axis=0)
        acc = acc + jnp.sum(p * vp[idx][...], axis=0)
    own = lax.broadcasted_iota(jnp.int32, (ATT_HEADS, 1), 0) == h
    o_ref[...] = jnp.sum(jnp.where(own, acc / l, 0.0), axis=0, keepdims=True)


def _sample_attn_call(pt_flat, sel_flat, q3, k3, v3, cache_k, cache_v, n_pages):
    n = q3.shape[0]
    head_spec = pl.BlockSpec((None, ATT_HEADS, HEAD_DIM), lambda b, h, pt, sel: (b, 0, 0))

    def page_map(b, h, pt, sel, *, r, pg):
        blk = jnp.maximum(sel[(b * MOBA_TOPK + r) * ATT_HEADS + h], 0)
        return (0, pt[b * n_pages + blk * _PAGES_PER_BLOCK + pg], 0, 0, 0)

    page_specs = [pl.BlockSpec((None, None, PAGE_SIZE, ATT_HEADS, HEAD_DIM), functools.partial(page_map, r=r, pg=pg))
                  for r in range(MOBA_TOPK) for pg in range(_PAGES_PER_BLOCK)]
    grid_spec = pltpu.PrefetchScalarGridSpec(
        num_scalar_prefetch=2, grid=(n, ATT_HEADS),
        in_specs=[head_spec, head_spec, head_spec] + page_specs + page_specs,
        out_specs=pl.BlockSpec((None, 1, HEAD_DIM), lambda b, h, pt, sel: (b * ATT_HEADS + h, 0, 0)))
    out = pl.pallas_call(
        _sample_attn_kernel, grid_spec=grid_spec,
        out_shape=jax.ShapeDtypeStruct((n * ATT_HEADS, 1, HEAD_DIM), F32),
        compiler_params=_cparams(2), name="sample_attn",
    )(pt_flat, sel_flat, q3, k3, v3, *([cache_k] * _SEL_PAGES), *([cache_v] * _SEL_PAGES))
    return out.reshape(n, ATT_WIDTH)


def _ssd_sample_kernel(xs_ref, bc_ref, dtx_ref, z_ref, st_ref, alogx_ref, dsk_ref, nw_ref, y_ref, sto_ref):
    xs, dtx = xs_ref[...], dtx_ref[...]
    decay = jnp.exp(dtx * (-jnp.exp(alogx_ref[...])))
    xdt = xs * dtx
    bc = bc_ref[...]
    rows_per_group = SSD_INNER // SSD_GROUPS
    gs = SSD_GROUPS * SSD_STATE
    eye = (lax.broadcasted_iota(jnp.int32, (rows_per_group, rows_per_group), 0)
           == lax.broadcasted_iota(jnp.int32, (rows_per_group, rows_per_group), 1))

    def to_col(v):
        return jnp.sum(jnp.where(eye, jnp.broadcast_to(v, eye.shape), 0.0), axis=1, keepdims=True)

    def to_row(v):
        return jnp.sum(jnp.where(eye, jnp.broadcast_to(v, eye.shape), 0.0), axis=0, keepdims=True)

    ys = []
    for g in range(SSD_GROUPS):
        lo = g * rows_per_group
        b_row = bc[:, g * SSD_STATE:(g + 1) * SSD_STATE]
        c_row = bc[:, gs + g * SSD_STATE:gs + (g + 1) * SSD_STATE]
        hg = st_ref[lo:lo + rows_per_group, :]
        hnew = to_col(decay[:, lo:lo + rows_per_group]) * hg + to_col(xdt[:, lo:lo + rows_per_group]) * b_row
        sto_ref[lo:lo + rows_per_group, :] = hnew
        ys.append(to_row(jnp.sum(hnew * c_row, axis=1, keepdims=True)))
    y = jnp.concatenate(ys, axis=1)
    y_ref[...] = _gated_norm(y, xs, z_ref[...], dsk_ref[...], nw_ref[...])


def _ssd_sample_call(xs, bc, dtx, z, state, alog_x, dsk_row, nw_row):
    n = xs.shape[0]
    r3 = lambda a: a.reshape(n, 1, a.shape[-1])
    row_spec = lambda c: pl.BlockSpec((None, 1, c), lambda b: (b, 0, 0))
    st_spec = pl.BlockSpec((None, SSD_INNER, SSD_STATE), lambda b: (b, 0, 0))
    full = lambda a: pl.BlockSpec(a.shape, lambda b: (0,) * a.ndim)
    y, st = pl.pallas_call(
        _ssd_sample_kernel,
        grid=(n,),
        in_specs=[row_spec(SSD_INNER), row_spec(SSD_CONV_DIM - SSD_INNER), row_spec(SSD_INNER), row_spec(SSD_INNER),
                  st_spec, full(alog_x), full(dsk_row), full(nw_row)],
        out_specs=[row_spec(SSD_INNER), st_spec],
        out_shape=[jax.ShapeDtypeStruct((n, 1, SSD_INNER), F32), jax.ShapeDtypeStruct((n, SSD_INNER, SSD_STATE), F32)],
        compiler_params=_cparams(1), name="ssd_sample",
    )(r3(xs), r3(bc), r3(dtx), r3(z), state, alog_x, dsk_row, nw_row)
    return y.reshape(n, SSD_INNER), st


def kernel(x_prompt, x_sample, cache_k, cache_v, state_ssm, state_ssd_conv, state_pool, state_ffn_conv, page_table,
           c_prompt, c_sample, ada_w, ada_b, norm_pre, norm_post, mix_in_w, mix_out_w, ssd_conv_w, ssd_conv_b,
           ssd_dt_bias, ssd_a_log, ssd_d, ssd_norm_w, pool_w, pool_b, pool_scale, ffn_up_w, ffn_conv_w,
           ffn_conv_b, ffn_down_w):
    bp, t, d = x_prompt.shape
    ns = x_sample.shape[0]
    n_pages = page_table.shape[1]
    past_len = n_pages * cache_k.shape[2]
    assert x_sample.shape[1] == 1 and cache_k.shape[2] == PAGE_SIZE and past_len % MOBA_BLOCK == 0
    assert t % ROW_TILE == 0 or t < ROW_TILE

    w_main = mix_in_w[0, :, :MAIN_COLS].astype(BF16)
    w_dt = mix_in_w[0, :, MAIN_COLS:]
    wdt_t = w_dt.T
    wdt_x = jnp.repeat(w_dt, SSD_HEADDIM, axis=1)
    dtb_col = ssd_dt_bias[0].reshape(SSD_HEADS, 1)
    dtb_x = jnp.repeat(ssd_dt_bias[0], SSD_HEADDIM).reshape(1, SSD_INNER)
    alog_col = ssd_a_log[0].reshape(SSD_HEADS, 1)
    alog_x = jnp.repeat(ssd_a_log[0], SSD_HEADDIM).reshape(1, SSD_INNER)
    dsk_row = jnp.repeat(ssd_d[0], SSD_HEADDIM).reshape(1, SSD_INNER)
    nw_row = ssd_norm_w[0].reshape(1, SSD_INNER)
    w_att = mix_out_w[0, :ATT_WIDTH].astype(BF16)
    w_ssd = mix_out_w[0, ATT_WIDTH:].astype(BF16)
    conv_w0 = ssd_conv_w[0]
    conv_b0 = ssd_conv_b[0].reshape(1, SSD_CONV_DIM)
    up_w = ffn_up_w.astype(BF16)
    dn_w = ffn_down_w.astype(BF16)
    pw = pool_w[0].astype(BF16)
    pb = pool_b[0].reshape(1, d)
    ps = pool_scale[0].reshape(1, d)

    mods = _ada_call(jnp.concatenate([c_prompt, c_sample], axis=0), ada_w, ada_b)

    def mod(layer, slot, sample):
        m = mods[layer * 2 + slot]
        m = m[bp:] if sample else m[:bp]
        parts = (m[:, :d], m[:, d:2 * d], m[:, 2 * d:])
        return tuple(p[None] for p in parts) if sample else tuple(p[:, None, :] for p in parts)

    npre = lambda layer, slot: norm_pre[layer, slot].reshape(1, d)
    npost = lambda layer, slot: norm_post[layer, slot].reshape(1, d)

    def ffn(x, layer, sample, prev=None):
        sh, sc, gt = mod(layer, 1, sample)
        return _ffn_call(x, sh, sc, gt, npre(layer, 1), npost(layer, 1), up_w[layer], ffn_conv_w[layer],
                         ffn_conv_b[layer].reshape(1, 2 * D_FF), dn_w[layer], prev=prev)

    sh, sc, gt = mod(0, 0, False)
    q, k, v, z, xs, bc, dt_t, kmean, cst = _mixin_prompt_call(
        x_prompt, sh, sc, npre(0, 0), w_main, wdt_t, dtb_col, conv_w0, conv_b0)
    att = _attn_prompt_call(q, k, v, kmean)
    y, ssm_p = _ssd_prompt_call(xs, bc, dt_t, z, alog_col, dsk_row, nw_row)
    xp = _mixout_call(att, y, x_prompt, gt, npost(0, 0), w_att, w_ssd, per_seq_mod=True)
    xp, fst0 = ffn(xp, 0, False)
    sh, sc, gt = mod(1, 0, False)
    xp, plast = _pool_call(xp, sh, sc, gt, npre(1, 0), npost(1, 0), pw, pb, ps)
    xp, fst1 = ffn(xp, 1, False)

    k_p = k.reshape(1, bp, t, ATT_HEADS, HEAD_DIM)
    v_p = v.reshape(1, bp, t, ATT_HEADS, HEAD_DIM)
    ssm_p = ssm_p.reshape(1, bp, SSD_HEADS, SSD_HEADDIM, SSD_STATE)
    sconv_p = cst[None, :, SUBLANES - (SSD_CONV - 1):, :]
    pool_p = plast[None, :, 1:, :]
    fconv_p = jnp.stack([fst0, fst1])[:, :, SUBLANES - (FFN_CONV - 1):, :]

    xs0 = x_sample.reshape(ns, d)
    sh, sc, gt = mod(0, 0, True)
    conv_prev_t = jnp.transpose(state_ssd_conv[0], (1, 0, 2))
    qs, ks, vs, zs, xss, bcs, dtx, raw = _mixin_sample_call(
        xs0, sh[0], sc[0], npre(0, 0), w_main, wdt_x, dtb_x, conv_w0, conv_b0, conv_prev_t, past_len)
    pt_flat = page_table.reshape(-1)
    kmean_s = _past_kmean_call(pt_flat, cache_k, ns, n_pages)
    jown = past_len // MOBA_BLOCK
    by_head = lambda a: a.reshape(ns, ATT_HEADS, HEAD_DIM)
    sel = _sample_select_call(by_head(qs), kmean_s, jown)
    att_s = _sample_attn_call(pt_flat, sel.reshape(-1), by_head(qs), by_head(ks), by_head(vs),
                              cache_k, cache_v, n_pages)
    y_s, ssm_s = _ssd_sample_call(xss, bcs, dtx, zs, state_ssm[0].reshape(ns, SSD_INNER, SSD_STATE),
                                  alog_x, dsk_row, nw_row)
    xq = _mixout_call(att_s[None], y_s[None], xs0[None], gt, npost(0, 0), w_att, w_ssd, per_seq_mod=False)
    xq, u0 = ffn(xq, 0, True, prev=(state_ffn_conv[0, :, 0], state_ffn_conv[0, :, 1]))
    sh, sc, gt = mod(1, 0, True)
    pool_prev_t = jnp.transpose(state_pool[0], (1, 0, 2))
    xq, h_pool = _pool_call(xq, sh, sc, gt, npre(1, 0), npost(1, 0), pw, pb, ps, prev_t=pool_prev_t, pos0=past_len)
    xq, u1 = ffn(xq, 1, True, prev=(state_ffn_conv[1, :, 0], state_ffn_conv[1, :, 1]))

    y_sample = xq.reshape(ns, 1, d)
    k_s = ks.reshape(1, ns, 1, ATT_HEADS, HEAD_DIM)
    v_s = vs.reshape(1, ns, 1, ATT_HEADS, HEAD_DIM)
    ssm_s = ssm_s.reshape(1, ns, SSD_HEADS, SSD_HEADDIM, SSD_STATE)
    sconv_s = jnp.concatenate([state_ssd_conv[0, :, 1:], raw[:, None, :]], axis=1)[None]
    pool_s = jnp.concatenate([state_pool[0, :, 1:], h_pool[0][:, None, :]], axis=1)[None]
    fconv_s = jnp.stack([jnp.stack([state_ffn_conv[0, :, 1], u0], axis=1),
                         jnp.stack([state_ffn_conv[1, :, 1], u1], axis=1)])
    return (xp, y_sample, k_p, v_p, ssm_p, sconv_p, pool_p, fconv_p, k_s, v_s, ssm_s, sconv_s, pool_s, fconv_s)
```

```python
import functools
import math

import jax
import jax.numpy as jnp
from jax import lax
from jax.experimental import pallas as pl
from jax.experimental.pallas import tpu as pltpu

F32 = jnp.float32
BF16 = jnp.bfloat16
HIGHEST = lax.Precision.HIGHEST

D_MODEL = 1024
ATT_HEADS = 8
HEAD_DIM = 64
ATT_WIDTH = ATT_HEADS * HEAD_DIM
ROT_DIM = HEAD_DIM // 4
ROPE_THETA = 500000.0
MOBA_BLOCK = 256
MOBA_TOPK = 3
SSD_HEADS = 8
SSD_HEADDIM = 64
SSD_INNER = SSD_HEADS * SSD_HEADDIM
SSD_GROUPS = 2
SSD_STATE = 128
SSD_CONV = 4
SSD_CONV_DIM = SSD_INNER + 2 * SSD_GROUPS * SSD_STATE
POOL_WINDOWS = (2, 4, 8, 16)
POOL_GROUP_DIM = D_MODEL // len(POOL_WINDOWS)
POOL_MAX = 16
D_FF = 2816
FFN_CONV = 3
EPS = 1e-6
PAGE_SIZE = 128
MAIN_COLS = 3 * ATT_WIDTH + SSD_INNER + SSD_CONV_DIM

LANES = 128
SUBLANES = 8
VMEM_LIMIT_BYTES = 56 * 1024 * 1024

ROW_TILE = 512
ATTN_Q_TILE = 256
SSD_CHUNK = 128
FFN_CHUNK = 256
PAGES_PER_STEP = 16

ATTN_KV_BLOCKS = 4

NEG_INF = float("-inf")
MASKED = -1e30
LOG2_E = 1.4426950408889634


def _cparams(n_axes, flags=None):
    return pltpu.CompilerParams(dimension_semantics=("arbitrary",) * n_axes,
                                vmem_limit_bytes=VMEM_LIMIT_BYTES, flags=flags)


def _silu(x):
    return x / (1.0 + jnp.exp(-x))


def _softplus(x):
    return jnp.maximum(x, 0.0) + jnp.log1p(jnp.exp(-jnp.abs(x)))


def _rms(x, g):
    return x * lax.rsqrt(jnp.mean(x * x, axis=-1, keepdims=True) + EPS) * g


def _norm_mod(x, g, shift, scale):
    return _rms(x, g) * (1.0 + scale) + shift


def _dot_nt(a, b, precision=None):
    return lax.dot_general(a, b, (((1,), (1,)), ((), ())), preferred_element_type=F32, precision=precision)


def _dot_tn(a, b):
    return lax.dot_general(a, b, (((0,), (0,)), ((), ())), preferred_element_type=F32)


def _rope_tables(pos, shape):
    lane = lax.broadcasted_iota(jnp.int32, shape, 1)
    j = lane & (HEAD_DIM - 1)
    half = ROT_DIM // 2
    inv = jnp.exp((lane & (half - 1)).astype(F32) * (-2.0 * math.log(ROPE_THETA) / ROT_DIM))
    ang = pos * inv
    c, s = jnp.cos(ang), jnp.sin(ang)
    cos_t = jnp.where(j < ROT_DIM, c, 1.0)
    sin_t = jnp.where(j < half, -s, jnp.where(j < ROT_DIM, s, 0.0))
    return cos_t, sin_t, j < half


def _rope_group(xg, cos_t, sin_t, low):
    half = ROT_DIM // 2
    partner = jnp.where(low, pltpu.roll(xg, LANES - half, 1), pltpu.roll(xg, half, 1))
    return xg * cos_t + partner * sin_t


def _ada_kernel(c_ref, w_ref, b_ref, o_ref):
    o_ref[...] = jnp.dot(_silu(c_ref[...]), w_ref[...], preferred_element_type=F32, precision=HIGHEST) + b_ref[...]


def _ada_call(c_all, ada_w, ada_b):
    nl, ns, d, d3 = ada_w.shape
    n = c_all.shape[0]
    tn = 1024
    return pl.pallas_call(
        _ada_kernel,
        grid=(nl * ns, d3 // tn),
        in_specs=[pl.BlockSpec((n, d), lambda i, j: (0, 0)),
                  pl.BlockSpec((None, None, d, tn), lambda i, j: (i // ns, i % ns, 0, j)),
                  pl.BlockSpec((None, 1, tn), lambda i, j: (i, 0, j))],
        out_specs=pl.BlockSpec((None, n, tn), lambda i, j: (i, 0, j)),
        out_shape=jax.ShapeDtypeStruct((nl * ns, n, d3), F32),
        compiler_params=_cparams(2), name="ada_mod",
    )(c_all, ada_w, ada_b.reshape(nl * ns, 1, d3))


def _mixin_prompt_kernel(x_ref, sh_ref, sc_ref, g_ref, w_ref, wdt_ref, dtb_ref, cw_ref, cb_ref,
                         q_ref, k_ref, v_ref, z_ref, xs_ref, bc_ref, dt_ref, km_ref, cst_ref,
                         buf_ref, *, tm):
    t = pl.program_id(1)
    h = _norm_mod(x_ref[...], g_ref[...], sh_ref[...], sc_ref[...])
    proj = jnp.dot(h.astype(BF16), w_ref[...], preferred_element_type=F32)

    row = lax.broadcasted_iota(jnp.int32, (tm, LANES), 0)
    cos_t, sin_t, low = _rope_tables((t * tm + row).astype(F32), (tm, LANES))
    for gi in range(ATT_WIDTH // LANES):
        lo = gi * LANES
        q_ref[:, lo:lo + LANES] = _rope_group(proj[:, lo:lo + LANES], cos_t, sin_t, low)
        k_ref[:, lo:lo + LANES] = _rope_group(proj[:, ATT_WIDTH + lo:ATT_WIDTH + lo + LANES], cos_t, sin_t, low)
    v_ref[...] = proj[:, 2 * ATT_WIDTH:3 * ATT_WIDTH]
    z_ref[...] = proj[:, 3 * ATT_WIDTH:3 * ATT_WIDTH + SSD_INNER]
    for blk in range(tm // MOBA_BLOCK):
        kb = k_ref[blk * MOBA_BLOCK:(blk + 1) * MOBA_BLOCK, :]
        km_ref[blk] = jnp.sum(kb, axis=0, keepdims=True) * (1.0 / MOBA_BLOCK)

    raw = proj[:, 3 * ATT_WIDTH + SSD_INNER:MAIN_COLS]

    @pl.when(t == 0)
    def _():
        buf_ref[0:SUBLANES, :] = jnp.zeros((SUBLANES, SSD_CONV_DIM), F32)

    buf_ref[SUBLANES:SUBLANES + tm, :] = raw
    conv = cb_ref[...] + cw_ref[SSD_CONV - 1:SSD_CONV, :] * raw
    for i in range(1, SSD_CONV):
        conv = conv + cw_ref[SSD_CONV - 1 - i:SSD_CONV - i, :] * buf_ref[SUBLANES - i:SUBLANES - i + tm, :]
    act = _silu(conv)
    xs_ref[...] = act[:, :SSD_INNER]
    bc_ref[...] = act[:, SSD_INNER:]
    cst_ref[...] = raw[tm - SUBLANES:tm, :]
    buf_ref[0:SUBLANES, :] = raw[tm - SUBLANES:tm, :]

    dt_raw = _dot_nt(wdt_ref[...], h, precision=HIGHEST)
    dt_ref[...] = _softplus(dt_raw + dtb_ref[...])


def _mixin_prompt_call(x, shift, scale, g, w_main, wdt_t, dtb_col, conv_w, conv_b):
    b, t, d = x.shape
    tm = min(ROW_TILE, t)
    nt = t // tm
    nbt = tm // MOBA_BLOCK
    row_spec = lambda c: pl.BlockSpec((None, tm, c), lambda bi, ti: (bi, ti, 0))
    mod_spec = pl.BlockSpec((None, 1, d), lambda bi, ti: (bi, 0, 0))
    full = lambda a: pl.BlockSpec(a.shape, lambda bi, ti: (0,) * a.ndim)
    outs = pl.pallas_call(
        functools.partial(_mixin_prompt_kernel, tm=tm),
        grid=(b, nt),
        in_specs=[row_spec(d), mod_spec, mod_spec, full(g), full(w_main), full(wdt_t), full(dtb_col),
                  full(conv_w), full(conv_b)],
        out_specs=[row_spec(ATT_WIDTH), row_spec(ATT_WIDTH), row_spec(ATT_WIDTH), row_spec(SSD_INNER),
                   row_spec(SSD_INNER), row_spec(SSD_CONV_DIM - SSD_INNER),
                   pl.BlockSpec((None, SSD_HEADS, tm), lambda bi, ti: (bi, 0, ti)),
                   pl.BlockSpec((nbt, 1, ATT_WIDTH), lambda bi, ti: (bi * nt + ti, 0, 0)),
                   pl.BlockSpec((None, SUBLANES, SSD_CONV_DIM), lambda bi, ti: (bi, 0, 0))],
        out_shape=[jax.ShapeDtypeStruct((b, t, ATT_WIDTH), F32)] * 3
        + [jax.ShapeDtypeStruct((b, t, SSD_INNER), F32)] * 2
        + [jax.ShapeDtypeStruct((b, t, SSD_CONV_DIM - SSD_INNER), F32),
           jax.ShapeDtypeStruct((b, SSD_HEADS, t), F32),
           jax.ShapeDtypeStruct((b * nt * nbt, 1, ATT_WIDTH), F32),
           jax.ShapeDtypeStruct((b, SUBLANES, SSD_CONV_DIM), F32)],
        scratch_shapes=[pltpu.VMEM((tm + SUBLANES, SSD_CONV_DIM), F32)],
        compiler_params=_cparams(2), name="mixin_prompt",
    )(x, shift, scale, g, w_main, wdt_t, dtb_col, conv_w, conv_b)
    q, k, v, z, xs, bc, dt_t, kmean, cst = outs
    return q, k, v, z, xs, bc, dt_t, kmean.reshape(b, t // MOBA_BLOCK, ATT_WIDTH), cst


def _topk_first(gate, ids, n_valid, axis):
    picks = []
    for r in range(MOBA_TOPK):
        mx = jnp.max(gate, axis=axis, keepdims=True)
        idx = jnp.min(jnp.where(gate == mx, ids, 1e9), axis=axis, keepdims=True)
        idx = jnp.where(r < n_valid, idx, -1.0)
        picks.append(idx)
        gate = jnp.where(ids == idx, NEG_INF, gate)
    return picks


def _attn_prompt_kernel(q_ref, k_ref, v_ref, km_ref, o_ref, ka_scr, vt_scr, acc_scr, *, tq, t):
    i = pl.program_id(2)
    kv = ATTN_KV_BLOCKS * MOBA_BLOCK
    nb = km_ref.shape[0]

    @pl.when(i == 0)
    def _():
        blk_lane = lax.broadcasted_iota(jnp.int32, (MOBA_BLOCK, LANES), 1)
        sub = lax.broadcasted_iota(jnp.int32, (2 * SUBLANES, MOBA_BLOCK), 0)
        ones_rows = jnp.where(sub == 0, 1.0, 0.0).astype(BF16)

        def prep(c, _):
            r0 = pl.multiple_of(c * MOBA_BLOCK, MOBA_BLOCK)
            ka_scr[pl.ds(r0, MOBA_BLOCK), 0:LANES] = k_ref[pl.ds(r0, MOBA_BLOCK), :].astype(BF16)
            ka_scr[pl.ds(r0, MOBA_BLOCK), LANES:2 * LANES] = jnp.where(blk_lane == c, 1.0, 0.0).astype(BF16)
            vt_scr[0:LANES, pl.ds(r0, MOBA_BLOCK)] = v_ref[pl.ds(r0, MOBA_BLOCK), :].T.astype(BF16)
            vt_scr[LANES:LANES + 2 * SUBLANES, pl.ds(r0, MOBA_BLOCK)] = ones_rows
            return 0

        lax.fori_loop(0, t // MOBA_BLOCK, prep, 0)

    jown = (i * tq) // MOBA_BLOCK
    n = 2 * tq
    q = q_ref[...]
    lane = lax.broadcasted_iota(jnp.int32, (tq, LANES), 1)
    qcat = jnp.concatenate([jnp.where(lane < HEAD_DIM, q, 0.0), jnp.where(lane >= HEAD_DIM, q, 0.0)], axis=0)
    km = jnp.concatenate([km_ref[...], jnp.zeros((LANES - nb, LANES), F32)], axis=0)
    gate = _dot_nt(qcat, km, precision=HIGHEST)
    ids = lax.broadcasted_iota(jnp.int32, (n, LANES), 1).astype(F32)
    gate = jnp.where(ids < jown.astype(F32), gate, NEG_INF)
    p0, p1, p2 = _topk_first(gate, ids, jown, 1)
    bias = jnp.where((ids == p0) | (ids == p1) | (ids == p2), 0.0, MASKED)
    qs = qcat * (LOG2_E / math.sqrt(HEAD_DIM))
    qa = jnp.concatenate([qs, bias], axis=1).astype(BF16)

    r0 = pl.multiple_of(i * tq, tq)
    causal = lax.broadcasted_iota(jnp.int32, (tq, tq), 0) <= lax.broadcasted_iota(jnp.int32, (tq, tq), 1)
    qsb = qs.astype(BF16)
    ms = []
    for hh in range(2):
        st = jnp.where(causal, _dot_nt(ka_scr[pl.ds(r0, tq), 0:LANES], qsb[hh * tq:(hh + 1) * tq]), NEG_INF)
        m = jnp.max(st, axis=0, keepdims=True)
        p = jnp.exp2(st - m)
        acc_scr[hh] = jnp.dot(vt_scr[:, pl.ds(r0, tq)], p.astype(BF16), preferred_element_type=F32)
        ms.append(m)

    def body(j, ms):
        c0 = pl.multiple_of(j * kv, kv)
        out = []
        sts = [_dot_nt(ka_scr[pl.ds(c0, kv), :], qa[hh * tq:(hh + 1) * tq]) for hh in range(2)]
        for hh in range(2):
            st = sts[hh]
            m_new = jnp.maximum(ms[hh], jnp.max(st, axis=0, keepdims=True))
            p = jnp.exp2(st - m_new)
            acc_scr[hh] = jnp.exp2(ms[hh] - m_new) * acc_scr[hh] + jnp.dot(
                vt_scr[:, pl.ds(c0, kv)], p.astype(BF16), preferred_element_type=F32)
            out.append(m_new)
        return tuple(out)

    lax.fori_loop(0, (jown + ATTN_KV_BLOCKS - 1) // ATTN_KV_BLOCKS, body, tuple(ms))
    a0, a1 = acc_scr[0], acc_scr[1]
    row = lax.broadcasted_iota(jnp.int32, (LANES, tq), 0)
    ot = jnp.where(row < HEAD_DIM, a0[0:LANES] / a0[LANES:LANES + 1], a1[0:LANES] / a1[LANES:LANES + 1])
    o_ref[...] = ot.T


def _attn_prompt_call(q, k, v, kmean):
    b, t, _ = q.shape
    tq = min(ATTN_Q_TILE, t)
    nb = kmean.shape[1]
    assert tq == MOBA_BLOCK and tq & (tq - 1) == 0 and nb <= LANES and t % (ATTN_KV_BLOCKS * MOBA_BLOCK) == 0
    npair = ATT_WIDTH // LANES
    return pl.pallas_call(
        functools.partial(_attn_prompt_kernel, tq=tq, t=t),
        scratch_shapes=[pltpu.VMEM((t, 2 * LANES), BF16), pltpu.VMEM((LANES + 2 * SUBLANES, t), BF16),
                        pltpu.VMEM((2, LANES + 2 * SUBLANES, tq), F32)],
        grid=(b, npair, t // tq),
        in_specs=[pl.BlockSpec((None, tq, LANES), lambda bi, hp, i: (bi, i, hp)),
                  pl.BlockSpec((None, t, LANES), lambda bi, hp, i: (bi, 0, hp)),
                  pl.BlockSpec((None, t, LANES), lambda bi, hp, i: (bi, 0, hp)),
                  pl.BlockSpec((None, nb, LANES), lambda bi, hp, i: (bi, 0, hp))],
        out_specs=pl.BlockSpec((None, tq, LANES), lambda bi, hp, i: (bi, i, hp)),
        out_shape=jax.ShapeDtypeStruct((b, t, ATT_WIDTH), F32),
        compiler_params=_cparams(3), name="moba_prompt",
    )(q, k, v, kmean)


def _gated_norm(y, xs, z, dsk, nw):
    yg = (y + dsk * xs) * _silu(z)
    return _rms(yg, nw)


def _ssd_prompt_kernel(xs_ref, bc_ref, dt_ref, z_ref, alog_ref, dsk_ref, nw_ref, y_ref, st_ref, h_scr, *, L):
    c = pl.program_id(1)

    @pl.when(c == 0)
    def _():
        h_scr[...] = jnp.zeros(h_scr.shape, F32)

    dt = dt_ref[...]
    a = -jnp.exp(alog_ref[...])
    lane = lax.broadcasted_iota(jnp.int32, dt.shape, 1)
    acum = dt * a
    s = 1
    while s < L:
        acum = acum + jnp.where(lane >= s, pltpu.roll(acum, s, 1), 0.0)
        s *= 2
    rows = jnp.concatenate([acum, dt, jnp.zeros((L - 2 * SSD_HEADS, L), F32)], axis=0)
    cols = rows.T

    xs = xs_ref[...]
    bc = bc_ref[...]
    ri = lax.broadcasted_iota(jnp.int32, (L, L), 0)
    ci = lax.broadcasted_iota(jnp.int32, (L, L), 1)
    causal = ri >= ci
    low = lax.broadcasted_iota(jnp.int32, (L, LANES), 1) < SSD_HEADDIM
    low_rows = lax.broadcasted_iota(jnp.int32, (LANES, SSD_STATE), 0) < SSD_HEADDIM
    gs = SSD_GROUPS * SSD_STATE
    heads_per_group = SSD_HEADS // SSD_GROUPS
    ys = []
    for g in range(SSD_GROUPS):
        bg = bc[:, g * SSD_STATE:(g + 1) * SSD_STATE].astype(BF16)
        cg = bc[:, gs + g * SSD_STATE:gs + (g + 1) * SSD_STATE].astype(BF16)
        cbt = _dot_nt(cg, bg)
        for pr in range(heads_per_group // 2):
            hp = g * (heads_per_group // 2) + pr
            xp = xs[:, hp * LANES:(hp + 1) * LANES]
            xpb = xp.astype(BF16)
            hst = h_scr[hp * LANES:(hp + 1) * LANES, :]
            inter = _dot_nt(cg, hst.astype(BF16))
            intra, ecols, wcols, cds = [], [], [], []
            for hh in range(2):
                hd = 2 * hp + hh
                col = cols[:, hd:hd + 1]
                row = acum[hd:hd + 1, :]
                decay = jnp.exp(jnp.where(causal, col - row, NEG_INF))
                m = cbt * decay * dt[hd:hd + 1, :]
                intra.append(jnp.dot(m.astype(BF16), xpb, preferred_element_type=F32))
                last = acum[hd:hd + 1, L - 1:L]
                ecols.append(jnp.exp(col))
                wcols.append(jnp.exp(last - col) * cols[:, SSD_HEADS + hd:SSD_HEADS + hd + 1])
                cds.append(jnp.exp(last))
            ys.append(jnp.where(low, intra[0], intra[1]) + inter * jnp.where(low, ecols[0], ecols[1]))
            xw = (xp * jnp.where(low, wcols[0], wcols[1])).astype(BF16)
            upd = _dot_tn(xw, bg)
            h_scr[hp * LANES:(hp + 1) * LANES, :] = hst * jnp.where(low_rows, cds[0], cds[1]) + upd
    y = jnp.concatenate(ys, axis=1)
    y_ref[...] = _gated_norm(y, xs, z_ref[...], dsk_ref[...], nw_ref[...])
    st_ref[...] = h_scr[...]


def _ssd_prompt_call(xs, bc, dt_t, z, alog_col, dsk_row, nw_row):
    b, t, _ = xs.shape
    L = SSD_CHUNK
    row_spec = lambda c: pl.BlockSpec((None, L, c), lambda bi, ci: (bi, ci, 0))
    full = lambda a: pl.BlockSpec(a.shape, lambda bi, ci: (0,) * a.ndim)
    return pl.pallas_call(
        functools.partial(_ssd_prompt_kernel, L=L),
        grid=(b, t // L),
        in_specs=[row_spec(SSD_INNER), row_spec(SSD_CONV_DIM - SSD_INNER),
                  pl.BlockSpec((None, SSD_HEADS, L), lambda bi, ci: (bi, 0, ci)),
                  row_spec(SSD_INNER), full(alog_col), full(dsk_row), full(nw_row)],
        out_specs=[row_spec(SSD_INNER),
                   pl.BlockSpec((None, SSD_INNER, SSD_STATE), lambda bi, ci: (bi, 0, 0))],
        out_shape=[jax.ShapeDtypeStruct((b, t, SSD_INNER), F32),
                   jax.ShapeDtypeStruct((b, SSD_INNER, SSD_STATE), F32)],
        scratch_shapes=[pltpu.VMEM((SSD_INNER, SSD_STATE), F32)],
        compiler_params=_cparams(2), name="ssd_prompt",
    )(xs, bc, dt_t, z, alog_col, dsk_row, nw_row)


def _mixout_kernel(att_ref, y_ref, x_ref, gt_ref, gp_ref, wa_ref, wy_ref, o_ref):
    out = (jnp.dot(att_ref[...].astype(BF16), wa_ref[...], preferred_element_type=F32)
           + jnp.dot(y_ref[...].astype(BF16), wy_ref[...], preferred_element_type=F32))
    o_ref[...] = x_ref[...] + gt_ref[...] * _rms(out, gp_ref[...])


def _mixout_call(att, y, x, gate, gp, wa, wy, *, per_seq_mod):
    b, t, d = x.shape
    tm = min(ROW_TILE, t)
    row_spec = lambda c: pl.BlockSpec((None, tm, c), lambda bi, ti: (bi, ti, 0))
    mod_spec = (pl.BlockSpec((None, 1, d), lambda bi, ti: (bi, 0, 0)) if per_seq_mod
                else pl.BlockSpec((None, tm, d), lambda bi, ti: (bi, ti, 0)))
    full = lambda a: pl.BlockSpec(a.shape, lambda bi, ti: (0,) * a.ndim)
    return pl.pallas_call(
        _mixout_kernel,
        grid=(b, t // tm),
        in_specs=[row_spec(ATT_WIDTH), row_spec(SSD_INNER), row_spec(d), mod_spec, full(gp), full(wa), full(wy)],
        out_specs=row_spec(d),
        out_shape=jax.ShapeDtypeStruct((b, t, d), F32),
        compiler_params=_cparams(2), name="mix_out",
    )(att, y, x, gate, gp, wa, wy)


def _ffn_kernel(*refs, tm, decode):
    if decode:
        (x_ref, sh_ref, sc_ref, gt_ref, g_ref, gp_ref, up_ref, cw_ref, cb_ref, dn_ref, p0_ref, p1_ref,
         o_ref, u_ref, ubuf0_ref, ubuf1_ref, acc_ref, hb_ref) = refs
    else:
        (x_ref, sh_ref, sc_ref, gt_ref, g_ref, gp_ref, up_ref, cw_ref, cb_ref, dn_ref,
         o_ref, cst_ref, carry_ref, ubuf0_ref, ubuf1_ref, acc_ref, hb_ref) = refs
        t = pl.program_id(1)

        @pl.when(t == 0)
        def _():
            carry_ref[...] = jnp.zeros(carry_ref.shape, F32)

    x = x_ref[...]
    hb_ref[...] = _norm_mod(x, g_ref[...], sh_ref[...], sc_ref[...]).astype(BF16)
    acc_ref[...] = jnp.zeros(acc_ref.shape, F32)
    fc = FFN_CHUNK
    n_chunks = D_FF // fc
    ubufs = (ubuf0_ref, ubuf1_ref)

    def col_offset(j, half):
        return pl.multiple_of(half * D_FF + j * fc, LANES)

    def up_project(j, ubuf):
        hb = hb_ref[...]
        for half in range(2):
            off, lo = col_offset(j, half), half * fc
            raw = jnp.dot(hb, up_ref[:, pl.ds(off, fc)], preferred_element_type=F32)
            ubuf[SUBLANES:SUBLANES + tm, lo:lo + fc] = raw
            if decode:
                u_ref[:, pl.ds(off, fc)] = raw
            else:
                ubuf[0:SUBLANES, lo:lo + fc] = carry_ref[:, pl.ds(off, fc)]
                carry_ref[:, pl.ds(off, fc)] = raw[tm - SUBLANES:tm, :]

    def conv_gate_down(j, ubuf):
        halves = []
        for half in range(2):
            off, lo = col_offset(j, half), half * fc
            cw = cw_ref[:, pl.ds(off, fc)]
            conv = cb_ref[:, pl.ds(off, fc)] + cw[FFN_CONV - 1:FFN_CONV, :] * ubuf[SUBLANES:SUBLANES + tm, lo:lo + fc]
            if decode:
                conv = conv + cw[1:2, :] * p1_ref[:, pl.ds(off, fc)] + cw[0:1, :] * p0_ref[:, pl.ds(off, fc)]
            else:
                for i in range(1, FFN_CONV):
                    conv = conv + (cw[FFN_CONV - 1 - i:FFN_CONV - i, :]
                                   * ubuf[SUBLANES - i:SUBLANES - i + tm, lo:lo + fc])
            halves.append(conv)
        act = _silu(halves[0]) * halves[1]
        r0 = pl.multiple_of(j * fc, fc)
        acc_ref[...] += jnp.dot(act.astype(BF16), dn_ref[pl.ds(r0, fc), :], preferred_element_type=F32)

    up_project(0, ubufs[0])

    def pair(i, _):
        up_project(2 * i + 1, ubufs[1])
        conv_gate_down(2 * i, ubufs[0])
        up_project(2 * i + 2, ubufs[0])
        conv_gate_down(2 * i + 1, ubufs[1])
        return 0

    assert n_chunks % 2 == 1
    lax.fori_loop(0, n_chunks // 2, pair, 0)
    conv_gate_down(n_chunks - 1, ubufs[0])
    o_ref[...] = x + gt_ref[...] * _rms(acc_ref[...], gp_ref[...])
    if not decode:
        cst_ref[...] = carry_ref[...]


def _ffn_call(x, shift, scale, gate, g, gp, up_w, conv_w, conv_b, dn_w, prev=None):
    b, t, d = x.shape
    decode = prev is not None
    tm = min(ROW_TILE, t)
    row_spec = lambda c: pl.BlockSpec((None, tm, c), lambda bi, ti: (bi, ti, 0))
    mod_spec = (pl.BlockSpec((None, tm, d), lambda bi, ti: (bi, ti, 0)) if decode
                else pl.BlockSpec((None, 1, d), lambda bi, ti: (bi, 0, 0)))
    full = lambda a: pl.BlockSpec(a.shape, lambda bi, ti: (0,) * a.ndim)
    in_specs = [row_spec(d), mod_spec, mod_spec, mod_spec, full(g), full(gp), full(up_w), full(conv_w),
                full(conv_b), full(dn_w)]
    args = [x, shift, scale, gate, g, gp, up_w, conv_w, conv_b, dn_w]
    scratch = [pltpu.VMEM((tm + SUBLANES, 2 * FFN_CHUNK), F32)] * 2 + [pltpu.VMEM((tm, d), F32), pltpu.VMEM((tm, d), BF16)]
    if decode:
        assert b == 1 and t == tm
        in_specs += [pl.BlockSpec((tm, 2 * D_FF), lambda bi, ti: (ti, 0))] * 2
        args += list(prev)
        out_specs = [row_spec(d), pl.BlockSpec((tm, 2 * D_FF), lambda bi, ti: (ti, 0))]
        out_shape = [jax.ShapeDtypeStruct((b, t, d), F32), jax.ShapeDtypeStruct((t, 2 * D_FF), F32)]
    else:
        out_specs = [row_spec(d), pl.BlockSpec((None, SUBLANES, 2 * D_FF), lambda bi, ti: (bi, 0, 0))]
        out_shape = [jax.ShapeDtypeStruct((b, t, d), F32), jax.ShapeDtypeStruct((b, SUBLANES, 2 * D_FF), F32)]
        scratch = [pltpu.VMEM((SUBLANES, 2 * D_FF), F32)] + scratch
    return pl.pallas_call(
        functools.partial(_ffn_kernel, tm=tm, decode=decode),
        grid=(b, t // tm), in_specs=in_specs, out_specs=out_specs, out_shape=out_shape,
        scratch_shapes=scratch, compiler_params=_cparams(2),
        name="conv_ffn_sample" if decode else "conv_ffn_prompt",
    )(*args)


def _pool_kernel(*refs, tm, decode, pos0):
    if decode:
        (x_ref, sh_ref, sc_ref, gt_ref, g_ref, gp_ref, pw_ref, pb_ref, ps_ref, prev_ref, o_ref, h_ref) = refs
    else:
        (x_ref, sh_ref, sc_ref, gt_ref, g_ref, gp_ref, pw_ref, pb_ref, ps_ref, o_ref, last_ref, buf_ref) = refs
        t = pl.program_id(1)

        @pl.when(t == 0)
        def _():
            buf_ref[0:POOL_MAX, :] = jnp.zeros((POOL_MAX, D_MODEL), F32)

    x = x_ref[...]
    h = _norm_mod(x, g_ref[...], sh_ref[...], sc_ref[...])
    if decode:
        h_ref[...] = h
    else:
        buf_ref[POOL_MAX:POOL_MAX + tm, :] = h
        row = lax.broadcasted_iota(jnp.int32, (tm, 1), 0)
        pos = pos0 + t * tm + row
    ys = []
    for gi, win in enumerate(POOL_WINDOWS):
        lo = gi * POOL_GROUP_DIM
        hg = h[:, lo:lo + POOL_GROUP_DIM]
        wsum = hg
        for i in range(1, win):
            if decode:
                wsum = wsum + prev_ref[POOL_MAX - 1 - i, :, lo:lo + POOL_GROUP_DIM]
            else:
                wsum = wsum + buf_ref[POOL_MAX - i:POOL_MAX - i + tm, lo:lo + POOL_GROUP_DIM]
        if decode:
            cnt = float(min(pos0 + 1, win))
        else:
            cnt = jnp.minimum(pos + 1, win).astype(F32)
        pooled = wsum / cnt - hg
        ys.append(jnp.dot(pooled.astype(BF16), pw_ref[gi], preferred_element_type=F32))
    y = (jnp.concatenate(ys, axis=1) + pb_ref[...]) * ps_ref[...]
    o_ref[...] = x + gt_ref[...] * _rms(y, gp_ref[...])
    if not decode:
        last_ref[...] = h[tm - POOL_MAX:tm, :]
        buf_ref[0:POOL_MAX, :] = h[tm - POOL_MAX:tm, :]


def _pool_call(x, shift, scale, gate, g, gp, pw, pb, ps, prev_t=None, pos0=0):
    b, t, d = x.shape
    decode = prev_t is not None
    tm = min(ROW_TILE, t)
    row_spec = lambda c: pl.BlockSpec((None, tm, c), lambda bi, ti: (bi, ti, 0))
    mod_spec = (pl.BlockSpec((None, tm, d), lambda bi, ti: (bi, ti, 0)) if decode
                else pl.BlockSpec((None, 1, d), lambda bi, ti: (bi, 0, 0)))
    full = lambda a: pl.BlockSpec(a.shape, lambda bi, ti: (0,) * a.ndim)
    in_specs = [row_spec(d), mod_spec, mod_spec, mod_spec, full(g), full(gp), full(pw), full(pb), full(ps)]
    args = [x, shift, scale, gate, g, gp, pw, pb, ps]
    if decode:
        assert b == 1 and t == tm
        in_specs.append(full(prev_t))
        args.append(prev_t)
        out_specs = [row_spec(d), row_spec(d)]
        out_shape = [jax.ShapeDtypeStruct((b, t, d), F32)] * 2
        scratch = []
    else:
        out_specs = [row_spec(d), pl.BlockSpec((None, POOL_MAX, d), lambda bi, ti: (bi, 0, 0))]
        out_shape = [jax.ShapeDtypeStruct((b, t, d), F32), jax.ShapeDtypeStruct((b, POOL_MAX, d), F32)]
        scratch = [pltpu.VMEM((tm + POOL_MAX, d), F32)]
    return pl.pallas_call(
        functools.partial(_pool_kernel, tm=tm, decode=decode, pos0=pos0),
        grid=(b, t // tm), in_specs=in_specs, out_specs=out_specs, out_shape=out_shape,
        scratch_shapes=scratch, compiler_params=_cparams(2),
        name="pool_sample" if decode else "pool_prompt",
    )(*args)


def _mixin_sample_kernel(x_ref, sh_ref, sc_ref, g_ref, w_ref, wdtx_ref, dtbx_ref, cw_ref, cb_ref, prev_ref,
                         q_ref, k_ref, v_ref, z_ref, xs_ref, bc_ref, dtx_ref, raw_ref, *, pos0):
    h = _norm_mod(x_ref[...], g_ref[...], sh_ref[...], sc_ref[...])
    proj = jnp.dot(h.astype(BF16), w_ref[...], preferred_element_type=F32)
    n = h.shape[0]
    cos_t, sin_t, low = _rope_tables(jnp.float32(pos0), (n, LANES))
    for gi in range(ATT_WIDTH // LANES):
        lo = gi * LANES
        q_ref[:, lo:lo + LANES] = _rope_group(proj[:, lo:lo + LANES], cos_t, sin_t, low)
        k_ref[:, lo:lo + LANES] = _rope_group(proj[:, ATT_WIDTH + lo:ATT_WIDTH + lo + LANES], cos_t, sin_t, low)
    v_ref[...] = proj[:, 2 * ATT_WIDTH:3 * ATT_WIDTH]
    z_ref[...] = proj[:, 3 * ATT_WIDTH:3 * ATT_WIDTH + SSD_INNER]
    raw = proj[:, 3 * ATT_WIDTH + SSD_INNER:MAIN_COLS]
    raw_ref[...] = raw
    conv = cb_ref[...] + cw_ref[SSD_CONV - 1:SSD_CONV, :] * raw
    for i in range(SSD_CONV - 1):
        conv = conv + cw_ref[i:i + 1, :] * prev_ref[i]
    act = _silu(conv)
    xs_ref[...] = act[:, :SSD_INNER]
    bc_ref[...] = act[:, SSD_INNER:]
    dtx_ref[...] = _softplus(jnp.dot(h, wdtx_ref[...], preferred_element_type=F32, precision=HIGHEST) + dtbx_ref[...])


def _mixin_sample_call(x, shift, scale, g, w_main, wdt_x, dtb_x, conv_w, conv_b, prev_t, pos0):
    n, d = x.shape
    full = lambda a: pl.BlockSpec(a.shape, lambda i: (0,) * a.ndim)
    args = [x, shift, scale, g, w_main, wdt_x, dtb_x, conv_w, conv_b, prev_t]
    widths = [ATT_WIDTH] * 3 + [SSD_INNER] * 2 + [SSD_CONV_DIM - SSD_INNER, SSD_INNER, SSD_CONV_DIM]
    return pl.pallas_call(
        functools.partial(_mixin_sample_kernel, pos0=pos0),
        grid=(1,),
        in_specs=[full(a) for a in args],
        out_specs=[pl.BlockSpec((n, c), lambda i: (0, 0)) for c in widths],
        out_shape=[jax.ShapeDtypeStruct((n, c), F32) for c in widths],
        compiler_params=_cparams(1), name="mixin_sample",
    )(*args)


_PAGES_PER_BLOCK = MOBA_BLOCK // PAGE_SIZE
_SEL_PAGES = MOBA_TOPK * _PAGES_PER_BLOCK
SAMPLE_HEADS_PER_STEP = 4


def _token_minor(cache):
    return jnp.transpose(cache, (0, 1, 3, 4, 2))


def _past_kmean_kernel(pt_ref, *refs):
    pages, o_ref = refs[:PAGES_PER_STEP], refs[PAGES_PER_STEP]
    step = pl.program_id(1)
    blocks_per_step = PAGES_PER_STEP // _PAGES_PER_BLOCK

    @pl.when(step == 0)
    def _():
        o_ref[...] = jnp.zeros(o_ref.shape, F32)

    acc = o_ref[...]
    lane = lax.broadcasted_iota(jnp.int32, acc.shape, 2)
    for blk in range(blocks_per_step):
        tot = pages[blk * _PAGES_PER_BLOCK][...]
        for r in range(1, _PAGES_PER_BLOCK):
            tot = tot + pages[blk * _PAGES_PER_BLOCK + r][...]
        mean = jnp.sum(tot, axis=-1, keepdims=True) * (1.0 / MOBA_BLOCK)
        acc = jnp.where(lane == step * blocks_per_step + blk, mean, acc)
    o_ref[...] = acc


def _past_kmean_call(pt_flat, cache_kt, n_seq, n_pages):
    steps = n_pages // PAGES_PER_STEP
    nb = n_pages // _PAGES_PER_BLOCK

    def page_map(b, s, pt, *, r):
        return (0, pt[b * n_pages + s * PAGES_PER_STEP + r], 0, 0, 0)

    grid_spec = pltpu.PrefetchScalarGridSpec(
        num_scalar_prefetch=1, grid=(n_seq, steps),
        in_specs=[pl.BlockSpec((None, None, ATT_HEADS, HEAD_DIM, PAGE_SIZE), functools.partial(page_map, r=r))
                  for r in range(PAGES_PER_STEP)],
        out_specs=pl.BlockSpec((None, ATT_HEADS, HEAD_DIM, nb), lambda b, s, pt: (b, 0, 0, 0)))
    return pl.pallas_call(
        _past_kmean_kernel, grid_spec=grid_spec,
        out_shape=jax.ShapeDtypeStruct((n_seq, ATT_HEADS, HEAD_DIM, nb), F32),
        compiler_params=_cparams(2), name="past_kmean",
    )(pt_flat, *([cache_kt] * PAGES_PER_STEP))


def _sample_select_kernel(q_ref, km_ref, sel_ref, *, jown):
    gate = jnp.sum(km_ref[...] * q_ref[...], axis=1, keepdims=True)
    ids = lax.broadcasted_iota(jnp.int32, gate.shape, 2).astype(F32)
    gate = jnp.where(ids < float(jown), gate, NEG_INF)
    picks = _topk_first(gate, ids, jown, 2)
    for r in range(MOBA_TOPK):
        sel_ref[r] = picks[r].astype(jnp.int32)


def _sample_select_call(q4, kmean, jown):
    n, _, _, nb = kmean.shape
    return pl.pallas_call(
        functools.partial(_sample_select_kernel, jown=jown),
        grid=(n,),
        in_specs=[pl.BlockSpec((None, ATT_HEADS, HEAD_DIM, 1), lambda b: (b, 0, 0, 0)),
                  pl.BlockSpec((None, ATT_HEADS, HEAD_DIM, nb), lambda b: (b, 0, 0, 0))],
        out_specs=pl.BlockSpec((None, MOBA_TOPK, ATT_HEADS, 1, 1), lambda b: (b, 0, 0, 0, 0)),
        out_shape=jax.ShapeDtypeStruct((n, MOBA_TOPK, ATT_HEADS, 1, 1), jnp.int32),
        compiler_params=_cparams(1), name="sample_select",
    )(q4, kmean)


def _sample_attn_kernel(pt_ref, sel_ref, q_ref, kn_ref, vn_ref, *refs):
    n_pg = SAMPLE_HEADS_PER_STEP * _SEL_PAGES
    kp, vp, o_ref = refs[:n_pg], refs[n_pg:2 * n_pg], refs[2 * n_pg]
    b, grp = pl.program_id(0), pl.program_id(1)
    scale = 1.0 / math.sqrt(HEAD_DIM)
    for hh in range(SAMPLE_HEADS_PER_STEP):
        h = grp * SAMPLE_HEADS_PER_STEP + hh
        q = q_ref[hh] * scale
        s_self = jnp.sum(q * kn_ref[hh], axis=0, keepdims=True)
        scores = []
        for r in range(MOBA_TOPK):
            valid = sel_ref[(b * MOBA_TOPK + r) * ATT_HEADS + h] >= 0
            for pg in range(_PAGES_PER_BLOCK):
                s = jnp.sum(kp[hh * _SEL_PAGES + r * _PAGES_PER_BLOCK + pg][...] * q, axis=0, keepdims=True)
                scores.append(jnp.where(valid, s, NEG_INF))
        m = s_self
        for s in scores:
            m = jnp.maximum(m, jnp.max(s, axis=1, keepdims=True))
        p_self = jnp.exp(s_self - m)
        l = p_self
        acc = p_self * vn_ref[hh]
        for idx, s in enumerate(scores):
            p = jnp.exp(s - m)
            l = l + jnp.sum(p, axis=1, keepdims=True)
            acc = acc + jnp.sum(vp[hh * _SEL_PAGES + idx][...] * p, axis=1, keepdims=True)
        o_ref[hh] = acc / l


def _sample_attn_call(pt_flat, sel_flat, q4, k4, v4, cache_kt, cache_vt, n_pages):
    n = q4.shape[0]
    hps = SAMPLE_HEADS_PER_STEP
    head_spec = pl.BlockSpec((None, hps, HEAD_DIM, 1), lambda b, g, pt, sel: (b, g, 0, 0))

    def page_map(b, g, pt, sel, *, hh, r, pg):
        h = g * hps + hh
        blk = jnp.maximum(sel[(b * MOBA_TOPK + r) * ATT_HEADS + h], 0)
        return (0, pt[b * n_pages + blk * _PAGES_PER_BLOCK + pg], h, 0, 0)

    page_specs = [pl.BlockSpec((None, None, None, HEAD_DIM, PAGE_SIZE), functools.partial(page_map, hh=hh, r=r, pg=pg))
                  for hh in range(hps) for r in range(MOBA_TOPK) for pg in range(_PAGES_PER_BLOCK)]
    grid_spec = pltpu.PrefetchScalarGridSpec(
        num_scalar_prefetch=2, grid=(n, ATT_HEADS // hps),
        in_specs=[head_spec, head_spec, head_spec] + page_specs + page_specs,
        out_specs=head_spec)
    n_pg = hps * _SEL_PAGES
    return pl.pallas_call(
        _sample_attn_kernel, grid_spec=grid_spec,
        out_shape=jax.ShapeDtypeStruct((n, ATT_HEADS, HEAD_DIM, 1), F32),
        compiler_params=_cparams(2), name="sample_attn",
    )(pt_flat, sel_flat, q4, k4, v4, *([cache_kt] * n_pg), *([cache_vt] * n_pg))


def _ssd_sample_kernel(xs_ref, bc_ref, dtx_ref, z_ref, st_ref, alogx_ref, dsk_ref, nw_ref, y_ref, sto_ref):
    xs, dtx = xs_ref[...], dtx_ref[...]
    decay = jnp.exp(dtx * (-jnp.exp(alogx_ref[...])))
    xdt = xs * dtx
    bc = bc_ref[...]
    rows_per_group = SSD_INNER // SSD_GROUPS
    gs = SSD_GROUPS * SSD_STATE
    eye = (lax.broadcasted_iota(jnp.int32, (rows_per_group, rows_per_group), 0)
           == lax.broadcasted_iota(jnp.int32, (rows_per_group, rows_per_group), 1))

    def to_col(v):
        return jnp.sum(jnp.where(eye, jnp.broadcast_to(v, eye.shape), 0.0), axis=1, keepdims=True)

    def to_row(v):
        return jnp.sum(jnp.where(eye, jnp.broadcast_to(v, eye.shape), 0.0), axis=0, keepdims=True)

    ys = []
    for g in range(SSD_GROUPS):
        lo = g * rows_per_group
        b_row = bc[:, g * SSD_STATE:(g + 1) * SSD_STATE]
        c_row = bc[:, gs + g * SSD_STATE:gs + (g + 1) * SSD_STATE]
        hg = st_ref[lo:lo + rows_per_group, :]
        hnew = to_col(decay[:, lo:lo + rows_per_group]) * hg + to_col(xdt[:, lo:lo + rows_per_group]) * b_row
        sto_ref[lo:lo + rows_per_group, :] = hnew
        ys.append(to_row(jnp.sum(hnew * c_row, axis=1, keepdims=True)))
    y = jnp.concatenate(ys, axis=1)
    y_ref[...] = _gated_norm(y, xs, z_ref[...], dsk_ref[...], nw_ref[...])


def _ssd_sample_call(xs, bc, dtx, z, state, alog_x, dsk_row, nw_row):
    n = xs.shape[0]
    r3 = lambda a: a.reshape(n, 1, a.shape[-1])
    row_spec = lambda c: pl.BlockSpec((None, 1, c), lambda b: (b, 0, 0))
    st_spec = pl.BlockSpec((None, SSD_INNER, SSD_STATE), lambda b: (b, 0, 0))
    full = lambda a: pl.BlockSpec(a.shape, lambda b: (0,) * a.ndim)
    y, st = pl.pallas_call(
        _ssd_sample_kernel,
        grid=(n,),
        in_specs=[row_spec(SSD_INNER), row_spec(SSD_CONV_DIM - SSD_INNER), row_spec(SSD_INNER), row_spec(SSD_INNER),
                  st_spec, full(alog_x), full(dsk_row), full(nw_row)],
        out_specs=[row_spec(SSD_INNER), st_spec],
        out_shape=[jax.ShapeDtypeStruct((n, 1, SSD_INNER), F32), jax.ShapeDtypeStruct((n, SSD_INNER, SSD_STATE), F32)],
        compiler_params=_cparams(1), name="ssd_sample",
    )(r3(xs), r3(bc), r3(dtx), r3(z), state, alog_x, dsk_row, nw_row)
    return y.reshape(n, SSD_INNER), st


def kernel(x_prompt, x_sample, cache_k, cache_v, state_ssm, state_ssd_conv, state_pool, state_ffn_conv, page_table,
           c_prompt, c_sample, ada_w, ada_b, norm_pre, norm_post, mix_in_w, mix_out_w, ssd_conv_w, ssd_conv_b,
           ssd_dt_bias, ssd_a_log, ssd_d, ssd_norm_w, pool_w, pool_b, pool_scale, ffn_up_w, ffn_conv_w,
           ffn_conv_b, ffn_down_w):
    bp, t, d = x_prompt.shape
    ns = x_sample.shape[0]
    n_pages = page_table.shape[1]
    past_len = n_pages * cache_k.shape[2]
    assert x_sample.shape[1] == 1 and cache_k.shape[2] == PAGE_SIZE and past_len % MOBA_BLOCK == 0
    assert t % ROW_TILE == 0 or t < ROW_TILE

    w_main = mix_in_w[0, :, :MAIN_COLS].astype(BF16)
    w_dt = mix_in_w[0, :, MAIN_COLS:]
    wdt_t = w_dt.T
    wdt_x = jnp.repeat(w_dt, SSD_HEADDIM, axis=1)
    dtb_col = ssd_dt_bias[0].reshape(SSD_HEADS, 1)
    dtb_x = jnp.repeat(ssd_dt_bias[0], SSD_HEADDIM).reshape(1, SSD_INNER)
    alog_col = ssd_a_log[0].reshape(SSD_HEADS, 1)
    alog_x = jnp.repeat(ssd_a_log[0], SSD_HEADDIM).reshape(1, SSD_INNER)
    dsk_row = jnp.repeat(ssd_d[0], SSD_HEADDIM).reshape(1, SSD_INNER)
    nw_row = ssd_norm_w[0].reshape(1, SSD_INNER)
    w_att = mix_out_w[0, :ATT_WIDTH].astype(BF16)
    w_ssd = mix_out_w[0, ATT_WIDTH:].astype(BF16)
    conv_w0 = ssd_conv_w[0]
    conv_b0 = ssd_conv_b[0].reshape(1, SSD_CONV_DIM)
    up_w = ffn_up_w.astype(BF16)
    dn_w = ffn_down_w.astype(BF16)
    pw = pool_w[0].astype(BF16)
    pb = pool_b[0].reshape(1, d)
    ps = pool_scale[0].reshape(1, d)

    mods = _ada_call(jnp.concatenate([c_prompt, c_sample], axis=0), ada_w, ada_b)

    def mod(layer, slot, sample):
        m = mods[layer * 2 + slot]
        m = m[bp:] if sample else m[:bp]
        parts = (m[:, :d], m[:, d:2 * d], m[:, 2 * d:])
        return tuple(p[None] for p in parts) if sample else tuple(p[:, None, :] for p in parts)

    npre = lambda layer, slot: norm_pre[layer, slot].reshape(1, d)
    npost = lambda layer, slot: norm_post[layer, slot].reshape(1, d)

    def ffn(x, layer, sample, prev=None):
        sh, sc, gt = mod(layer, 1, sample)
        return _ffn_call(x, sh, sc, gt, npre(layer, 1), npost(layer, 1), up_w[layer], ffn_conv_w[layer],
                         ffn_conv_b[layer].reshape(1, 2 * D_FF), dn_w[layer], prev=prev)

    sh, sc, gt = mod(0, 0, False)
    q, k, v, z, xs, bc, dt_t, kmean, cst = _mixin_prompt_call(
        x_prompt, sh, sc, npre(0, 0), w_main, wdt_t, dtb_col, conv_w0, conv_b0)
    att = _attn_prompt_call(q, k, v, kmean)
    y, ssm_p = _ssd_prompt_call(xs, bc, dt_t, z, alog_col, dsk_row, nw_row)
    xp = _mixout_call(att, y, x_prompt, gt, npost(0, 0), w_att, w_ssd, per_seq_mod=True)
    xp, fst0 = ffn(xp, 0, False)
    sh, sc, gt = mod(1, 0, False)
    xp, plast = _pool_call(xp, sh, sc, gt, npre(1, 0), npost(1, 0), pw, pb, ps)
    xp, fst1 = ffn(xp, 1, False)

    k_p = k.reshape(1, bp, t, ATT_HEADS, HEAD_DIM)
    v_p = v.reshape(1, bp, t, ATT_HEADS, HEAD_DIM)
    ssm_p = ssm_p.reshape(1, bp, SSD_HEADS, SSD_HEADDIM, SSD_STATE)
    sconv_p = cst[None, :, SUBLANES - (SSD_CONV - 1):, :]
    pool_p = plast[None, :, 1:, :]
    fconv_p = jnp.stack([fst0, fst1])[:, :, SUBLANES - (FFN_CONV - 1):, :]

    xs0 = x_sample.reshape(ns, d)
    sh, sc, gt = mod(0, 0, True)
    conv_prev_t = jnp.transpose(state_ssd_conv[0], (1, 0, 2))
    qs, ks, vs, zs, xss, bcs, dtx, raw = _mixin_sample_call(
        xs0, sh[0], sc[0], npre(0, 0), w_main, wdt_x, dtb_x, conv_w0, conv_b0, conv_prev_t, past_len)
    pt_flat = page_table.reshape(-1)
    cache_kt, cache_vt = _token_minor(cache_k), _token_minor(cache_v)
    kmean_s = _past_kmean_call(pt_flat, cache_kt, ns, n_pages)
    jown = past_len // MOBA_BLOCK
    as_cols = lambda a: a.reshape(ns, ATT_HEADS, HEAD_DIM, 1)
    sel = _sample_select_call(as_cols(qs), kmean_s, jown)
    att_s = _sample_attn_call(pt_flat, sel.reshape(-1), as_cols(qs), as_cols(ks), as_cols(vs),
                              cache_kt, cache_vt, n_pages).reshape(ns, ATT_WIDTH)
    y_s, ssm_s = _ssd_sample_call(xss, bcs, dtx, zs, state_ssm[0].reshape(ns, SSD_INNER, SSD_STATE),
                                  alog_x, dsk_row, nw_row)
    xq = _mixout_call(att_s[None], y_s[None], xs0[None], gt, npost(0, 0), w_att, w_ssd, per_seq_mod=False)
    xq, u0 = ffn(xq, 0, True, prev=(state_ffn_conv[0, :, 0], state_ffn_conv[0, :, 1]))
    sh, sc, gt = mod(1, 0, True)
    pool_prev_t = jnp.transpose(state_pool[0], (1, 0, 2))
    xq, h_pool = _pool_call(xq, sh, sc, gt, npre(1, 0), npost(1, 0), pw, pb, ps, prev_t=pool_prev_t, pos0=past_len)
    xq, u1 = ffn(xq, 1, True, prev=(state_ffn_conv[1, :, 0], state_ffn_conv[1, :, 1]))

    y_sample = xq.reshape(ns, 1, d)
    k_s = ks.reshape(1, ns, 1, ATT_HEADS, HEAD_DIM)
    v_s = vs.reshape(1, ns, 1, ATT_HEADS, HEAD_DIM)
    ssm_s = ssm_s.reshape(1, ns, SSD_HEADS, SSD_HEADDIM, SSD_STATE)
    sconv_s = jnp.concatenate([state_ssd_conv[0, :, 1:], raw[:, None, :]], axis=1)[None]
    pool_s = jnp.concatenate([state_pool[0, :, 1:], h_pool[0][:, None, :]], axis=1)[None]
    fconv_s = jnp.stack([jnp.stack([state_ffn_conv[0, :, 1], u0], axis=1),
                         jnp.stack([state_ffn_conv[1, :, 1], u1], axis=1)])
    return (xp, y_sample, k_p, v_p, ssm_p, sconv_p, pool_p, fconv_p, k_s, v_s, ssm_s, sconv_s, pool_s, fconv_s)
```

```python
import functools
import math

import jax
import jax.numpy as jnp
from jax import lax
from jax.experimental import pallas as pl
from jax.experimental.pallas import tpu as pltpu

F32 = jnp.float32
BF16 = jnp.bfloat16
HIGHEST = lax.Precision.HIGHEST

D_MODEL = 1024
ATT_HEADS = 8
HEAD_DIM = 64
ATT_WIDTH = ATT_HEADS * HEAD_DIM
ROT_DIM = HEAD_DIM // 4
ROPE_THETA = 500000.0
MOBA_BLOCK = 256
MOBA_TOPK = 3
SSD_HEADS = 8
SSD_HEADDIM = 64
SSD_INNER = SSD_HEADS * SSD_HEADDIM
SSD_GROUPS = 2
SSD_STATE = 128
SSD_CONV = 4
SSD_CONV_DIM = SSD_INNER + 2 * SSD_GROUPS * SSD_STATE
POOL_WINDOWS = (2, 4, 8, 16)
POOL_GROUP_DIM = D_MODEL // len(POOL_WINDOWS)
POOL_MAX = 16
D_FF = 2816
FFN_CONV = 3
EPS = 1e-6
PAGE_SIZE = 128
MAIN_COLS = 3 * ATT_WIDTH + SSD_INNER + SSD_CONV_DIM

LANES = 128
SUBLANES = 8
VMEM_LIMIT_BYTES = 56 * 1024 * 1024

ROW_TILE = 512
ATTN_Q_TILE = 256
SSD_CHUNK = 128
SSD_CHUNKS_PER_STEP = 4
FFN_CHUNK = 256
PAGES_PER_STEP = 16

ATTN_KV_BLOCKS = 4
ATTN_TILES_PER_STEP = 4

NEG_INF = float("-inf")
MASKED = -1e30
LOG2_E = 1.4426950408889634


def _cparams(n_axes, flags=None):
    return pltpu.CompilerParams(dimension_semantics=("arbitrary",) * n_axes,
                                vmem_limit_bytes=VMEM_LIMIT_BYTES, flags=flags)


def _silu(x):
    return x / (1.0 + jnp.exp(-x))


def _softplus(x):
    return jnp.maximum(x, 0.0) + jnp.log1p(jnp.exp(-jnp.abs(x)))


def _rms(x, g):
    return x * lax.rsqrt(jnp.mean(x * x, axis=-1, keepdims=True) + EPS) * g


def _norm_mod(x, g, shift, scale):
    return _rms(x, g) * (1.0 + scale) + shift


def _dot_nt(a, b, precision=None):
    return lax.dot_general(a, b, (((1,), (1,)), ((), ())), preferred_element_type=F32, precision=precision)


def _dot_tn(a, b):
    return lax.dot_general(a, b, (((0,), (0,)), ((), ())), preferred_element_type=F32)


def _rope_tables(pos, shape):
    lane = lax.broadcasted_iota(jnp.int32, shape, 1)
    j = lane & (HEAD_DIM - 1)
    half = ROT_DIM // 2
    inv = jnp.exp((lane & (half - 1)).astype(F32) * (-2.0 * math.log(ROPE_THETA) / ROT_DIM))
    ang = pos * inv
    c, s = jnp.cos(ang), jnp.sin(ang)
    cos_t = jnp.where(j < ROT_DIM, c, 1.0)
    sin_t = jnp.where(j < half, -s, jnp.where(j < ROT_DIM, s, 0.0))
    return cos_t, sin_t, j < half


def _rope_group(xg, cos_t, sin_t, low):
    half = ROT_DIM // 2
    partner = jnp.where(low, pltpu.roll(xg, LANES - half, 1), pltpu.roll(xg, half, 1))
    return xg * cos_t + partner * sin_t


def _ada_kernel(c_ref, w_ref, b_ref, o_ref):
    o_ref[...] = jnp.dot(_silu(c_ref[...]), w_ref[...], preferred_element_type=F32, precision=HIGHEST) + b_ref[...]


def _ada_call(c_all, ada_w, ada_b):
    nl, ns, d, d3 = ada_w.shape
    n = c_all.shape[0]
    tn = 1024
    return pl.pallas_call(
        _ada_kernel,
        grid=(nl * ns, d3 // tn),
        in_specs=[pl.BlockSpec((n, d), lambda i, j: (0, 0)),
                  pl.BlockSpec((None, None, d, tn), lambda i, j: (i // ns, i % ns, 0, j)),
                  pl.BlockSpec((None, 1, tn), lambda i, j: (i, 0, j))],
        out_specs=pl.BlockSpec((None, n, tn), lambda i, j: (i, 0, j)),
        out_shape=jax.ShapeDtypeStruct((nl * ns, n, d3), F32),
        compiler_params=_cparams(2), name="ada_mod",
    )(c_all, ada_w, ada_b.reshape(nl * ns, 1, d3))


def _mixin_prompt_kernel(x_ref, sh_ref, sc_ref, g_ref, w_ref, wdt_ref, dtb_ref, cw_ref, cb_ref,
                         q_ref, k_ref, v_ref, z_ref, xs_ref, bc_ref, dt_ref, km_ref, cst_ref,
                         buf_ref, oc_ref, os_ref, *, tm):
    t = pl.program_id(1)
    lane = lax.broadcasted_iota(jnp.int32, (1, LANES), 1)
    j = lane & (HEAD_DIM - 1)
    half = ROT_DIM // 2
    inv = jnp.exp((lane & (half - 1)).astype(F32) * (-2.0 * math.log(ROPE_THETA) / ROT_DIM))
    rot = jnp.where(j < ROT_DIM, 1.0, 0.0)
    sgn = jnp.where(j < half, -1.0, rot)
    low = j < half

    @pl.when(t == 0)
    def _():
        buf_ref[0:SUBLANES, :] = jnp.zeros((SUBLANES, SSD_CONV_DIM), F32)
        ang = lax.broadcasted_iota(jnp.int32, (tm, LANES), 0).astype(F32) * inv
        oc_ref[...] = jnp.cos(ang)
        os_ref[...] = jnp.sin(ang)

    h = _norm_mod(x_ref[...], g_ref[...], sh_ref[...], sc_ref[...])
    proj = jnp.dot(h.astype(BF16), w_ref[...], preferred_element_type=F32)

    base = (t * tm).astype(F32) * inv
    cb, sb = jnp.cos(base), jnp.sin(base)
    oc, osn = oc_ref[...], os_ref[...]
    cos_t = (cb * rot) * oc - (sb * rot) * osn + (1.0 - rot)
    sin_t = (sb * sgn) * oc + (cb * sgn) * osn
    for gi in range(ATT_WIDTH // LANES):
        lo = gi * LANES
        q_ref[:, lo:lo + LANES] = _rope_group(proj[:, lo:lo + LANES], cos_t, sin_t, low)
        k_ref[:, lo:lo + LANES] = _rope_group(proj[:, ATT_WIDTH + lo:ATT_WIDTH + lo + LANES], cos_t, sin_t, low)
    v_ref[...] = proj[:, 2 * ATT_WIDTH:3 * ATT_WIDTH]
    z_ref[...] = proj[:, 3 * ATT_WIDTH:3 * ATT_WIDTH + SSD_INNER]
    for blk in range(tm // MOBA_BLOCK):
        kb = k_ref[blk * MOBA_BLOCK:(blk + 1) * MOBA_BLOCK, :]
        km_ref[blk] = jnp.sum(kb, axis=0, keepdims=True) * (1.0 / MOBA_BLOCK)

    raw = proj[:, 3 * ATT_WIDTH + SSD_INNER:MAIN_COLS]
    buf_ref[SUBLANES:SUBLANES + tm, :] = raw
    conv = cb_ref[...] + cw_ref[SSD_CONV - 1:SSD_CONV, :] * raw
    for i in range(1, SSD_CONV):
        conv = conv + cw_ref[SSD_CONV - 1 - i:SSD_CONV - i, :] * buf_ref[SUBLANES - i:SUBLANES - i + tm, :]
    act = _silu(conv)
    xs_ref[...] = act[:, :SSD_INNER]
    bc_ref[...] = act[:, SSD_INNER:]
    cst_ref[...] = raw[tm - SUBLANES:tm, :]
    buf_ref[0:SUBLANES, :] = raw[tm - SUBLANES:tm, :]

    dt_raw = _dot_nt(wdt_ref[...], h, precision=HIGHEST)
    dt_ref[...] = _softplus(dt_raw + dtb_ref[...])


def _mixin_prompt_call(x, shift, scale, g, w_main, wdt_t, dtb_col, conv_w, conv_b):
    b, t, d = x.shape
    tm = min(ROW_TILE, t)
    nt = t // tm
    nbt = tm // MOBA_BLOCK
    row_spec = lambda c: pl.BlockSpec((None, tm, c), lambda bi, ti: (bi, ti, 0))
    mod_spec = pl.BlockSpec((None, 1, d), lambda bi, ti: (bi, 0, 0))
    full = lambda a: pl.BlockSpec(a.shape, lambda bi, ti: (0,) * a.ndim)
    outs = pl.pallas_call(
        functools.partial(_mixin_prompt_kernel, tm=tm),
        grid=(b, nt),
        in_specs=[row_spec(d), mod_spec, mod_spec, full(g), full(w_main), full(wdt_t), full(dtb_col),
                  full(conv_w), full(conv_b)],
        out_specs=[row_spec(ATT_WIDTH), row_spec(ATT_WIDTH), row_spec(ATT_WIDTH), row_spec(SSD_INNER),
                   row_spec(SSD_INNER), row_spec(SSD_CONV_DIM - SSD_INNER),
                   pl.BlockSpec((None, SSD_HEADS, tm), lambda bi, ti: (bi, 0, ti)),
                   pl.BlockSpec((nbt, 1, ATT_WIDTH), lambda bi, ti: (bi * nt + ti, 0, 0)),
                   pl.BlockSpec((None, SUBLANES, SSD_CONV_DIM), lambda bi, ti: (bi, 0, 0))],
        out_shape=[jax.ShapeDtypeStruct((b, t, ATT_WIDTH), F32)] * 3
        + [jax.ShapeDtypeStruct((b, t, SSD_INNER), F32)] * 2
        + [jax.ShapeDtypeStruct((b, t, SSD_CONV_DIM - SSD_INNER), F32),
           jax.ShapeDtypeStruct((b, SSD_HEADS, t), F32),
           jax.ShapeDtypeStruct((b * nt * nbt, 1, ATT_WIDTH), F32),
           jax.ShapeDtypeStruct((b, SUBLANES, SSD_CONV_DIM), F32)],
        scratch_shapes=[pltpu.VMEM((tm + SUBLANES, SSD_CONV_DIM), F32), pltpu.VMEM((tm, LANES), F32),
                        pltpu.VMEM((tm, LANES), F32)],
        compiler_params=_cparams(2), name="mixin_prompt",
    )(x, shift, scale, g, w_main, wdt_t, dtb_col, conv_w, conv_b)
    q, k, v, z, xs, bc, dt_t, kmean, cst = outs
    return q, k, v, z, xs, bc, dt_t, kmean.reshape(b, t // MOBA_BLOCK, ATT_WIDTH), cst


def _topk_first(gate, ids, n_valid, axis):
    picks = []
    for r in range(MOBA_TOPK):
        mx = jnp.max(gate, axis=axis, keepdims=True)
        idx = jnp.min(jnp.where(gate == mx, ids, 1e9), axis=axis, keepdims=True)
        idx = jnp.where(r < n_valid, idx, -1.0)
        picks.append(idx)
        gate = jnp.where(ids == idx, NEG_INF, gate)
    return picks


def _attn_prompt_kernel(q_ref, k_ref, v_ref, km_ref, o_ref, ka_scr, vt_scr, acc_scr, *, tq, t):
    i = pl.program_id(2)
    kv = ATTN_KV_BLOCKS * MOBA_BLOCK
    nb = km_ref.shape[0]

    @pl.when(i == 0)
    def _():
        blk_lane = lax.broadcasted_iota(jnp.int32, (MOBA_BLOCK, LANES), 1)
        sub = lax.broadcasted_iota(jnp.int32, (2 * SUBLANES, MOBA_BLOCK), 0)
        ones_rows = jnp.where(sub == 0, 1.0, 0.0).astype(BF16)

        def prep(c, _):
            r0 = pl.multiple_of(c * MOBA_BLOCK, MOBA_BLOCK)
            ka_scr[pl.ds(r0, MOBA_BLOCK), 0:LANES] = k_ref[pl.ds(r0, MOBA_BLOCK), :].astype(BF16)
            ka_scr[pl.ds(r0, MOBA_BLOCK), LANES:2 * LANES] = jnp.where(blk_lane == c, 1.0, 0.0).astype(BF16)
            vt_scr[0:LANES, pl.ds(r0, MOBA_BLOCK)] = v_ref[pl.ds(r0, MOBA_BLOCK), :].T.astype(BF16)
            vt_scr[LANES:LANES + 2 * SUBLANES, pl.ds(r0, MOBA_BLOCK)] = ones_rows
            return 0

        lax.fori_loop(0, t // MOBA_BLOCK, prep, 0)

    n = 2 * tq
    nch = ATTN_TILES_PER_STEP
    km = jnp.concatenate([km_ref[...], jnp.zeros((LANES - nb, LANES), F32)], axis=0)
    lane = lax.broadcasted_iota(jnp.int32, (tq, LANES), 1)
    ids = lax.broadcasted_iota(jnp.int32, (LANES, n), 0).astype(F32)
    key = lax.broadcasted_iota(jnp.int32, (tq, n), 0)
    qry = lax.broadcasted_iota(jnp.int32, (tq, n), 1) & (tq - 1)
    qas, ms = [], []
    for c in range(nch):
        jown = i * nch + c
        q = q_ref[c * tq:(c + 1) * tq, :]
        qcat = jnp.concatenate([jnp.where(lane < HEAD_DIM, q, 0.0), jnp.where(lane >= HEAD_DIM, q, 0.0)], axis=0)
        gate = _dot_nt(km, qcat, precision=HIGHEST)
        gate = jnp.where(ids < jown.astype(F32), gate, NEG_INF)
        p0, p1, p2 = _topk_first(gate, ids, jown, 0)
        bias = jnp.where((ids == p0) | (ids == p1) | (ids == p2), 0.0, MASKED).T
        qs = qcat * (LOG2_E / math.sqrt(HEAD_DIM))
        qas.append(jnp.concatenate([qs, bias], axis=1).astype(BF16))

        r0 = pl.multiple_of(jown * tq, tq)
        st = jnp.where(key <= qry, _dot_nt(ka_scr[pl.ds(r0, tq), 0:LANES], qs.astype(BF16)), NEG_INF)
        m = jnp.max(st, axis=0, keepdims=True)
        p = jnp.exp2(st - m)
        acc_scr[c] = jnp.dot(vt_scr[:, pl.ds(r0, tq)], p.astype(BF16), preferred_element_type=F32)
        ms.append(m)

    def body(j, ms):
        c0 = pl.multiple_of(j * kv, kv)
        sts = [_dot_nt(ka_scr[pl.ds(c0, kv), :], qas[c]) for c in range(nch)]
        out = []
        for c in range(nch):
            m_new = jnp.maximum(ms[c], jnp.max(sts[c], axis=0, keepdims=True))
            p = jnp.exp2(sts[c] - m_new)
            acc_scr[c] = jnp.exp2(ms[c] - m_new) * acc_scr[c] + jnp.dot(
                vt_scr[:, pl.ds(c0, kv)], p.astype(BF16), preferred_element_type=F32)
            out.append(m_new)
        return tuple(out)

    last_own = i * nch + nch - 1
    lax.fori_loop(0, (last_own + ATTN_KV_BLOCKS - 1) // ATTN_KV_BLOCKS, body, tuple(ms))
    row = lax.broadcasted_iota(jnp.int32, (LANES, tq), 0)
    for c in range(nch):
        acc = acc_scr[c]
        ot = acc[0:LANES, :] / acc[LANES:LANES + 1, :]
        o_ref[c * tq:(c + 1) * tq, :] = jnp.where(row < HEAD_DIM, ot[:, :tq], ot[:, tq:]).T


def _attn_prompt_call(q, k, v, kmean):
    b, t, _ = q.shape
    tq = min(ATTN_Q_TILE, t)
    nb = kmean.shape[1]
    nch = ATTN_TILES_PER_STEP
    assert tq == MOBA_BLOCK and tq & (tq - 1) == 0 and nb <= LANES and t % (ATTN_KV_BLOCKS * MOBA_BLOCK) == 0
    assert t % (nch * tq) == 0
    npair = ATT_WIDTH // LANES
    return pl.pallas_call(
        functools.partial(_attn_prompt_kernel, tq=tq, t=t),
        scratch_shapes=[pltpu.VMEM((t, 2 * LANES), BF16), pltpu.VMEM((LANES + 2 * SUBLANES, t), BF16),
                        pltpu.VMEM((nch, LANES + 2 * SUBLANES, 2 * tq), F32)],
        grid=(b, npair, t // (nch * tq)),
        in_specs=[pl.BlockSpec((None, nch * tq, LANES), lambda bi, hp, i: (bi, i, hp)),
                  pl.BlockSpec((None, t, LANES), lambda bi, hp, i: (bi, 0, hp)),
                  pl.BlockSpec((None, t, LANES), lambda bi, hp, i: (bi, 0, hp)),
                  pl.BlockSpec((None, nb, LANES), lambda bi, hp, i: (bi, 0, hp))],
        out_specs=pl.BlockSpec((None, nch * tq, LANES), lambda bi, hp, i: (bi, i, hp)),
        out_shape=jax.ShapeDtypeStruct((b, t, ATT_WIDTH), F32),
        compiler_params=_cparams(3), name="moba_prompt",
    )(q, k, v, kmean)


def _gated_norm(y, xs, z, dsk, nw):
    yg = (y + dsk * xs) * _silu(z)
    return _rms(yg, nw)


def _ssd_prompt_kernel(xs_ref, bc_ref, dt_ref, z_ref, alog_ref, dsk_ref, nw_ref, y_ref, st_ref, h_scr, *, L, n_sub):
    c = pl.program_id(1)

    @pl.when(c == 0)
    def _():
        h_scr[...] = jnp.zeros(h_scr.shape, F32)

    a = -jnp.exp(alog_ref[...])
    for sub in range(n_sub):
        _ssd_chunk(sub * L, L, a, xs_ref, bc_ref, dt_ref, z_ref, dsk_ref, nw_ref, y_ref, h_scr)
    st_ref[...] = h_scr[...]


def _ssd_chunk(r0, L, a, xs_ref, bc_ref, dt_ref, z_ref, dsk_ref, nw_ref, y_ref, h_scr):
    dt = dt_ref[:, r0:r0 + L]
    lane = lax.broadcasted_iota(jnp.int32, dt.shape, 1)
    acum = dt * a
    s = 1
    while s < L:
        acum = acum + jnp.where(lane >= s, pltpu.roll(acum, s, 1), 0.0)
        s *= 2
    rows = jnp.concatenate([acum, dt, jnp.zeros((L - 2 * SSD_HEADS, L), F32)], axis=0)
    cols = rows.T

    xs = xs_ref[r0:r0 + L, :]
    bc = bc_ref[r0:r0 + L, :]
    ri = lax.broadcasted_iota(jnp.int32, (L, L), 0)
    ci = lax.broadcasted_iota(jnp.int32, (L, L), 1)
    causal = ri >= ci
    low = lax.broadcasted_iota(jnp.int32, (L, LANES), 1) < SSD_HEADDIM
    low_rows = lax.broadcasted_iota(jnp.int32, (LANES, SSD_STATE), 0) < SSD_HEADDIM
    gs = SSD_GROUPS * SSD_STATE
    heads_per_group = SSD_HEADS // SSD_GROUPS
    ys = []
    for g in range(SSD_GROUPS):
        bg = bc[:, g * SSD_STATE:(g + 1) * SSD_STATE].astype(BF16)
        cg = bc[:, gs + g * SSD_STATE:gs + (g + 1) * SSD_STATE].astype(BF16)
        cbt = _dot_nt(cg, bg)
        for pr in range(heads_per_group // 2):
            hp = g * (heads_per_group // 2) + pr
            xp = xs[:, hp * LANES:(hp + 1) * LANES]
            xpb = xp.astype(BF16)
            hst = h_scr[hp * LANES:(hp + 1) * LANES, :]
            inter = _dot_nt(cg, hst.astype(BF16))
            intra, ecols, wcols, cds = [], [], [], []
            for hh in range(2):
                hd = 2 * hp + hh
                col = cols[:, hd:hd + 1]
                row = acum[hd:hd + 1, :]
                decay = jnp.exp(jnp.where(causal, col - row, NEG_INF))
                m = cbt * decay * dt[hd:hd + 1, :]
                intra.append(jnp.dot(m.astype(BF16), xpb, preferred_element_type=F32))
                last = acum[hd:hd + 1, L - 1:L]
                ecols.append(jnp.exp(col))
                wcols.append(jnp.exp(last - col) * cols[:, SSD_HEADS + hd:SSD_HEADS + hd + 1])
                cds.append(jnp.exp(last))
            ys.append(jnp.where(low, intra[0], intra[1]) + inter * jnp.where(low, ecols[0], ecols[1]))
            xw = (xp * jnp.where(low, wcols[0], wcols[1])).astype(BF16)
            upd = _dot_tn(xw, bg)
            h_scr[hp * LANES:(hp + 1) * LANES, :] = hst * jnp.where(low_rows, cds[0], cds[1]) + upd
    y = jnp.concatenate(ys, axis=1)
    y_ref[r0:r0 + L, :] = _gated_norm(y, xs, z_ref[r0:r0 + L, :], dsk_ref[...], nw_ref[...])


def _ssd_prompt_call(xs, bc, dt_t, z, alog_col, dsk_row, nw_row):
    b, t, _ = xs.shape
    L = SSD_CHUNK
    n_sub = SSD_CHUNKS_PER_STEP
    rows = L * n_sub
    assert t % rows == 0
    row_spec = lambda c: pl.BlockSpec((None, rows, c), lambda bi, ci: (bi, ci, 0))
    full = lambda a: pl.BlockSpec(a.shape, lambda bi, ci: (0,) * a.ndim)
    return pl.pallas_call(
        functools.partial(_ssd_prompt_kernel, L=L, n_sub=n_sub),
        grid=(b, t // rows),
        in_specs=[row_spec(SSD_INNER), row_spec(SSD_CONV_DIM - SSD_INNER),
                  pl.BlockSpec((None, SSD_HEADS, rows), lambda bi, ci: (bi, 0, ci)),
                  row_spec(SSD_INNER), full(alog_col), full(dsk_row), full(nw_row)],
        out_specs=[row_spec(SSD_INNER),
                   pl.BlockSpec((None, SSD_INNER, SSD_STATE), lambda bi, ci: (bi, 0, 0))],
        out_shape=[jax.ShapeDtypeStruct((b, t, SSD_INNER), F32),
                   jax.ShapeDtypeStruct((b, SSD_INNER, SSD_STATE), F32)],
        scratch_shapes=[pltpu.VMEM((SSD_INNER, SSD_STATE), F32)],
        compiler_params=_cparams(2), name="ssd_prompt",
    )(xs, bc, dt_t, z, alog_col, dsk_row, nw_row)


def _mixout_kernel(att_ref, y_ref, x_ref, gt_ref, gp_ref, wa_ref, wy_ref, o_ref):
    out = (jnp.dot(att_ref[...].astype(BF16), wa_ref[...], preferred_element_type=F32)
           + jnp.dot(y_ref[...].astype(BF16), wy_ref[...], preferred_element_type=F32))
    o_ref[...] = x_ref[...] + gt_ref[...] * _rms(out, gp_ref[...])


def _mixout_call(att, y, x, gate, gp, wa, wy, *, per_seq_mod):
    b, t, d = x.shape
    tm = min(ROW_TILE, t)
    row_spec = lambda c: pl.BlockSpec((None, tm, c), lambda bi, ti: (bi, ti, 0))
    mod_spec = (pl.BlockSpec((None, 1, d), lambda bi, ti: (bi, 0, 0)) if per_seq_mod
                else pl.BlockSpec((None, tm, d), lambda bi, ti: (bi, ti, 0)))
    full = lambda a: pl.BlockSpec(a.shape, lambda bi, ti: (0,) * a.ndim)
    return pl.pallas_call(
        _mixout_kernel,
        grid=(b, t // tm),
        in_specs=[row_spec(ATT_WIDTH), row_spec(SSD_INNER), row_spec(d), mod_spec, full(gp), full(wa), full(wy)],
        out_specs=row_spec(d),
        out_shape=jax.ShapeDtypeStruct((b, t, d), F32),
        compiler_params=_cparams(2), name="mix_out",
    )(att, y, x, gate, gp, wa, wy)


def _ffn_kernel(*refs, tm, decode):
    if decode:
        (x_ref, sh_ref, sc_ref, gt_ref, g_ref, gp_ref, up_ref, cw_ref, cb_ref, dn_ref, p0_ref, p1_ref,
         o_ref, u_ref, ubuf0_ref, ubuf1_ref, acc_ref, hb_ref) = refs
    else:
        (x_ref, sh_ref, sc_ref, gt_ref, g_ref, gp_ref, up_ref, cw_ref, cb_ref, dn_ref,
         o_ref, cst_ref, carry_ref, ubuf0_ref, ubuf1_ref, acc_ref, hb_ref) = refs
        t = pl.program_id(1)

        @pl.when(t == 0)
        def _():
            carry_ref[...] = jnp.zeros(carry_ref.shape, F32)

    x = x_ref[...]
    hb_ref[...] = _norm_mod(x, g_ref[...], sh_ref[...], sc_ref[...]).astype(BF16)
    acc_ref[...] = jnp.zeros(acc_ref.shape, F32)
    fc = FFN_CHUNK
    n_chunks = D_FF // fc
    ubufs = (ubuf0_ref, ubuf1_ref)

    def col_offset(j, half):
        return pl.multiple_of(half * D_FF + j * fc, LANES)

    def up_project(j, ubuf):
        hb = hb_ref[...]
        for half in range(2):
            off, lo = col_offset(j, half), half * fc
            raw = jnp.dot(hb, up_ref[:, pl.ds(off, fc)], preferred_element_type=F32)
            ubuf[SUBLANES:SUBLANES + tm, lo:lo + fc] = raw
            if decode:
                u_ref[:, pl.ds(off, fc)] = raw
            else:
                ubuf[0:SUBLANES, lo:lo + fc] = carry_ref[:, pl.ds(off, fc)]
                carry_ref[:, pl.ds(off, fc)] = raw[tm - SUBLANES:tm, :]

    def conv_gate_down(j, ubuf):
        halves = []
        for half in range(2):
            off, lo = col_offset(j, half), half * fc
            cw = cw_ref[:, pl.ds(off, fc)]
            conv = cb_ref[:, pl.ds(off, fc)] + cw[FFN_CONV - 1:FFN_CONV, :] * ubuf[SUBLANES:SUBLANES + tm, lo:lo + fc]
            if decode:
                conv = conv + cw[1:2, :] * p1_ref[:, pl.ds(off, fc)] + cw[0:1, :] * p0_ref[:, pl.ds(off, fc)]
            else:
                for i in range(1, FFN_CONV):
                    conv = conv + (cw[FFN_CONV - 1 - i:FFN_CONV - i, :]
                                   * ubuf[SUBLANES - i:SUBLANES - i + tm, lo:lo + fc])
            halves.append(conv)
        act = _silu(halves[0]) * halves[1]
        r0 = pl.multiple_of(j * fc, fc)
        acc_ref[...] += jnp.dot(act.astype(BF16), dn_ref[pl.ds(r0, fc), :], preferred_element_type=F32)

    up_project(0, ubufs[0])

    def pair(i, _):
        up_project(2 * i + 1, ubufs[1])
        conv_gate_down(2 * i, ubufs[0])
        up_project(2 * i + 2, ubufs[0])
        conv_gate_down(2 * i + 1, ubufs[1])
        return 0

    assert n_chunks % 2 == 1
    lax.fori_loop(0, n_chunks // 2, pair, 0)
    conv_gate_down(n_chunks - 1, ubufs[0])
    o_ref[...] = x + gt_ref[...] * _rms(acc_ref[...], gp_ref[...])
    if not decode:
        cst_ref[...] = carry_ref[...]


def _ffn_call(x, shift, scale, gate, g, gp, up_w, conv_w, conv_b, dn_w, prev=None):
    b, t, d = x.shape
    decode = prev is not None
    tm = min(ROW_TILE, t)
    row_spec = lambda c: pl.BlockSpec((None, tm, c), lambda bi, ti: (bi, ti, 0))
    mod_spec = (pl.BlockSpec((None, tm, d), lambda bi, ti: (bi, ti, 0)) if decode
                else pl.BlockSpec((None, 1, d), lambda bi, ti: (bi, 0, 0)))
    full = lambda a: pl.BlockSpec(a.shape, lambda bi, ti: (0,) * a.ndim)
    in_specs = [row_spec(d), mod_spec, mod_spec, mod_spec, full(g), full(gp), full(up_w), full(conv_w),
                full(conv_b), full(dn_w)]
    args = [x, shift, scale, gate, g, gp, up_w, conv_w, conv_b, dn_w]
    scratch = [pltpu.VMEM((tm + SUBLANES, 2 * FFN_CHUNK), F32)] * 2 + [pltpu.VMEM((tm, d), F32), pltpu.VMEM((tm, d), BF16)]
    if decode:
        assert b == 1 and t == tm
        in_specs += [pl.BlockSpec((tm, 2 * D_FF), lambda bi, ti: (ti, 0))] * 2
        args += list(prev)
        out_specs = [row_spec(d), pl.BlockSpec((tm, 2 * D_FF), lambda bi, ti: (ti, 0))]
        out_shape = [jax.ShapeDtypeStruct((b, t, d), F32), jax.ShapeDtypeStruct((t, 2 * D_FF), F32)]
    else:
        out_specs = [row_spec(d), pl.BlockSpec((None, SUBLANES, 2 * D_FF), lambda bi, ti: (bi, 0, 0))]
        out_shape = [jax.ShapeDtypeStruct((b, t, d), F32), jax.ShapeDtypeStruct((b, SUBLANES, 2 * D_FF), F32)]
        scratch = [pltpu.VMEM((SUBLANES, 2 * D_FF), F32)] + scratch
    return pl.pallas_call(
        functools.partial(_ffn_kernel, tm=tm, decode=decode),
        grid=(b, t // tm), in_specs=in_specs, out_specs=out_specs, out_shape=out_shape,
        scratch_shapes=scratch, compiler_params=_cparams(2),
        name="conv_ffn_sample" if decode else "conv_ffn_prompt",
    )(*args)


def _pool_kernel(*refs, tm, decode, pos0):
    if decode:
        (x_ref, sh_ref, sc_ref, gt_ref, g_ref, gp_ref, pw_ref, pb_ref, ps_ref, prev_ref, o_ref, h_ref) = refs
    else:
        (x_ref, sh_ref, sc_ref, gt_ref, g_ref, gp_ref, pw_ref, pb_ref, ps_ref, o_ref, last_ref, buf_ref) = refs
        t = pl.program_id(1)

        @pl.when(t == 0)
        def _():
            buf_ref[0:POOL_MAX, :] = jnp.zeros((POOL_MAX, D_MODEL), F32)

    x = x_ref[...]
    h = _norm_mod(x, g_ref[...], sh_ref[...], sc_ref[...])
    if decode:
        h_ref[...] = h
    else:
        buf_ref[POOL_MAX:POOL_MAX + tm, :] = h
        row = lax.broadcasted_iota(jnp.int32, (tm, 1), 0)
        pos = pos0 + t * tm + row
    ys = []
    for gi, win in enumerate(POOL_WINDOWS):
        lo = gi * POOL_GROUP_DIM
        hg = h[:, lo:lo + POOL_GROUP_DIM]
        wsum = hg
        for i in range(1, win):
            if decode:
                wsum = wsum + prev_ref[POOL_MAX - 1 - i, :, lo:lo + POOL_GROUP_DIM]
            else:
                wsum = wsum + buf_ref[POOL_MAX - i:POOL_MAX - i + tm, lo:lo + POOL_GROUP_DIM]
        if decode:
            cnt = float(min(pos0 + 1, win))
        else:
            cnt = jnp.minimum(pos + 1, win).astype(F32)
        pooled = wsum / cnt - hg
        ys.append(jnp.dot(pooled.astype(BF16), pw_ref[gi], preferred_element_type=F32))
    y = (jnp.concatenate(ys, axis=1) + pb_ref[...]) * ps_ref[...]
    o_ref[...] = x + gt_ref[...] * _rms(y, gp_ref[...])
    if not decode:
        last_ref[...] = h[tm - POOL_MAX:tm, :]
        buf_ref[0:POOL_MAX, :] = h[tm - POOL_MAX:tm, :]


def _pool_call(x, shift, scale, gate, g, gp, pw, pb, ps, prev_t=None, pos0=0):
    b, t, d = x.shape
    decode = prev_t is not None
    tm = min(ROW_TILE, t)
    row_spec = lambda c: pl.BlockSpec((None, tm, c), lambda bi, ti: (bi, ti, 0))
    mod_spec = (pl.BlockSpec((None, tm, d), lambda bi, ti: (bi, ti, 0)) if decode
                else pl.BlockSpec((None, 1, d), lambda bi, ti: (bi, 0, 0)))
    full = lambda a: pl.BlockSpec(a.shape, lambda bi, ti: (0,) * a.ndim)
    in_specs = [row_spec(d), mod_spec, mod_spec, mod_spec, full(g), full(gp), full(pw), full(pb), full(ps)]
    args = [x, shift, scale, gate, g, gp, pw, pb, ps]
    if decode:
        assert b == 1 and t == tm
        in_specs.append(full(prev_t))
        args.append(prev_t)
        out_specs = [row_spec(d), row_spec(d)]
        out_shape = [jax.ShapeDtypeStruct((b, t, d), F32)] * 2
        scratch = []
    else:
        out_specs = [row_spec(d), pl.BlockSpec((None, POOL_MAX, d), lambda bi, ti: (bi, 0, 0))]
        out_shape = [jax.ShapeDtypeStruct((b, t, d), F32), jax.ShapeDtypeStruct((b, POOL_MAX, d), F32)]
        scratch = [pltpu.VMEM((tm + POOL_MAX, d), F32)]
    return pl.pallas_call(
        functools.partial(_pool_kernel, tm=tm, decode=decode, pos0=pos0),
        grid=(b, t // tm), in_specs=in_specs, out_specs=out_specs, out_shape=out_shape,
        scratch_shapes=scratch, compiler_params=_cparams(2),
        name="pool_sample" if decode else "pool_prompt",
    )(*args)


def _mixin_sample_kernel(x_ref, sh_ref, sc_ref, g_ref, w_ref, wdtx_ref, dtbx_ref, cw_ref, cb_ref, prev_ref,
                         q_ref, k_ref, v_ref, z_ref, xs_ref, bc_ref, dtx_ref, raw_ref, *, pos0):
    h = _norm_mod(x_ref[...], g_ref[...], sh_ref[...], sc_ref[...])
    proj = jnp.dot(h.astype(BF16), w_ref[...], preferred_element_type=F32)
    n = h.shape[0]
    cos_t, sin_t, low = _rope_tables(jnp.float32(pos0), (n, LANES))
    for gi in range(ATT_WIDTH // LANES):
        lo = gi * LANES
        q_ref[:, lo:lo + LANES] = _rope_group(proj[:, lo:lo + LANES], cos_t, sin_t, low)
        k_ref[:, lo:lo + LANES] = _rope_group(proj[:, ATT_WIDTH + lo:ATT_WIDTH + lo + LANES], cos_t, sin_t, low)
    v_ref[...] = proj[:, 2 * ATT_WIDTH:3 * ATT_WIDTH]
    z_ref[...] = proj[:, 3 * ATT_WIDTH:3 * ATT_WIDTH + SSD_INNER]
    raw = proj[:, 3 * ATT_WIDTH + SSD_INNER:MAIN_COLS]
    raw_ref[...] = raw
    conv = cb_ref[...] + cw_ref[SSD_CONV - 1:SSD_CONV, :] * raw
    for i in range(SSD_CONV - 1):
        conv = conv + cw_ref[i:i + 1, :] * prev_ref[i]
    act = _silu(conv)
    xs_ref[...] = act[:, :SSD_INNER]
    bc_ref[...] = act[:, SSD_INNER:]
    dtx_ref[...] = _softplus(jnp.dot(h, wdtx_ref[...], preferred_element_type=F32, precision=HIGHEST) + dtbx_ref[...])


def _mixin_sample_call(x, shift, scale, g, w_main, wdt_x, dtb_x, conv_w, conv_b, prev_t, pos0):
    n, d = x.shape
    full = lambda a: pl.BlockSpec(a.shape, lambda i: (0,) * a.ndim)
    args = [x, shift, scale, g, w_main, wdt_x, dtb_x, conv_w, conv_b, prev_t]
    widths = [ATT_WIDTH] * 3 + [SSD_INNER] * 2 + [SSD_CONV_DIM - SSD_INNER, SSD_INNER, SSD_CONV_DIM]
    return pl.pallas_call(
        functools.partial(_mixin_sample_kernel, pos0=pos0),
        grid=(1,),
        in_specs=[full(a) for a in args],
        out_specs=[pl.BlockSpec((n, c), lambda i: (0, 0)) for c in widths],
        out_shape=[jax.ShapeDtypeStruct((n, c), F32) for c in widths],
        compiler_params=_cparams(1), name="mixin_sample",
    )(*args)


_PAGES_PER_BLOCK = MOBA_BLOCK // PAGE_SIZE
_SEL_PAGES = MOBA_TOPK * _PAGES_PER_BLOCK
SAMPLE_HEADS_PER_STEP = 4


def _token_minor(cache):
    return jnp.transpose(cache, (0, 1, 3, 4, 2))


def _past_kmean_kernel(pt_ref, *refs):
    pages, o_ref = refs[:PAGES_PER_STEP], refs[PAGES_PER_STEP]
    step = pl.program_id(1)
    blocks_per_step = PAGES_PER_STEP // _PAGES_PER_BLOCK

    @pl.when(step == 0)
    def _():
        o_ref[...] = jnp.zeros(o_ref.shape, F32)

    acc = o_ref[...]
    lane = lax.broadcasted_iota(jnp.int32, acc.shape, 2)
    for blk in range(blocks_per_step):
        tot = pages[blk * _PAGES_PER_BLOCK][...]
        for r in range(1, _PAGES_PER_BLOCK):
            tot = tot + pages[blk * _PAGES_PER_BLOCK + r][...]
        mean = jnp.sum(tot, axis=-1, keepdims=True) * (1.0 / MOBA_BLOCK)
        acc = jnp.where(lane == step * blocks_per_step + blk, mean, acc)
    o_ref[...] = acc


def _past_kmean_call(pt_flat, cache_kt, n_seq, n_pages):
    steps = n_pages // PAGES_PER_STEP
    nb = n_pages // _PAGES_PER_BLOCK

    def page_map(b, s, pt, *, r):
        return (0, pt[b * n_pages + s * PAGES_PER_STEP + r], 0, 0, 0)

    grid_spec = pltpu.PrefetchScalarGridSpec(
        num_scalar_prefetch=1, grid=(n_seq, steps),
        in_specs=[pl.BlockSpec((None, None, ATT_HEADS, HEAD_DIM, PAGE_SIZE), functools.partial(page_map, r=r))
                  for r in range(PAGES_PER_STEP)],
        out_specs=pl.BlockSpec((None, ATT_HEADS, HEAD_DIM, nb), lambda b, s, pt: (b, 0, 0, 0)))
    return pl.pallas_call(
        _past_kmean_kernel, grid_spec=grid_spec,
        out_shape=jax.ShapeDtypeStruct((n_seq, ATT_HEADS, HEAD_DIM, nb), F32),
        compiler_params=_cparams(2), name="past_kmean",
    )(pt_flat, *([cache_kt] * PAGES_PER_STEP))


def _sample_select_kernel(q_ref, km_ref, sel_ref, *, jown):
    gate = jnp.sum(km_ref[...] * q_ref[...], axis=1, keepdims=True)
    ids = lax.broadcasted_iota(jnp.int32, gate.shape, 2).astype(F32)
    gate = jnp.where(ids < float(jown), gate, NEG_INF)
    picks = _topk_first(gate, ids, jown, 2)
    for r in range(MOBA_TOPK):
        sel_ref[r] = picks[r].astype(jnp.int32)


def _sample_select_call(q4, kmean, jown):
    n, _, _, nb = kmean.shape
    return pl.pallas_call(
        functools.partial(_sample_select_kernel, jown=jown),
        grid=(n,),
        in_specs=[pl.BlockSpec((None, ATT_HEADS, HEAD_DIM, 1), lambda b: (b, 0, 0, 0)),
                  pl.BlockSpec((None, ATT_HEADS, HEAD_DIM, nb), lambda b: (b, 0, 0, 0))],
        out_specs=pl.BlockSpec((None, MOBA_TOPK, ATT_HEADS, 1, 1), lambda b: (b, 0, 0, 0, 0)),
        out_shape=jax.ShapeDtypeStruct((n, MOBA_TOPK, ATT_HEADS, 1, 1), jnp.int32),
        compiler_params=_cparams(1), name="sample_select",
    )(q4, kmean)


def _sample_attn_kernel(pt_ref, sel_ref, q_ref, kn_ref, vn_ref, *refs):
    n_pg = SAMPLE_HEADS_PER_STEP * _SEL_PAGES
    kp, vp, o_ref = refs[:n_pg], refs[n_pg:2 * n_pg], refs[2 * n_pg]
    b, grp = pl.program_id(0), pl.program_id(1)
    scale = 1.0 / math.sqrt(HEAD_DIM)
    for hh in range(SAMPLE_HEADS_PER_STEP):
        h = grp * SAMPLE_HEADS_PER_STEP + hh
        q = q_ref[hh] * scale
        s_self = jnp.sum(q * kn_ref[hh], axis=0, keepdims=True)
        scores = []
        for r in range(MOBA_TOPK):
            valid = sel_ref[(b * MOBA_TOPK + r) * ATT_HEADS + h] >= 0
            for pg in range(_PAGES_PER_BLOCK):
                s = jnp.sum(kp[hh * _SEL_PAGES + r * _PAGES_PER_BLOCK + pg][...] * q, axis=0, keepdims=True)
                scores.append(jnp.where(valid, s, NEG_INF))
        m = s_self
        for s in scores:
            m = jnp.maximum(m, jnp.max(s, axis=1, keepdims=True))
        p_self = jnp.exp(s_self - m)
        l = p_self
        acc = p_self * vn_ref[hh]
        for idx, s in enumerate(scores):
            p = jnp.exp(s - m)
            l = l + jnp.sum(p, axis=1, keepdims=True)
            acc = acc + jnp.sum(vp[hh * _SEL_PAGES + idx][...] * p, axis=1, keepdims=True)
        o_ref[hh] = acc / l


def _sample_attn_call(pt_flat, sel_flat, q4, k4, v4, cache_kt, cache_vt, n_pages):
    n = q4.shape[0]
    hps = SAMPLE_HEADS_PER_STEP
    head_spec = pl.BlockSpec((None, hps, HEAD_DIM, 1), lambda b, g, pt, sel: (b, g, 0, 0))

    def page_map(b, g, pt, sel, *, hh, r, pg):
        h = g * hps + hh
        blk = jnp.maximum(sel[(b * MOBA_TOPK + r) * ATT_HEADS + h], 0)
        return (0, pt[b * n_pages + blk * _PAGES_PER_BLOCK + pg], h, 0, 0)

    page_specs = [pl.BlockSpec((None, None, None, HEAD_DIM, PAGE_SIZE), functools.partial(page_map, hh=hh, r=r, pg=pg))
                  for hh in range(hps) for r in range(MOBA_TOPK) for pg in range(_PAGES_PER_BLOCK)]
    grid_spec = pltpu.PrefetchScalarGridSpec(
        num_scalar_prefetch=2, grid=(n, ATT_HEADS // hps),
        in_specs=[head_spec, head_spec, head_spec] + page_specs + page_specs,
        out_specs=head_spec)
    n_pg = hps * _SEL_PAGES
    return pl.pallas_call(
        _sample_attn_kernel, grid_spec=grid_spec,
        out_shape=jax.ShapeDtypeStruct((n, ATT_HEADS, HEAD_DIM, 1), F32),
        compiler_params=_cparams(2), name="sample_attn",
    )(pt_flat, sel_flat, q4, k4, v4, *([cache_kt] * n_pg), *([cache_vt] * n_pg))


def _ssd_sample_kernel(xs_ref, bc_ref, dtx_ref, z_ref, st_ref, alogx_ref, dsk_ref, nw_ref, y_ref, sto_ref):
    xs, dtx = xs_ref[...], dtx_ref[...]
    decay = jnp.exp(dtx * (-jnp.exp(alogx_ref[...])))
    xdt = xs * dtx
    bc = bc_ref[...]
    rows_per_group = SSD_INNER // SSD_GROUPS
    gs = SSD_GROUPS * SSD_STATE
    eye = (lax.broadcasted_iota(jnp.int32, (rows_per_group, rows_per_group), 0)
           == lax.broadcasted_iota(jnp.int32, (rows_per_group, rows_per_group), 1))

    def to_col(v):
        return jnp.sum(jnp.where(eye, jnp.broadcast_to(v, eye.shape), 0.0), axis=1, keepdims=True)

    def to_row(v):
        return jnp.sum(jnp.where(eye, jnp.broadcast_to(v, eye.shape), 0.0), axis=0, keepdims=True)

    ys = []
    for g in range(SSD_GROUPS):
        lo = g * rows_per_group
        b_row = bc[:, g * SSD_STATE:(g + 1) * SSD_STATE]
        c_row = bc[:, gs + g * SSD_STATE:gs + (g + 1) * SSD_STATE]
        hg = st_ref[lo:lo + rows_per_group, :]
        hnew = to_col(decay[:, lo:lo + rows_per_group]) * hg + to_col(xdt[:, lo:lo + rows_per_group]) * b_row
        sto_ref[lo:lo + rows_per_group, :] = hnew
        ys.append(to_row(jnp.sum(hnew * c_row, axis=1, keepdims=True)))
    y = jnp.concatenate(ys, axis=1)
    y_ref[...] = _gated_norm(y, xs, z_ref[...], dsk_ref[...], nw_ref[...])


def _ssd_sample_call(xs, bc, dtx, z, state, alog_x, dsk_row, nw_row):
    n = xs.shape[0]
    r3 = lambda a: a.reshape(n, 1, a.shape[-1])
    row_spec = lambda c: pl.BlockSpec((None, 1, c), lambda b: (b, 0, 0))
    st_spec = pl.BlockSpec((None, SSD_INNER, SSD_STATE), lambda b: (b, 0, 0))
    full = lambda a: pl.BlockSpec(a.shape, lambda b: (0,) * a.ndim)
    y, st = pl.pallas_call(
        _ssd_sample_kernel,
        grid=(n,),
        in_specs=[row_spec(SSD_INNER), row_spec(SSD_CONV_DIM - SSD_INNER), row_spec(SSD_INNER), row_spec(SSD_INNER),
                  st_spec, full(alog_x), full(dsk_row), full(nw_row)],
        out_specs=[row_spec(SSD_INNER), st_spec],
        out_shape=[jax.ShapeDtypeStruct((n, 1, SSD_INNER), F32), jax.ShapeDtypeStruct((n, SSD_INNER, SSD_STATE), F32)],
        compiler_params=_cparams(1), name="ssd_sample",
    )(r3(xs), r3(bc), r3(dtx), r3(z), state, alog_x, dsk_row, nw_row)
    return y.reshape(n, SSD_INNER), st


def kernel(x_prompt, x_sample, cache_k, cache_v, state_ssm, state_ssd_conv, state_pool, state_ffn_conv, page_table,
           c_prompt, c_sample, ada_w, ada_b, norm_pre, norm_post, mix_in_w, mix_out_w, ssd_conv_w, ssd_conv_b,
           ssd_dt_bias, ssd_a_log, ssd_d, ssd_norm_w, pool_w, pool_b, pool_scale, ffn_up_w, ffn_conv_w,
           ffn_conv_b, ffn_down_w):
    bp, t, d = x_prompt.shape
    ns = x_sample.shape[0]
    n_pages = page_table.shape[1]
    past_len = n_pages * cache_k.shape[2]
    assert x_sample.shape[1] == 1 and cache_k.shape[2] == PAGE_SIZE and past_len % MOBA_BLOCK == 0
    assert t % ROW_TILE == 0 or t < ROW_TILE

    w_main = mix_in_w[0, :, :MAIN_COLS].astype(BF16)
    w_dt = mix_in_w[0, :, MAIN_COLS:]
    wdt_t = w_dt.T
    wdt_x = jnp.repeat(w_dt, SSD_HEADDIM, axis=1)
    dtb_col = ssd_dt_bias[0].reshape(SSD_HEADS, 1)
    dtb_x = jnp.repeat(ssd_dt_bias[0], SSD_HEADDIM).reshape(1, SSD_INNER)
    alog_col = ssd_a_log[0].reshape(SSD_HEADS, 1)
    alog_x = jnp.repeat(ssd_a_log[0], SSD_HEADDIM).reshape(1, SSD_INNER)
    dsk_row = jnp.repeat(ssd_d[0], SSD_HEADDIM).reshape(1, SSD_INNER)
    nw_row = ssd_norm_w[0].reshape(1, SSD_INNER)
    w_att = mix_out_w[0, :ATT_WIDTH].astype(BF16)
    w_ssd = mix_out_w[0, ATT_WIDTH:].astype(BF16)
    conv_w0 = ssd_conv_w[0]
    conv_b0 = ssd_conv_b[0].reshape(1, SSD_CONV_DIM)
    up_w = ffn_up_w.astype(BF16)
    dn_w = ffn_down_w.astype(BF16)
    pw = pool_w[0].astype(BF16)
    pb = pool_b[0].reshape(1, d)
    ps = pool_scale[0].reshape(1, d)

    mods = _ada_call(jnp.concatenate([c_prompt, c_sample], axis=0), ada_w, ada_b)

    def mod(layer, slot, sample):
        m = mods[layer * 2 + slot]
        m = m[bp:] if sample else m[:bp]
        parts = (m[:, :d], m[:, d:2 * d], m[:, 2 * d:])
        return tuple(p[None] for p in parts) if sample else tuple(p[:, None, :] for p in parts)

    npre = lambda layer, slot: norm_pre[layer, slot].reshape(1, d)
    npost = lambda layer, slot: norm_post[layer, slot].reshape(1, d)

    def ffn(x, layer, sample, prev=None):
        sh, sc, gt = mod(layer, 1, sample)
        return _ffn_call(x, sh, sc, gt, npre(layer, 1), npost(layer, 1), up_w[layer], ffn_conv_w[layer],
                         ffn_conv_b[layer].reshape(1, 2 * D_FF), dn_w[layer], prev=prev)

    sh, sc, gt = mod(0, 0, False)
    q, k, v, z, xs, bc, dt_t, kmean, cst = _mixin_prompt_call(
        x_prompt, sh, sc, npre(0, 0), w_main, wdt_t, dtb_col, conv_w0, conv_b0)
    att = _attn_prompt_call(q, k, v, kmean)
    y, ssm_p = _ssd_prompt_call(xs, bc, dt_t, z, alog_col, dsk_row, nw_row)
    xp = _mixout_call(att, y, x_prompt, gt, npost(0, 0), w_att, w_ssd, per_seq_mod=True)
    xp, fst0 = ffn(xp, 0, False)
    sh, sc, gt = mod(1, 0, False)
    xp, plast = _pool_call(xp, sh, sc, gt, npre(1, 0), npost(1, 0), pw, pb, ps)
    xp, fst1 = ffn(xp, 1, False)

    k_p = k.reshape(1, bp, t, ATT_HEADS, HEAD_DIM)
    v_p = v.reshape(1, bp, t, ATT_HEADS, HEAD_DIM)
    ssm_p = ssm_p.reshape(1, bp, SSD_HEADS, SSD_HEADDIM, SSD_STATE)
    sconv_p = cst[None, :, SUBLANES - (SSD_CONV - 1):, :]
    pool_p = plast[None, :, 1:, :]
    fconv_p = jnp.stack([fst0, fst1])[:, :, SUBLANES - (FFN_CONV - 1):, :]

    xs0 = x_sample.reshape(ns, d)
    sh, sc, gt = mod(0, 0, True)
    conv_prev_t = jnp.transpose(state_ssd_conv[0], (1, 0, 2))
    qs, ks, vs, zs, xss, bcs, dtx, raw = _mixin_sample_call(
        xs0, sh[0], sc[0], npre(0, 0), w_main, wdt_x, dtb_x, conv_w0, conv_b0, conv_prev_t, past_len)
    pt_flat = page_table.reshape(-1)
    cache_kt, cache_vt = _token_minor(cache_k), _token_minor(cache_v)
    kmean_s = _past_kmean_call(pt_flat, cache_kt, ns, n_pages)
    jown = past_len // MOBA_BLOCK
    as_cols = lambda a: a.reshape(ns, ATT_HEADS, HEAD_DIM, 1)
    sel = _sample_select_call(as_cols(qs), kmean_s, jown)
    att_s = _sample_attn_call(pt_flat, sel.reshape(-1), as_cols(qs), as_cols(ks), as_cols(vs),
                              cache_kt, cache_vt, n_pages).reshape(ns, ATT_WIDTH)
    y_s, ssm_s = _ssd_sample_call(xss, bcs, dtx, zs, state_ssm[0].reshape(ns, SSD_INNER, SSD_STATE),
                                  alog_x, dsk_row, nw_row)
    xq = _mixout_call(att_s[None], y_s[None], xs0[None], gt, npost(0, 0), w_att, w_ssd, per_seq_mod=False)
    xq, u0 = ffn(xq, 0, True, prev=(state_ffn_conv[0, :, 0], state_ffn_conv[0, :, 1]))
    sh, sc, gt = mod(1, 0, True)
    pool_prev_t = jnp.transpose(state_pool[0], (1, 0, 2))
    xq, h_pool = _pool_call(xq, sh, sc, gt, npre(1, 0), npost(1, 0), pw, pb, ps, prev_t=pool_prev_t, pos0=past_len)
    xq, u1 = ffn(xq, 1, True, prev=(state_ffn_conv[1, :, 0], state_ffn_conv[1, :, 1]))

    y_sample = xq.reshape(ns, 1, d)
    k_s = ks.reshape(1, ns, 1, ATT_HEADS, HEAD_DIM)
    v_s = vs.reshape(1, ns, 1, ATT_HEADS, HEAD_DIM)
    ssm_s = ssm_s.reshape(1, ns, SSD_HEADS, SSD_HEADDIM, SSD_STATE)
    sconv_s = jnp.concatenate([state_ssd_conv[0, :, 1:], raw[:, None, :]], axis=1)[None]
    pool_s = jnp.concatenate([state_pool[0, :, 1:], h_pool[0][:, None, :]], axis=1)[None]
    fconv_s = jnp.stack([jnp.stack([state_ffn_conv[0, :, 1], u0], axis=1),
                         jnp.stack([state_ffn_conv[1, :, 1], u1], axis=1)])
    return (xp, y_sample, k_p, v_p, ssm_p, sconv_p, pool_p, fconv_p, k_s, v_s, ssm_s, sconv_s, pool_s, fconv_s)
```

```python
import functools
import math

import jax
import jax.numpy as jnp
from jax import lax
from jax.experimental import pallas as pl
from jax.experimental.pallas import tpu as pltpu

F32 = jnp.float32
BF16 = jnp.bfloat16
HIGHEST = lax.Precision.HIGHEST

D_MODEL = 1024
ATT_HEADS = 8
HEAD_DIM = 64
ATT_WIDTH = ATT_HEADS * HEAD_DIM
ROT_DIM = HEAD_DIM // 4
ROPE_THETA = 500000.0
MOBA_BLOCK = 256
MOBA_TOPK = 3
SSD_HEADS = 8
SSD_HEADDIM = 64
SSD_INNER = SSD_HEADS * SSD_HEADDIM
SSD_GROUPS = 2
SSD_STATE = 128
SSD_CONV = 4
SSD_CONV_DIM = SSD_INNER + 2 * SSD_GROUPS * SSD_STATE
POOL_WINDOWS = (2, 4, 8, 16)
POOL_GROUP_DIM = D_MODEL // len(POOL_WINDOWS)
POOL_MAX = 16
D_FF = 2816
FFN_CONV = 3
EPS = 1e-6
PAGE_SIZE = 128
MAIN_COLS = 3 * ATT_WIDTH + SSD_INNER + SSD_CONV_DIM

LANES = 128
SUBLANES = 8
VMEM_LIMIT_BYTES = 56 * 1024 * 1024

ROW_TILE = 512
ATTN_Q_TILE = 256
SSD_CHUNK = 128
SSD_CHUNKS_PER_STEP = 4
FFN_CHUNK = 256
PAGES_PER_STEP = 32

ATTN_KV_BLOCKS = 4
ATTN_TILES_PER_STEP = 4

NEG_INF = float("-inf")
MASKED = -1e30
LOG2_E = 1.4426950408889634
VT_ROWS = HEAD_DIM + 2 * SUBLANES


def _cparams(n_axes, flags=None):
    return pltpu.CompilerParams(dimension_semantics=("arbitrary",) * n_axes,
                                vmem_limit_bytes=VMEM_LIMIT_BYTES, flags=flags)


def _silu(x):
    return x / (1.0 + jnp.exp(-x))


def _softplus(x):
    return jnp.maximum(x, 0.0) + jnp.log1p(jnp.exp(-jnp.abs(x)))


def _rms(x, g):
    return x * lax.rsqrt(jnp.mean(x * x, axis=-1, keepdims=True) + EPS) * g


def _norm_mod(x, g, shift, scale):
    return _rms(x, g) * (1.0 + scale) + shift


def _dot_nt(a, b, precision=None):
    return lax.dot_general(a, b, (((1,), (1,)), ((), ())), preferred_element_type=F32, precision=precision)


def _dot_tn(a, b):
    return lax.dot_general(a, b, (((0,), (0,)), ((), ())), preferred_element_type=F32)


def _rope_tables(pos, shape):
    lane = lax.broadcasted_iota(jnp.int32, shape, 1)
    j = lane & (HEAD_DIM - 1)
    half = ROT_DIM // 2
    inv = jnp.exp((lane & (half - 1)).astype(F32) * (-2.0 * math.log(ROPE_THETA) / ROT_DIM))
    ang = pos * inv
    c, s = jnp.cos(ang), jnp.sin(ang)
    cos_t = jnp.where(j < ROT_DIM, c, 1.0)
    sin_t = jnp.where(j < half, -s, jnp.where(j < ROT_DIM, s, 0.0))
    return cos_t, sin_t, j < half


def _rope_group(xg, cos_t, sin_t, low):
    half = ROT_DIM // 2
    partner = jnp.where(low, pltpu.roll(xg, LANES - half, 1), pltpu.roll(xg, half, 1))
    return xg * cos_t + partner * sin_t


def _ada_kernel(c_ref, w_ref, b_ref, o_ref):
    o_ref[...] = jnp.dot(_silu(c_ref[...]), w_ref[...], preferred_element_type=F32, precision=HIGHEST) + b_ref[...]


def _ada_call(c_all, ada_w, ada_b):
    nl, ns, d, d3 = ada_w.shape
    n = c_all.shape[0]
    tn = 1024
    return pl.pallas_call(
        _ada_kernel,
        grid=(nl * ns, d3 // tn),
        in_specs=[pl.BlockSpec((n, d), lambda i, j: (0, 0)),
                  pl.BlockSpec((None, None, d, tn), lambda i, j: (i // ns, i % ns, 0, j)),
                  pl.BlockSpec((None, 1, tn), lambda i, j: (i, 0, j))],
        out_specs=pl.BlockSpec((None, n, tn), lambda i, j: (i, 0, j)),
        out_shape=jax.ShapeDtypeStruct((nl * ns, n, d3), F32),
        compiler_params=_cparams(2), name="ada_mod",
    )(c_all, ada_w, ada_b.reshape(nl * ns, 1, d3))


def _mixin_prompt_kernel(x_ref, sh_ref, sc_ref, g_ref, w_ref, wdt_ref, dtb_ref, cw_ref, cb_ref,
                         q_ref, k_ref, v_ref, z_ref, xs_ref, bc_ref, dt_ref, km_ref, cst_ref,
                         buf_ref, oc_ref, os_ref, *, tm):
    t = pl.program_id(1)
    lane = lax.broadcasted_iota(jnp.int32, (1, LANES), 1)
    j = lane & (HEAD_DIM - 1)
    half = ROT_DIM // 2
    inv = jnp.exp((lane & (half - 1)).astype(F32) * (-2.0 * math.log(ROPE_THETA) / ROT_DIM))
    rot = jnp.where(j < ROT_DIM, 1.0, 0.0)
    sgn = jnp.where(j < half, -1.0, rot)
    low = j < half

    @pl.when(t == 0)
    def _():
        buf_ref[0:SUBLANES, :] = jnp.zeros((SUBLANES, SSD_CONV_DIM), F32)
        ang = lax.broadcasted_iota(jnp.int32, (tm, LANES), 0).astype(F32) * inv
        oc_ref[...] = jnp.cos(ang)
        os_ref[...] = jnp.sin(ang)

    h = _norm_mod(x_ref[...], g_ref[...], sh_ref[...], sc_ref[...])
    proj = jnp.dot(h.astype(BF16), w_ref[...], preferred_element_type=F32)

    base = (t * tm).astype(F32) * inv
    cb, sb = jnp.cos(base), jnp.sin(base)
    oc, osn = oc_ref[...], os_ref[...]
    cos_t = (cb * rot) * oc - (sb * rot) * osn + (1.0 - rot)
    sin_t = (sb * sgn) * oc + (cb * sgn) * osn
    for gi in range(ATT_WIDTH // LANES):
        lo = gi * LANES
        q_ref[:, lo:lo + LANES] = _rope_group(proj[:, lo:lo + LANES], cos_t, sin_t, low)
        k_ref[:, lo:lo + LANES] = _rope_group(proj[:, ATT_WIDTH + lo:ATT_WIDTH + lo + LANES], cos_t, sin_t, low)
    v_ref[...] = proj[:, 2 * ATT_WIDTH:3 * ATT_WIDTH]
    z_ref[...] = proj[:, 3 * ATT_WIDTH:3 * ATT_WIDTH + SSD_INNER]
    for blk in range(tm // MOBA_BLOCK):
        kb = k_ref[blk * MOBA_BLOCK:(blk + 1) * MOBA_BLOCK, :]
        km_ref[blk] = jnp.sum(kb, axis=0, keepdims=True) * (1.0 / MOBA_BLOCK)

    raw = proj[:, 3 * ATT_WIDTH + SSD_INNER:MAIN_COLS]
    buf_ref[SUBLANES:SUBLANES + tm, :] = raw
    conv = cb_ref[...] + cw_ref[SSD_CONV - 1:SSD_CONV, :] * raw
    for i in range(1, SSD_CONV):
        conv = conv + cw_ref[SSD_CONV - 1 - i:SSD_CONV - i, :] * buf_ref[SUBLANES - i:SUBLANES - i + tm, :]
    act = _silu(conv)
    xs_ref[...] = act[:, :SSD_INNER]
    bc_ref[...] = act[:, SSD_INNER:]
    cst_ref[...] = raw[tm - SUBLANES:tm, :]
    buf_ref[0:SUBLANES, :] = raw[tm - SUBLANES:tm, :]

    dt_raw = _dot_nt(wdt_ref[...], h, precision=HIGHEST)
    dt_ref[...] = _softplus(dt_raw + dtb_ref[...])


def _mixin_prompt_call(x, shift, scale, g, w_main, wdt_t, dtb_col, conv_w, conv_b):
    b, t, d = x.shape
    tm = min(ROW_TILE, t)
    nt = t // tm
    nbt = tm // MOBA_BLOCK
    row_spec = lambda c: pl.BlockSpec((None, tm, c), lambda bi, ti: (bi, ti, 0))
    mod_spec = pl.BlockSpec((None, 1, d), lambda bi, ti: (bi, 0, 0))
    full = lambda a: pl.BlockSpec(a.shape, lambda bi, ti: (0,) * a.ndim)
    outs = pl.pallas_call(
        functools.partial(_mixin_prompt_kernel, tm=tm),
        grid=(b, nt),
        in_specs=[row_spec(d), mod_spec, mod_spec, full(g), full(w_main), full(wdt_t), full(dtb_col),
                  full(conv_w), full(conv_b)],
        out_specs=[row_spec(ATT_WIDTH), row_spec(ATT_WIDTH), row_spec(ATT_WIDTH), row_spec(SSD_INNER),
                   row_spec(SSD_INNER), row_spec(SSD_CONV_DIM - SSD_INNER),
                   pl.BlockSpec((None, SSD_HEADS, tm), lambda bi, ti: (bi, 0, ti)),
                   pl.BlockSpec((nbt, 1, ATT_WIDTH), lambda bi, ti: (bi * nt + ti, 0, 0)),
                   pl.BlockSpec((None, SUBLANES, SSD_CONV_DIM), lambda bi, ti: (bi, 0, 0))],
        out_shape=[jax.ShapeDtypeStruct((b, t, ATT_WIDTH), F32)] * 3
        + [jax.ShapeDtypeStruct((b, t, SSD_INNER), F32)] * 2
        + [jax.ShapeDtypeStruct((b, t, SSD_CONV_DIM - SSD_INNER), F32),
           jax.ShapeDtypeStruct((b, SSD_HEADS, t), F32),
           jax.ShapeDtypeStruct((b * nt * nbt, 1, ATT_WIDTH), F32),
           jax.ShapeDtypeStruct((b, SUBLANES, SSD_CONV_DIM), F32)],
        scratch_shapes=[pltpu.VMEM((tm + SUBLANES, SSD_CONV_DIM), F32), pltpu.VMEM((tm, LANES), F32),
                        pltpu.VMEM((tm, LANES), F32)],
        compiler_params=_cparams(2), name="mixin_prompt",
    )(x, shift, scale, g, w_main, wdt_t, dtb_col, conv_w, conv_b)
    q, k, v, z, xs, bc, dt_t, kmean, cst = outs
    return q, k, v, z, xs, bc, dt_t, kmean.reshape(b, t // MOBA_BLOCK, ATT_WIDTH), cst


def _topk_first(gate, ids, n_valid, axis):
    picks = []
    for r in range(MOBA_TOPK):
        mx = jnp.max(gate, axis=axis, keepdims=True)
        idx = jnp.min(jnp.where(gate == mx, ids, 1e9), axis=axis, keepdims=True)
        idx = jnp.where(r < n_valid, idx, -1.0)
        picks.append(idx)
        gate = jnp.where(ids == idx, NEG_INF, gate)
    return picks


def _attn_prompt_kernel(q_ref, k_ref, v_ref, km_ref, o_ref, ka_scr, vt_scr, acc_scr, *, tq, t):
    i = pl.program_id(2)
    kv = ATTN_KV_BLOCKS * MOBA_BLOCK
    nb = km_ref.shape[0]

    @pl.when(i == 0)
    def _():
        blk_lane = lax.broadcasted_iota(jnp.int32, (MOBA_BLOCK, LANES), 1)
        sub = lax.broadcasted_iota(jnp.int32, (2 * SUBLANES, MOBA_BLOCK), 0)
        ones_rows = jnp.where(sub == 0, 1.0, 0.0).astype(BF16)

        def prep(c, _):
            r0 = pl.multiple_of(c * MOBA_BLOCK, MOBA_BLOCK)
            ka_scr[pl.ds(r0, MOBA_BLOCK), 0:LANES] = k_ref[pl.ds(r0, MOBA_BLOCK), :].astype(BF16)
            ka_scr[pl.ds(r0, MOBA_BLOCK), LANES:2 * LANES] = jnp.where(blk_lane == c, 1.0, 0.0).astype(BF16)
            vt = v_ref[pl.ds(r0, MOBA_BLOCK), :].T.astype(BF16)
            for hh in range(2):
                vt_scr[hh, 0:HEAD_DIM, pl.ds(r0, MOBA_BLOCK)] = vt[hh * HEAD_DIM:(hh + 1) * HEAD_DIM]
                vt_scr[hh, HEAD_DIM:VT_ROWS, pl.ds(r0, MOBA_BLOCK)] = ones_rows
            return 0

        lax.fori_loop(0, t // MOBA_BLOCK, prep, 0)

    n = 2 * tq
    nch = ATTN_TILES_PER_STEP
    km = jnp.concatenate([km_ref[...], jnp.zeros((LANES - nb, LANES), F32)], axis=0)
    lane = lax.broadcasted_iota(jnp.int32, (tq, LANES), 1)
    ids = lax.broadcasted_iota(jnp.int32, (LANES, n), 0).astype(F32)
    key = lax.broadcasted_iota(jnp.int32, (tq, n), 0)
    qry = lax.broadcasted_iota(jnp.int32, (tq, n), 1) & (tq - 1)
    qas, ms = [], []
    for c in range(nch):
        jown = i * nch + c
        q = q_ref[c * tq:(c + 1) * tq, :]
        qcat = jnp.concatenate([jnp.where(lane < HEAD_DIM, q, 0.0), jnp.where(lane >= HEAD_DIM, q, 0.0)], axis=0)
        gate = _dot_nt(km, qcat, precision=HIGHEST)
        gate = jnp.where(ids < jown.astype(F32), gate, NEG_INF)
        p0, p1, p2 = _topk_first(gate, ids, jown, 0)
        bias = jnp.where((ids == p0) | (ids == p1) | (ids == p2), 0.0, MASKED).T
        qs = qcat * (LOG2_E / math.sqrt(HEAD_DIM))
        qas.append(jnp.concatenate([qs, bias], axis=1).astype(BF16))

        r0 = pl.multiple_of(jown * tq, tq)
        st = jnp.where(key <= qry, _dot_nt(ka_scr[pl.ds(r0, tq), 0:LANES], qs.astype(BF16)), NEG_INF)
        m = jnp.max(st, axis=0, keepdims=True)
        p = jnp.exp2(st - m)
        pb = p.astype(BF16)
        for hh in range(2):
            acc_scr[c, hh] = jnp.dot(vt_scr[hh, :, pl.ds(r0, tq)], pb[:, hh * tq:(hh + 1) * tq],
                                     preferred_element_type=F32)
        ms.append(m)

    def body(j, ms):
        c0 = pl.multiple_of(j * kv, kv)
        sts = [_dot_nt(ka_scr[pl.ds(c0, kv), :], qas[c]) for c in range(nch)]
        out = []
        for c in range(nch):
            m_new = jnp.maximum(ms[c], jnp.max(sts[c], axis=0, keepdims=True))
            pb = jnp.exp2(sts[c] - m_new).astype(BF16)
            alpha = jnp.exp2(ms[c] - m_new)
            for hh in range(2):
                acc_scr[c, hh] = alpha[:, hh * tq:(hh + 1) * tq] * acc_scr[c, hh] + jnp.dot(
                    vt_scr[hh, :, pl.ds(c0, kv)], pb[:, hh * tq:(hh + 1) * tq], preferred_element_type=F32)
            out.append(m_new)
        return tuple(out)

    last_own = i * nch + nch - 1
    lax.fori_loop(0, (last_own + ATTN_KV_BLOCKS - 1) // ATTN_KV_BLOCKS, body, tuple(ms))
    for c in range(nch):
        outs = []
        for hh in range(2):
            acc = acc_scr[c, hh]
            outs.append(acc[0:HEAD_DIM, :] / acc[HEAD_DIM:HEAD_DIM + 1, :])
        o_ref[c * tq:(c + 1) * tq, :] = jnp.concatenate(outs, axis=0).T


def _attn_prompt_call(q, k, v, kmean):
    b, t, _ = q.shape
    tq = min(ATTN_Q_TILE, t)
    nb = kmean.shape[1]
    nch = ATTN_TILES_PER_STEP
    assert tq == MOBA_BLOCK and tq & (tq - 1) == 0 and nb <= LANES and t % (ATTN_KV_BLOCKS * MOBA_BLOCK) == 0
    assert t % (nch * tq) == 0
    npair = ATT_WIDTH // LANES
    return pl.pallas_call(
        functools.partial(_attn_prompt_kernel, tq=tq, t=t),
        scratch_shapes=[pltpu.VMEM((t, 2 * LANES), BF16), pltpu.VMEM((2, VT_ROWS, t), BF16),
                        pltpu.VMEM((nch, 2, VT_ROWS, tq), F32)],
        grid=(b, npair, t // (nch * tq)),
        in_specs=[pl.BlockSpec((None, nch * tq, LANES), lambda bi, hp, i: (bi, i, hp)),
                  pl.BlockSpec((None, t, LANES), lambda bi, hp, i: (bi, 0, hp)),
                  pl.BlockSpec((None, t, LANES), lambda bi, hp, i: (bi, 0, hp)),
                  pl.BlockSpec((None, nb, LANES), lambda bi, hp, i: (bi, 0, hp))],
        out_specs=pl.BlockSpec((None, nch * tq, LANES), lambda bi, hp, i: (bi, i, hp)),
        out_shape=jax.ShapeDtypeStruct((b, t, ATT_WIDTH), F32),
        compiler_params=_cparams(3), name="moba_prompt",
    )(q, k, v, kmean)


def _gated_norm(y, xs, z, dsk, nw):
    yg = (y + dsk * xs) * _silu(z)
    return _rms(yg, nw)


def _ssd_prompt_kernel(xs_ref, bc_ref, dt_ref, z_ref, alog_ref, dsk_ref, nw_ref, y_ref, st_ref, h_scr, *, L, n_sub):
    c = pl.program_id(1)

    @pl.when(c == 0)
    def _():
        h_scr[...] = jnp.zeros(h_scr.shape, F32)

    a = -jnp.exp(alog_ref[...])
    for sub in range(n_sub):
        _ssd_chunk(sub * L, L, a, xs_ref, bc_ref, dt_ref, z_ref, dsk_ref, nw_ref, y_ref, h_scr)
    st_ref[...] = h_scr[...]


def _ssd_chunk(r0, L, a, xs_ref, bc_ref, dt_ref, z_ref, dsk_ref, nw_ref, y_ref, h_scr):
    dt = dt_ref[:, r0:r0 + L]
    lane = lax.broadcasted_iota(jnp.int32, dt.shape, 1)
    acum = dt * a
    s = 1
    while s < L:
        acum = acum + jnp.where(lane >= s, pltpu.roll(acum, s, 1), 0.0)
        s *= 2
    nh = SSD_HEADS
    e_rows = jnp.exp(acum)
    w_rows = jnp.exp(acum[:, L - 1:L] - acum) * dt
    rows = jnp.concatenate([acum, e_rows, w_rows, jnp.zeros((L - 3 * nh, L), F32)], axis=0)
    cols = rows.T

    xs = xs_ref[r0:r0 + L, :]
    bc = bc_ref[r0:r0 + L, :]
    ri = lax.broadcasted_iota(jnp.int32, (L, L), 0)
    ci = lax.broadcasted_iota(jnp.int32, (L, L), 1)
    causal = ri >= ci
    low = lax.broadcasted_iota(jnp.int32, (L, LANES), 1) < SSD_HEADDIM
    low_rows = lax.broadcasted_iota(jnp.int32, (LANES, SSD_STATE), 0) < SSD_HEADDIM
    gs = SSD_GROUPS * SSD_STATE
    heads_per_group = SSD_HEADS // SSD_GROUPS
    ys = []
    for g in range(SSD_GROUPS):
        bg = bc[:, g * SSD_STATE:(g + 1) * SSD_STATE].astype(BF16)
        cg = bc[:, gs + g * SSD_STATE:gs + (g + 1) * SSD_STATE].astype(BF16)
        cbt = _dot_nt(cg, bg)
        for pr in range(heads_per_group // 2):
            hp = g * (heads_per_group // 2) + pr
            xp = xs[:, hp * LANES:(hp + 1) * LANES]
            xpb = xp.astype(BF16)
            hst = h_scr[hp * LANES:(hp + 1) * LANES, :]
            inter = _dot_nt(cg, hst.astype(BF16))
            intra, ecols, wcols, cds = [], [], [], []
            for hh in range(2):
                hd = 2 * hp + hh
                col = cols[:, hd:hd + 1]
                row = acum[hd:hd + 1, :]
                decay = jnp.exp(jnp.where(causal, col - row, NEG_INF))
                m = cbt * decay * dt[hd:hd + 1, :]
                intra.append(jnp.dot(m.astype(BF16), xpb, preferred_element_type=F32))
                last = acum[hd:hd + 1, L - 1:L]
                ecols.append(cols[:, nh + hd:nh + hd + 1])
                wcols.append(cols[:, 2 * nh + hd:2 * nh + hd + 1])
                cds.append(jnp.exp(last))
            ys.append(jnp.where(low, intra[0], intra[1]) + inter * jnp.where(low, ecols[0], ecols[1]))
            xw = (xp * jnp.where(low, wcols[0], wcols[1])).astype(BF16)
            upd = _dot_tn(xw, bg)
            h_scr[hp * LANES:(hp + 1) * LANES, :] = hst * jnp.where(low_rows, cds[0], cds[1]) + upd
    y = jnp.concatenate(ys, axis=1)
    y_ref[r0:r0 + L, :] = _gated_norm(y, xs, z_ref[r0:r0 + L, :], dsk_ref[...], nw_ref[...])


def _ssd_prompt_call(xs, bc, dt_t, z, alog_col, dsk_row, nw_row):
    b, t, _ = xs.shape
    L = SSD_CHUNK
    n_sub = SSD_CHUNKS_PER_STEP
    rows = L * n_sub
    assert t % rows == 0
    row_spec = lambda c: pl.BlockSpec((None, rows, c), lambda bi, ci: (bi, ci, 0))
    full = lambda a: pl.BlockSpec(a.shape, lambda bi, ci: (0,) * a.ndim)
    return pl.pallas_call(
        functools.partial(_ssd_prompt_kernel, L=L, n_sub=n_sub),
        grid=(b, t // rows),
        in_specs=[row_spec(SSD_INNER), row_spec(SSD_CONV_DIM - SSD_INNER),
                  pl.BlockSpec((None, SSD_HEADS, rows), lambda bi, ci: (bi, 0, ci)),
                  row_spec(SSD_INNER), full(alog_col), full(dsk_row), full(nw_row)],
        out_specs=[row_spec(SSD_INNER),
                   pl.BlockSpec((None, SSD_INNER, SSD_STATE), lambda bi, ci: (bi, 0, 0))],
        out_shape=[jax.ShapeDtypeStruct((b, t, SSD_INNER), F32),
                   jax.ShapeDtypeStruct((b, SSD_INNER, SSD_STATE), F32)],
        scratch_shapes=[pltpu.VMEM((SSD_INNER, SSD_STATE), F32)],
        compiler_params=_cparams(2), name="ssd_prompt",
    )(xs, bc, dt_t, z, alog_col, dsk_row, nw_row)


def _mixout_kernel(att_ref, y_ref, x_ref, gt_ref, gp_ref, wa_ref, wy_ref, o_ref):
    out = (jnp.dot(att_ref[...].astype(BF16), wa_ref[...], preferred_element_type=F32)
           + jnp.dot(y_ref[...].astype(BF16), wy_ref[...], preferred_element_type=F32))
    o_ref[...] = x_ref[...] + gt_ref[...] * _rms(out, gp_ref[...])


def _mixout_call(att, y, x, gate, gp, wa, wy, *, per_seq_mod):
    b, t, d = x.shape
    tm = min(ROW_TILE, t)
    row_spec = lambda c: pl.BlockSpec((None, tm, c), lambda bi, ti: (bi, ti, 0))
    mod_spec = (pl.BlockSpec((None, 1, d), lambda bi, ti: (bi, 0, 0)) if per_seq_mod
                else pl.BlockSpec((None, tm, d), lambda bi, ti: (bi, ti, 0)))
    full = lambda a: pl.BlockSpec(a.shape, lambda bi, ti: (0,) * a.ndim)
    return pl.pallas_call(
        _mixout_kernel,
        grid=(b, t // tm),
        in_specs=[row_spec(ATT_WIDTH), row_spec(SSD_INNER), row_spec(d), mod_spec, full(gp), full(wa), full(wy)],
        out_specs=row_spec(d),
        out_shape=jax.ShapeDtypeStruct((b, t, d), F32),
        compiler_params=_cparams(2), name="mix_out",
    )(att, y, x, gate, gp, wa, wy)


def _ffn_kernel(*refs, tm, decode):
    if decode:
        (x_ref, sh_ref, sc_ref, gt_ref, g_ref, gp_ref, up_ref, cw_ref, cb_ref, dn_ref, p0_ref, p1_ref,
         o_ref, u_ref, ubuf0_ref, ubuf1_ref, acc_ref, hb_ref) = refs
    else:
        (x_ref, sh_ref, sc_ref, gt_ref, g_ref, gp_ref, up_ref, cw_ref, cb_ref, dn_ref,
         o_ref, cst_ref, carry_ref, ubuf0_ref, ubuf1_ref, acc_ref, hb_ref) = refs
        t = pl.program_id(1)

        @pl.when(t == 0)
        def _():
            carry_ref[...] = jnp.zeros(carry_ref.shape, F32)

    x = x_ref[...]
    hb_ref[...] = _norm_mod(x, g_ref[...], sh_ref[...], sc_ref[...]).astype(BF16)
    acc_ref[...] = jnp.zeros(acc_ref.shape, F32)
    fc = FFN_CHUNK
    n_chunks = D_FF // fc
    ubufs = (ubuf0_ref, ubuf1_ref)

    def col_offset(j, half):
        return pl.multiple_of(half * D_FF + j * fc, LANES)

    def up_project(j, ubuf):
        hb = hb_ref[...]
        for half in range(2):
            off, lo = col_offset(j, half), half * fc
            raw = jnp.dot(hb, up_ref[:, pl.ds(off, fc)], preferred_element_type=F32)
            ubuf[SUBLANES:SUBLANES + tm, lo:lo + fc] = raw
            if decode:
                u_ref[:, pl.ds(off, fc)] = raw
            else:
                ubuf[0:SUBLANES, lo:lo + fc] = carry_ref[:, pl.ds(off, fc)]
                carry_ref[:, pl.ds(off, fc)] = raw[tm - SUBLANES:tm, :]

    def conv_gate_down(j, ubuf):
        halves = []
        for half in range(2):
            off, lo = col_offset(j, half), half * fc
            cw = cw_ref[:, pl.ds(off, fc)]
            conv = cb_ref[:, pl.ds(off, fc)] + cw[FFN_CONV - 1:FFN_CONV, :] * ubuf[SUBLANES:SUBLANES + tm, lo:lo + fc]
            if decode:
                conv = conv + cw[1:2, :] * p1_ref[:, pl.ds(off, fc)] + cw[0:1, :] * p0_ref[:, pl.ds(off, fc)]
            else:
                for i in range(1, FFN_CONV):
                    conv = conv + (cw[FFN_CONV - 1 - i:FFN_CONV - i, :]
                                   * ubuf[SUBLANES - i:SUBLANES - i + tm, lo:lo + fc])
            halves.append(conv)
        act = _silu(halves[0]) * halves[1]
        r0 = pl.multiple_of(j * fc, fc)
        acc_ref[...] += jnp.dot(act.astype(BF16), dn_ref[pl.ds(r0, fc), :], preferred_element_type=F32)

    up_project(0, ubufs[0])

    def pair(i, _):
        up_project(2 * i + 1, ubufs[1])
        conv_gate_down(2 * i, ubufs[0])
        up_project(2 * i + 2, ubufs[0])
        conv_gate_down(2 * i + 1, ubufs[1])
        return 0

    assert n_chunks % 2 == 1
    lax.fori_loop(0, n_chunks // 2, pair, 0)
    conv_gate_down(n_chunks - 1, ubufs[0])
    o_ref[...] = x + gt_ref[...] * _rms(acc_ref[...], gp_ref[...])
    if not decode:
        cst_ref[...] = carry_ref[...]


def _ffn_call(x, shift, scale, gate, g, gp, up_w, conv_w, conv_b, dn_w, prev=None):
    b, t, d = x.shape
    decode = prev is not None
    tm = min(ROW_TILE, t)
    row_spec = lambda c: pl.BlockSpec((None, tm, c), lambda bi, ti: (bi, ti, 0))
    mod_spec = (pl.BlockSpec((None, tm, d), lambda bi, ti: (bi, ti, 0)) if decode
                else pl.BlockSpec((None, 1, d), lambda bi, ti: (bi, 0, 0)))
    full = lambda a: pl.BlockSpec(a.shape, lambda bi, ti: (0,) * a.ndim)
    in_specs = [row_spec(d), mod_spec, mod_spec, mod_spec, full(g), full(gp), full(up_w), full(conv_w),
                full(conv_b), full(dn_w)]
    args = [x, shift, scale, gate, g, gp, up_w, conv_w, conv_b, dn_w]
    scratch = [pltpu.VMEM((tm + SUBLANES, 2 * FFN_CHUNK), F32)] * 2 + [pltpu.VMEM((tm, d), F32), pltpu.VMEM((tm, d), BF16)]
    if decode:
        assert b == 1 and t == tm
        in_specs += [pl.BlockSpec((tm, 2 * D_FF), lambda bi, ti: (ti, 0))] * 2
        args += list(prev)
        out_specs = [row_spec(d), pl.BlockSpec((tm, 2 * D_FF), lambda bi, ti: (ti, 0))]
        out_shape = [jax.ShapeDtypeStruct((b, t, d), F32), jax.ShapeDtypeStruct((t, 2 * D_FF), F32)]
    else:
        out_specs = [row_spec(d), pl.BlockSpec((None, SUBLANES, 2 * D_FF), lambda bi, ti: (bi, 0, 0))]
        out_shape = [jax.ShapeDtypeStruct((b, t, d), F32), jax.ShapeDtypeStruct((b, SUBLANES, 2 * D_FF), F32)]
        scratch = [pltpu.VMEM((SUBLANES, 2 * D_FF), F32)] + scratch
    return pl.pallas_call(
        functools.partial(_ffn_kernel, tm=tm, decode=decode),
        grid=(b, t // tm), in_specs=in_specs, out_specs=out_specs, out_shape=out_shape,
        scratch_shapes=scratch, compiler_params=_cparams(2),
        name="conv_ffn_sample" if decode else "conv_ffn_prompt",
    )(*args)


def _pool_kernel(*refs, tm, decode, pos0):
    if decode:
        (x_ref, sh_ref, sc_ref, gt_ref, g_ref, gp_ref, pw_ref, pb_ref, ps_ref, prev_ref, o_ref, h_ref) = refs
    else:
        (x_ref, sh_ref, sc_ref, gt_ref, g_ref, gp_ref, pw_ref, pb_ref, ps_ref, o_ref, last_ref, buf_ref) = refs
        t = pl.program_id(1)

        @pl.when(t == 0)
        def _():
            buf_ref[0:POOL_MAX, :] = jnp.zeros((POOL_MAX, D_MODEL), F32)

    x = x_ref[...]
    h = _norm_mod(x, g_ref[...], sh_ref[...], sc_ref[...])
    if decode:
        h_ref[...] = h
    else:
        buf_ref[POOL_MAX:POOL_MAX + tm, :] = h
        row = lax.broadcasted_iota(jnp.int32, (tm, 1), 0)
        pos = pos0 + t * tm + row
    ys = []
    if not decode:
        level = buf_ref[...]
        doubled, width, first_lane = [], 1, 0
        for gi, win in enumerate(POOL_WINDOWS):
            lo = gi * POOL_GROUP_DIM
            level = level[:, lo - first_lane:]
            first_lane = lo
            while width < win:
                level = level + pltpu.roll(level, width, 0)
                width *= 2
            doubled.append(level[POOL_MAX:POOL_MAX + tm, 0:POOL_GROUP_DIM])
    for gi, win in enumerate(POOL_WINDOWS):
        lo = gi * POOL_GROUP_DIM
        hg = h[:, lo:lo + POOL_GROUP_DIM]
        if decode:
            wsum = hg
            for i in range(1, win):
                wsum = wsum + prev_ref[POOL_MAX - 1 - i, :, lo:lo + POOL_GROUP_DIM]
        else:
            wsum = doubled[gi]
        if decode:
            cnt = float(min(pos0 + 1, win))
        else:
            cnt = jnp.minimum(pos + 1, win).astype(F32)
        pooled = wsum / cnt - hg
        ys.append(jnp.dot(pooled.astype(BF16), pw_ref[gi], preferred_element_type=F32))
    y = (jnp.concatenate(ys, axis=1) + pb_ref[...]) * ps_ref[...]
    o_ref[...] = x + gt_ref[...] * _rms(y, gp_ref[...])
    if not decode:
        last_ref[...] = h[tm - POOL_MAX:tm, :]
        buf_ref[0:POOL_MAX, :] = h[tm - POOL_MAX:tm, :]


def _pool_call(x, shift, scale, gate, g, gp, pw, pb, ps, prev_t=None, pos0=0):
    b, t, d = x.shape
    decode = prev_t is not None
    tm = min(ROW_TILE, t)
    row_spec = lambda c: pl.BlockSpec((None, tm, c), lambda bi, ti: (bi, ti, 0))
    mod_spec = (pl.BlockSpec((None, tm, d), lambda bi, ti: (bi, ti, 0)) if decode
                else pl.BlockSpec((None, 1, d), lambda bi, ti: (bi, 0, 0)))
    full = lambda a: pl.BlockSpec(a.shape, lambda bi, ti: (0,) * a.ndim)
    in_specs = [row_spec(d), mod_spec, mod_spec, mod_spec, full(g), full(gp), full(pw), full(pb), full(ps)]
    args = [x, shift, scale, gate, g, gp, pw, pb, ps]
    if decode:
        assert b == 1 and t == tm
        in_specs.append(full(prev_t))
        args.append(prev_t)
        out_specs = [row_spec(d), row_spec(d)]
        out_shape = [jax.ShapeDtypeStruct((b, t, d), F32)] * 2
        scratch = []
    else:
        out_specs = [row_spec(d), pl.BlockSpec((None, POOL_MAX, d), lambda bi, ti: (bi, 0, 0))]
        out_shape = [jax.ShapeDtypeStruct((b, t, d), F32), jax.ShapeDtypeStruct((b, POOL_MAX, d), F32)]
        scratch = [pltpu.VMEM((tm + POOL_MAX, d), F32)]
    return pl.pallas_call(
        functools.partial(_pool_kernel, tm=tm, decode=decode, pos0=pos0),
        grid=(b, t // tm), in_specs=in_specs, out_specs=out_specs, out_shape=out_shape,
        scratch_shapes=scratch, compiler_params=_cparams(2),
        name="pool_sample" if decode else "pool_prompt",
    )(*args)


def _mixin_sample_kernel(x_ref, sh_ref, sc_ref, g_ref, w_ref, wdtx_ref, dtbx_ref, cw_ref, cb_ref, prev_ref,
                         q_ref, k_ref, v_ref, z_ref, xs_ref, bc_ref, dtx_ref, raw_ref, *, pos0):
    h = _norm_mod(x_ref[...], g_ref[...], sh_ref[...], sc_ref[...])
    proj = jnp.dot(h.astype(BF16), w_ref[...], preferred_element_type=F32)
    n = h.shape[0]
    cos_t, sin_t, low = _rope_tables(jnp.float32(pos0), (n, LANES))
    for gi in range(ATT_WIDTH // LANES):
        lo = gi * LANES
        q_ref[:, lo:lo + LANES] = _rope_group(proj[:, lo:lo + LANES], cos_t, sin_t, low)
        k_ref[:, lo:lo + LANES] = _rope_group(proj[:, ATT_WIDTH + lo:ATT_WIDTH + lo + LANES], cos_t, sin_t, low)
    v_ref[...] = proj[:, 2 * ATT_WIDTH:3 * ATT_WIDTH]
    z_ref[...] = proj[:, 3 * ATT_WIDTH:3 * ATT_WIDTH + SSD_INNER]
    raw = proj[:, 3 * ATT_WIDTH + SSD_INNER:MAIN_COLS]
    raw_ref[...] = raw
    conv = cb_ref[...] + cw_ref[SSD_CONV - 1:SSD_CONV, :] * raw
    for i in range(SSD_CONV - 1):
        conv = conv + cw_ref[i:i + 1, :] * prev_ref[i]
    act = _silu(conv)
    xs_ref[...] = act[:, :SSD_INNER]
    bc_ref[...] = act[:, SSD_INNER:]
    dtx_ref[...] = _softplus(jnp.dot(h, wdtx_ref[...], preferred_element_type=F32, precision=HIGHEST) + dtbx_ref[...])


def _mixin_sample_call(x, shift, scale, g, w_main, wdt_x, dtb_x, conv_w, conv_b, prev_t, pos0):
    n, d = x.shape
    full = lambda a: pl.BlockSpec(a.shape, lambda i: (0,) * a.ndim)
    args = [x, shift, scale, g, w_main, wdt_x, dtb_x, conv_w, conv_b, prev_t]
    widths = [ATT_WIDTH] * 3 + [SSD_INNER] * 2 + [SSD_CONV_DIM - SSD_INNER, SSD_INNER, SSD_CONV_DIM]
    return pl.pallas_call(
        functools.partial(_mixin_sample_kernel, pos0=pos0),
        grid=(1,),
        in_specs=[full(a) for a in args],
        out_specs=[pl.BlockSpec((n, c), lambda i: (0, 0)) for c in widths],
        out_shape=[jax.ShapeDtypeStruct((n, c), F32) for c in widths],
        compiler_params=_cparams(1), name="mixin_sample",
    )(*args)


_PAGES_PER_BLOCK = MOBA_BLOCK // PAGE_SIZE
_SEL_PAGES = MOBA_TOPK * _PAGES_PER_BLOCK
SAMPLE_HEADS_PER_STEP = 8


def _token_minor(cache):
    return jnp.transpose(cache, (0, 1, 3, 4, 2))


def _past_kmean_kernel(pt_ref, *refs):
    pages, o_ref = refs[:PAGES_PER_STEP], refs[PAGES_PER_STEP]
    step = pl.program_id(1)
    blocks_per_step = PAGES_PER_STEP // _PAGES_PER_BLOCK

    @pl.when(step == 0)
    def _():
        o_ref[...] = jnp.zeros(o_ref.shape, F32)

    acc = o_ref[...]
    lane = lax.broadcasted_iota(jnp.int32, acc.shape, 2)
    for blk in range(blocks_per_step):
        tot = pages[blk * _PAGES_PER_BLOCK][...]
        for r in range(1, _PAGES_PER_BLOCK):
            tot = tot + pages[blk * _PAGES_PER_BLOCK + r][...]
        mean = jnp.sum(tot, axis=-1, keepdims=True) * (1.0 / MOBA_BLOCK)
        acc = jnp.where(lane == step * blocks_per_step + blk, mean, acc)
    o_ref[...] = acc


def _past_kmean_call(pt_flat, cache_kt, n_seq, n_pages):
    steps = n_pages // PAGES_PER_STEP
    nb = n_pages // _PAGES_PER_BLOCK

    def page_map(b, s, pt, *, r):
        return (0, pt[b * n_pages + s * PAGES_PER_STEP + r], 0, 0, 0)

    grid_spec = pltpu.PrefetchScalarGridSpec(
        num_scalar_prefetch=1, grid=(n_seq, steps),
        in_specs=[pl.BlockSpec((None, None, ATT_HEADS, HEAD_DIM, PAGE_SIZE), functools.partial(page_map, r=r))
                  for r in range(PAGES_PER_STEP)],
        out_specs=pl.BlockSpec((None, ATT_HEADS, HEAD_DIM, nb), lambda b, s, pt: (b, 0, 0, 0)))
    return pl.pallas_call(
        _past_kmean_kernel, grid_spec=grid_spec,
        out_shape=jax.ShapeDtypeStruct((n_seq, ATT_HEADS, HEAD_DIM, nb), F32),
        compiler_params=_cparams(2), name="past_kmean",
    )(pt_flat, *([cache_kt] * PAGES_PER_STEP))


def _sample_select_kernel(q_ref, km_ref, sel_ref, *, jown):
    gate = jnp.sum(km_ref[...] * q_ref[...], axis=1, keepdims=True)
    ids = lax.broadcasted_iota(jnp.int32, gate.shape, 2).astype(F32)
    gate = jnp.where(ids < float(jown), gate, NEG_INF)
    picks = _topk_first(gate, ids, jown, 2)
    for r in range(MOBA_TOPK):
        sel_ref[r] = picks[r].astype(jnp.int32)


def _sample_select_call(q4, kmean, jown):
    n, _, _, nb = kmean.shape
    return pl.pallas_call(
        functools.partial(_sample_select_kernel, jown=jown),
        grid=(n,),
        in_specs=[pl.BlockSpec((None, ATT_HEADS, HEAD_DIM, 1), lambda b: (b, 0, 0, 0)),
                  pl.BlockSpec((None, ATT_HEADS, HEAD_DIM, nb), lambda b: (b, 0, 0, 0))],
        out_specs=pl.BlockSpec((None, MOBA_TOPK, ATT_HEADS, 1, 1), lambda b: (b, 0, 0, 0, 0)),
        out_shape=jax.ShapeDtypeStruct((n, MOBA_TOPK, ATT_HEADS, 1, 1), jnp.int32),
        compiler_params=_cparams(1), name="sample_select",
    )(q4, kmean)


def _sample_attn_kernel(pt_ref, sel_ref, q_ref, kn_ref, vn_ref, *refs):
    n_pg = SAMPLE_HEADS_PER_STEP * _SEL_PAGES
    kp, vp, o_ref = refs[:n_pg], refs[n_pg:2 * n_pg], refs[2 * n_pg]
    b, grp = pl.program_id(0), pl.program_id(1)
    scale = 1.0 / math.sqrt(HEAD_DIM)
    for hh in range(SAMPLE_HEADS_PER_STEP):
        h = grp * SAMPLE_HEADS_PER_STEP + hh
        q = q_ref[hh] * scale
        s_self = jnp.sum(q * kn_ref[hh], axis=0, keepdims=True)
        scores = []
        for r in range(MOBA_TOPK):
            valid = sel_ref[(b * MOBA_TOPK + r) * ATT_HEADS + h] >= 0
            for pg in range(_PAGES_PER_BLOCK):
                s = jnp.sum(kp[hh * _SEL_PAGES + r * _PAGES_PER_BLOCK + pg][...] * q, axis=0, keepdims=True)
                scores.append(jnp.where(valid, s, NEG_INF))
        m = s_self
        for s in scores:
            m = jnp.maximum(m, jnp.max(s, axis=1, keepdims=True))
        p_self = jnp.exp(s_self - m)
        l = p_self
        acc = p_self * vn_ref[hh]
        for idx, s in enumerate(scores):
            p = jnp.exp(s - m)
            l = l + jnp.sum(p, axis=1, keepdims=True)
            acc = acc + jnp.sum(vp[hh * _SEL_PAGES + idx][...] * p, axis=1, keepdims=True)
        o_ref[hh] = acc / l


def _sample_attn_call(pages_flat, sel_flat, q4, k4, v4, cache_kt, cache_vt):
    n = q4.shape[0]
    hps = SAMPLE_HEADS_PER_STEP
    head_spec = pl.BlockSpec((None, hps, HEAD_DIM, 1), lambda b, g, pt, sel: (b, g, 0, 0))

    def page_map(b, g, pages, sel, *, hh, r, pg):
        h = g * hps + hh
        return (0, pages[((b * MOBA_TOPK + r) * ATT_HEADS + h) * _PAGES_PER_BLOCK + pg], h, 0, 0)

    page_specs = [pl.BlockSpec((None, None, None, HEAD_DIM, PAGE_SIZE), functools.partial(page_map, hh=hh, r=r, pg=pg))
                  for hh in range(hps) for r in range(MOBA_TOPK) for pg in range(_PAGES_PER_BLOCK)]
    grid_spec = pltpu.PrefetchScalarGridSpec(
        num_scalar_prefetch=2, grid=(n, ATT_HEADS // hps),
        in_specs=[head_spec, head_spec, head_spec] + page_specs + page_specs,
        out_specs=head_spec)
    n_pg = hps * _SEL_PAGES
    return pl.pallas_call(
        _sample_attn_kernel, grid_spec=grid_spec,
        out_shape=jax.ShapeDtypeStruct((n, ATT_HEADS, HEAD_DIM, 1), F32),
        compiler_params=_cparams(2), name="sample_attn",
    )(pages_flat, sel_flat, q4, k4, v4, *([cache_kt] * n_pg), *([cache_vt] * n_pg))


def _ssd_sample_kernel(xs_ref, bc_ref, dtx_ref, z_ref, st_ref, alogx_ref, dsk_ref, nw_ref, y_ref, sto_ref):
    xs, dtx = xs_ref[...], dtx_ref[...]
    decay = jnp.exp(dtx * (-jnp.exp(alogx_ref[...])))
    xdt = xs * dtx
    bc = bc_ref[...]
    rows_per_group = SSD_INNER // SSD_GROUPS
    gs = SSD_GROUPS * SSD_STATE
    eye = (lax.broadcasted_iota(jnp.int32, (rows_per_group, rows_per_group), 0)
           == lax.broadcasted_iota(jnp.int32, (rows_per_group, rows_per_group), 1))

    def to_col(v):
        return jnp.sum(jnp.where(eye, jnp.broadcast_to(v, eye.shape), 0.0), axis=1, keepdims=True)

    def to_row(v):
        return jnp.sum(jnp.where(eye, jnp.broadcast_to(v, eye.shape), 0.0), axis=0, keepdims=True)

    ys = []
    for g in range(SSD_GROUPS):
        lo = g * rows_per_group
        b_row = bc[:, g * SSD_STATE:(g + 1) * SSD_STATE]
        c_row = bc[:, gs + g * SSD_STATE:gs + (g + 1) * SSD_STATE]
        hg = st_ref[lo:lo + rows_per_group, :]
        hnew = to_col(decay[:, lo:lo + rows_per_group]) * hg + to_col(xdt[:, lo:lo + rows_per_group]) * b_row
        sto_ref[lo:lo + rows_per_group, :] = hnew
        ys.append(to_row(jnp.sum(hnew * c_row, axis=1, keepdims=True)))
    y = jnp.concatenate(ys, axis=1)
    y_ref[...] = _gated_norm(y, xs, z_ref[...], dsk_ref[...], nw_ref[...])


def _ssd_sample_call(xs, bc, dtx, z, state, alog_x, dsk_row, nw_row):
    n = xs.shape[0]
    r3 = lambda a: a.reshape(n, 1, a.shape[-1])
    row_spec = lambda c: pl.BlockSpec((None, 1, c), lambda b: (b, 0, 0))
    st_spec = pl.BlockSpec((None, SSD_INNER, SSD_STATE), lambda b: (b, 0, 0))
    full = lambda a: pl.BlockSpec(a.shape, lambda b: (0,) * a.ndim)
    y, st = pl.pallas_call(
        _ssd_sample_kernel,
        grid=(n,),
        in_specs=[row_spec(SSD_INNER), row_spec(SSD_CONV_DIM - SSD_INNER), row_spec(SSD_INNER), row_spec(SSD_INNER),
                  st_spec, full(alog_x), full(dsk_row), full(nw_row)],
        out_specs=[row_spec(SSD_INNER), st_spec],
        out_shape=[jax.ShapeDtypeStruct((n, 1, SSD_INNER), F32), jax.ShapeDtypeStruct((n, SSD_INNER, SSD_STATE), F32)],
        compiler_params=_cparams(1), name="ssd_sample",
    )(r3(xs), r3(bc), r3(dtx), r3(z), state, alog_x, dsk_row, nw_row)
    return y.reshape(n, SSD_INNER), st


def kernel(x_prompt, x_sample, cache_k, cache_v, state_ssm, state_ssd_conv, state_pool, state_ffn_conv, page_table,
           c_prompt, c_sample, ada_w, ada_b, norm_pre, norm_post, mix_in_w, mix_out_w, ssd_conv_w, ssd_conv_b,
           ssd_dt_bias, ssd_a_log, ssd_d, ssd_norm_w, pool_w, pool_b, pool_scale, ffn_up_w, ffn_conv_w,
           ffn_conv_b, ffn_down_w):
    bp, t, d = x_prompt.shape
    ns = x_sample.shape[0]
    n_pages = page_table.shape[1]
    past_len = n_pages * cache_k.shape[2]
    assert x_sample.shape[1] == 1 and cache_k.shape[2] == PAGE_SIZE and past_len % MOBA_BLOCK == 0
    assert t % ROW_TILE == 0 or t < ROW_TILE

    w_main = mix_in_w[0, :, :MAIN_COLS].astype(BF16)
    w_dt = mix_in_w[0, :, MAIN_COLS:]
    wdt_t = w_dt.T
    wdt_x = jnp.repeat(w_dt, SSD_HEADDIM, axis=1)
    dtb_col = ssd_dt_bias[0].reshape(SSD_HEADS, 1)
    dtb_x = jnp.repeat(ssd_dt_bias[0], SSD_HEADDIM).reshape(1, SSD_INNER)
    alog_col = ssd_a_log[0].reshape(SSD_HEADS, 1)
    alog_x = jnp.repeat(ssd_a_log[0], SSD_HEADDIM).reshape(1, SSD_INNER)
    dsk_row = jnp.repeat(ssd_d[0], SSD_HEADDIM).reshape(1, SSD_INNER)
    nw_row = ssd_norm_w[0].reshape(1, SSD_INNER)
    w_att = mix_out_w[0, :ATT_WIDTH].astype(BF16)
    w_ssd = mix_out_w[0, ATT_WIDTH:].astype(BF16)
    conv_w0 = ssd_conv_w[0]
    conv_b0 = ssd_conv_b[0].reshape(1, SSD_CONV_DIM)
    up_w = ffn_up_w.astype(BF16)
    dn_w = ffn_down_w.astype(BF16)
    pw = pool_w[0].astype(BF16)
    pb = pool_b[0].reshape(1, d)
    ps = pool_scale[0].reshape(1, d)

    mods = _ada_call(jnp.concatenate([c_prompt, c_sample], axis=0), ada_w, ada_b)

    def mod(layer, slot, sample):
        m = mods[layer * 2 + slot]
        m = m[bp:] if sample else m[:bp]
        parts = (m[:, :d], m[:, d:2 * d], m[:, 2 * d:])
        return tuple(p[None] for p in parts) if sample else tuple(p[:, None, :] for p in parts)

    npre = lambda layer, slot: norm_pre[layer, slot].reshape(1, d)
    npost = lambda layer, slot: norm_post[layer, slot].reshape(1, d)

    def ffn(x, layer, sample, prev=None):
        sh, sc, gt = mod(layer, 1, sample)
        return _ffn_call(x, sh, sc, gt, npre(layer, 1), npost(layer, 1), up_w[layer], ffn_conv_w[layer],
                         ffn_conv_b[layer].reshape(1, 2 * D_FF), dn_w[layer], prev=prev)

    sh, sc, gt = mod(0, 0, False)
    q, k, v, z, xs, bc, dt_t, kmean, cst = _mixin_prompt_call(
        x_prompt, sh, sc, npre(0, 0), w_main, wdt_t, dtb_col, conv_w0, conv_b0)
    att = _attn_prompt_call(q, k, v, kmean)
    y, ssm_p = _ssd_prompt_call(xs, bc, dt_t, z, alog_col, dsk_row, nw_row)
    xp = _mixout_call(att, y, x_prompt, gt, npost(0, 0), w_att, w_ssd, per_seq_mod=True)
    xp, fst0 = ffn(xp, 0, False)
    sh, sc, gt = mod(1, 0, False)
    xp, plast = _pool_call(xp, sh, sc, gt, npre(1, 0), npost(1, 0), pw, pb, ps)
    xp, fst1 = ffn(xp, 1, False)

    k_p = k.reshape(1, bp, t, ATT_HEADS, HEAD_DIM)
    v_p = v.reshape(1, bp, t, ATT_HEADS, HEAD_DIM)
    ssm_p = ssm_p.reshape(1, bp, SSD_HEADS, SSD_HEADDIM, SSD_STATE)
    sconv_p = cst[None, :, SUBLANES - (SSD_CONV - 1):, :]
    pool_p = plast[None, :, 1:, :]
    fconv_p = jnp.stack([fst0, fst1])[:, :, SUBLANES - (FFN_CONV - 1):, :]

    xs0 = x_sample.reshape(ns, d)
    sh, sc, gt = mod(0, 0, True)
    conv_prev_t = jnp.transpose(state_ssd_conv[0], (1, 0, 2))
    qs, ks, vs, zs, xss, bcs, dtx, raw = _mixin_sample_call(
        xs0, sh[0], sc[0], npre(0, 0), w_main, wdt_x, dtb_x, conv_w0, conv_b0, conv_prev_t, past_len)
    pt_flat = page_table.reshape(-1)
    cache_kt, cache_vt = _token_minor(cache_k), _token_minor(cache_v)
    kmean_s = _past_kmean_call(pt_flat, cache_kt, ns, n_pages)
    jown = past_len // MOBA_BLOCK
    as_cols = lambda a: a.reshape(ns, ATT_HEADS, HEAD_DIM, 1)
    sel = _sample_select_call(as_cols(qs), kmean_s, jown)
    sel3 = sel.reshape(ns, MOBA_TOPK, ATT_HEADS)
    page_pos = jnp.maximum(sel3, 0)[..., None] * _PAGES_PER_BLOCK + jnp.arange(_PAGES_PER_BLOCK, dtype=jnp.int32)
    sel_pages = jnp.take_along_axis(page_table, page_pos.reshape(ns, -1), axis=1)
    att_s = _sample_attn_call(sel_pages.reshape(-1), sel3.reshape(-1), as_cols(qs), as_cols(ks), as_cols(vs),
                              cache_kt, cache_vt).reshape(ns, ATT_WIDTH)
    y_s, ssm_s = _ssd_sample_call(xss, bcs, dtx, zs, state_ssm[0].reshape(ns, SSD_INNER, SSD_STATE),
                                  alog_x, dsk_row, nw_row)
    xq = _mixout_call(att_s[None], y_s[None], xs0[None], gt, npost(0, 0), w_att, w_ssd, per_seq_mod=False)
    xq, u0 = ffn(xq, 0, True, prev=(state_ffn_conv[0, :, 0], state_ffn_conv[0, :, 1]))
    sh, sc, gt = mod(1, 0, True)
    pool_prev_t = jnp.transpose(state_pool[0], (1, 0, 2))
    xq, h_pool = _pool_call(xq, sh, sc, gt, npre(1, 0), npost(1, 0), pw, pb, ps, prev_t=pool_prev_t, pos0=past_len)
    xq, u1 = ffn(xq, 1, True, prev=(state_ffn_conv[1, :, 0], state_ffn_conv[1, :, 1]))

    y_sample = xq.reshape(ns, 1, d)
    k_s = ks.reshape(1, ns, 1, ATT_HEADS, HEAD_DIM)
    v_s = vs.reshape(1, ns, 1, ATT_HEADS, HEAD_DIM)
    ssm_s = ssm_s.reshape(1, ns, SSD_HEADS, SSD_HEADDIM, SSD_STATE)
    sconv_s = jnp.concatenate([state_ssd_conv[0, :, 1:], raw[:, None, :]], axis=1)[None]
    pool_s = jnp.concatenate([state_pool[0, :, 1:], h_pool[0][:, None, :]], axis=1)[None]
    fconv_s = jnp.stack([jnp.stack([state_ffn_conv[0, :, 1], u0], axis=1),
                         jnp.stack([state_ffn_conv[1, :, 1], u1], axis=1)])
    return (xp, y_sample, k_p, v_p, ssm_p, sconv_p, pool_p, fconv_p, k_s, v_s, ssm_s, sconv_s, pool_s, fconv_s)
```

```python
import functools
import math

import jax
import jax.numpy as jnp
from jax import lax
from jax.experimental import pallas as pl
from jax.experimental.pallas import tpu as pltpu

F32 = jnp.float32
BF16 = jnp.bfloat16
HIGHEST = lax.Precision.HIGHEST

D_MODEL = 1024
ATT_HEADS = 8
HEAD_DIM = 64
ATT_WIDTH = ATT_HEADS * HEAD_DIM
ROT_DIM = HEAD_DIM // 4
ROPE_THETA = 500000.0
MOBA_BLOCK = 256
MOBA_TOPK = 3
SSD_HEADS = 8
SSD_HEADDIM = 64
SSD_INNER = SSD_HEADS * SSD_HEADDIM
SSD_GROUPS = 2
SSD_STATE = 128
SSD_CONV = 4
SSD_CONV_DIM = SSD_INNER + 2 * SSD_GROUPS * SSD_STATE
POOL_WINDOWS = (2, 4, 8, 16)
POOL_GROUP_DIM = D_MODEL // len(POOL_WINDOWS)
POOL_MAX = 16
D_FF = 2816
FFN_CONV = 3
EPS = 1e-6
PAGE_SIZE = 128
MAIN_COLS = 3 * ATT_WIDTH + SSD_INNER + SSD_CONV_DIM

LANES = 128
SUBLANES = 8
VMEM_LIMIT_BYTES = 56 * 1024 * 1024

ROW_TILE = 512
FFN_ROW_TILE = 512
ATTN_Q_TILE = 256
SSD_CHUNK = 128
SSD_CHUNKS_PER_STEP = 4
FFN_CHUNK = 256
PAGES_PER_STEP = 32

ATTN_KV_BLOCKS = 4
ATTN_TILES_PER_STEP = 4

NEG_INF = float("-inf")
MASKED = -1e30
LOG2_E = 1.4426950408889634
VT_ROWS = HEAD_DIM + 2 * SUBLANES


def _cparams(n_axes, flags=None):
    return pltpu.CompilerParams(dimension_semantics=("arbitrary",) * n_axes,
                                vmem_limit_bytes=VMEM_LIMIT_BYTES, flags=flags)


def _silu(x):
    return x / (1.0 + jnp.exp(-x))


def _softplus(x):
    return jnp.maximum(x, 0.0) + jnp.log1p(jnp.exp(-jnp.abs(x)))


def _rms(x, g):
    return x * lax.rsqrt(jnp.mean(x * x, axis=-1, keepdims=True) + EPS) * g


def _norm_mod(x, g, shift, scale):
    return _rms(x, g) * (1.0 + scale) + shift


def _dot_nt(a, b, precision=None):
    return lax.dot_general(a, b, (((1,), (1,)), ((), ())), preferred_element_type=F32, precision=precision)


def _dot_tn(a, b):
    return lax.dot_general(a, b, (((0,), (0,)), ((), ())), preferred_element_type=F32)


def _rope_tables(pos, shape):
    lane = lax.broadcasted_iota(jnp.int32, shape, 1)
    j = lane & (HEAD_DIM - 1)
    half = ROT_DIM // 2
    inv = jnp.exp((lane & (half - 1)).astype(F32) * (-2.0 * math.log(ROPE_THETA) / ROT_DIM))
    ang = pos * inv
    c, s = jnp.cos(ang), jnp.sin(ang)
    cos_t = jnp.where(j < ROT_DIM, c, 1.0)
    sin_t = jnp.where(j < half, -s, jnp.where(j < ROT_DIM, s, 0.0))
    return cos_t, sin_t, j < half


def _rope_group(xg, cos_t, sin_t, low):
    half = ROT_DIM // 2
    partner = jnp.where(low, pltpu.roll(xg, LANES - half, 1), pltpu.roll(xg, half, 1))
    return xg * cos_t + partner * sin_t


def _ada_kernel(c_ref, w_ref, b_ref, o_ref):
    o_ref[...] = jnp.dot(_silu(c_ref[...]), w_ref[...], preferred_element_type=F32, precision=HIGHEST) + b_ref[...]


def _ada_call(c_all, ada_w, ada_b):
    nl, ns, d, d3 = ada_w.shape
    n = c_all.shape[0]
    tn = 1024
    return pl.pallas_call(
        _ada_kernel,
        grid=(nl * ns, d3 // tn),
        in_specs=[pl.BlockSpec((n, d), lambda i, j: (0, 0)),
                  pl.BlockSpec((None, None, d, tn), lambda i, j: (i // ns, i % ns, 0, j)),
                  pl.BlockSpec((None, 1, tn), lambda i, j: (i, 0, j))],
        out_specs=pl.BlockSpec((None, n, tn), lambda i, j: (i, 0, j)),
        out_shape=jax.ShapeDtypeStruct((nl * ns, n, d3), F32),
        compiler_params=_cparams(2), name="ada_mod",
    )(c_all, ada_w, ada_b.reshape(nl * ns, 1, d3))


def _mixin_prompt_kernel(x_ref, sh_ref, sc_ref, g_ref, w_ref, wdt_ref, dtb_ref, cw_ref, cb_ref,
                         q_ref, k_ref, v_ref, z_ref, xs_ref, bc_ref, dt_ref, km_ref, cst_ref,
                         buf_ref, oc_ref, os_ref, *, tm):
    t = pl.program_id(1)
    lane = lax.broadcasted_iota(jnp.int32, (1, LANES), 1)
    j = lane & (HEAD_DIM - 1)
    half = ROT_DIM // 2
    inv = jnp.exp((lane & (half - 1)).astype(F32) * (-2.0 * math.log(ROPE_THETA) / ROT_DIM))
    rot = jnp.where(j < ROT_DIM, 1.0, 0.0)
    sgn = jnp.where(j < half, -1.0, rot)
    low = j < half

    @pl.when(t == 0)
    def _():
        buf_ref[0:SUBLANES, :] = jnp.zeros((SUBLANES, SSD_CONV_DIM), F32)
        ang = lax.broadcasted_iota(jnp.int32, (tm, LANES), 0).astype(F32) * inv
        oc_ref[...] = jnp.cos(ang)
        os_ref[...] = jnp.sin(ang)

    h = _norm_mod(x_ref[...], g_ref[...], sh_ref[...], sc_ref[...])
    proj = jnp.dot(h.astype(BF16), w_ref[...], preferred_element_type=F32)

    base = (t * tm).astype(F32) * inv
    cb, sb = jnp.cos(base), jnp.sin(base)
    oc, osn = oc_ref[...], os_ref[...]
    cos_t = (cb * rot) * oc - (sb * rot) * osn + (1.0 - rot)
    sin_t = (sb * sgn) * oc + (cb * sgn) * osn
    for gi in range(ATT_WIDTH // LANES):
        lo = gi * LANES
        q_ref[:, lo:lo + LANES] = _rope_group(proj[:, lo:lo + LANES], cos_t, sin_t, low)
        k_ref[:, lo:lo + LANES] = _rope_group(proj[:, ATT_WIDTH + lo:ATT_WIDTH + lo + LANES], cos_t, sin_t, low)
    v_ref[...] = proj[:, 2 * ATT_WIDTH:3 * ATT_WIDTH]
    z_ref[...] = proj[:, 3 * ATT_WIDTH:3 * ATT_WIDTH + SSD_INNER]
    for blk in range(tm // MOBA_BLOCK):
        kb = k_ref[blk * MOBA_BLOCK:(blk + 1) * MOBA_BLOCK, :]
        km_ref[blk] = jnp.sum(kb, axis=0, keepdims=True) * (1.0 / MOBA_BLOCK)

    raw = proj[:, 3 * ATT_WIDTH + SSD_INNER:MAIN_COLS]
    buf_ref[SUBLANES:SUBLANES + tm, :] = raw
    conv = cb_ref[...] + cw_ref[SSD_CONV - 1:SSD_CONV, :] * raw
    for i in range(1, SSD_CONV):
        conv = conv + cw_ref[SSD_CONV - 1 - i:SSD_CONV - i, :] * buf_ref[SUBLANES - i:SUBLANES - i + tm, :]
    act = _silu(conv)
    xs_ref[...] = act[:, :SSD_INNER]
    bc_ref[...] = act[:, SSD_INNER:].astype(BF16)
    cst_ref[...] = raw[tm - SUBLANES:tm, :]
    buf_ref[0:SUBLANES, :] = raw[tm - SUBLANES:tm, :]

    dt_raw = _dot_nt(wdt_ref[...], h, precision=HIGHEST)
    dt_ref[...] = _softplus(dt_raw + dtb_ref[...])


def _mixin_prompt_call(x, shift, scale, g, w_main, wdt_t, dtb_col, conv_w, conv_b):
    b, t, d = x.shape
    tm = min(ROW_TILE, t)
    nt = t // tm
    nbt = tm // MOBA_BLOCK
    row_spec = lambda c: pl.BlockSpec((None, tm, c), lambda bi, ti: (bi, ti, 0))
    mod_spec = pl.BlockSpec((None, 1, d), lambda bi, ti: (bi, 0, 0))
    full = lambda a: pl.BlockSpec(a.shape, lambda bi, ti: (0,) * a.ndim)
    outs = pl.pallas_call(
        functools.partial(_mixin_prompt_kernel, tm=tm),
        grid=(b, nt),
        in_specs=[row_spec(d), mod_spec, mod_spec, full(g), full(w_main), full(wdt_t), full(dtb_col),
                  full(conv_w), full(conv_b)],
        out_specs=[row_spec(ATT_WIDTH), row_spec(ATT_WIDTH), row_spec(ATT_WIDTH), row_spec(SSD_INNER),
                   row_spec(SSD_INNER), row_spec(SSD_CONV_DIM - SSD_INNER),
                   pl.BlockSpec((None, SSD_HEADS, tm), lambda bi, ti: (bi, 0, ti)),
                   pl.BlockSpec((nbt, 1, ATT_WIDTH), lambda bi, ti: (bi * nt + ti, 0, 0)),
                   pl.BlockSpec((None, SUBLANES, SSD_CONV_DIM), lambda bi, ti: (bi, 0, 0))],
        out_shape=[jax.ShapeDtypeStruct((b, t, ATT_WIDTH), F32)] * 3
        + [jax.ShapeDtypeStruct((b, t, SSD_INNER), F32)] * 2
        + [jax.ShapeDtypeStruct((b, t, SSD_CONV_DIM - SSD_INNER), BF16),
           jax.ShapeDtypeStruct((b, SSD_HEADS, t), F32),
           jax.ShapeDtypeStruct((b * nt * nbt, 1, ATT_WIDTH), F32),
           jax.ShapeDtypeStruct((b, SUBLANES, SSD_CONV_DIM), F32)],
        scratch_shapes=[pltpu.VMEM((tm + SUBLANES, SSD_CONV_DIM), F32), pltpu.VMEM((tm, LANES), F32),
                        pltpu.VMEM((tm, LANES), F32)],
        compiler_params=_cparams(2), name="mixin_prompt",
    )(x, shift, scale, g, w_main, wdt_t, dtb_col, conv_w, conv_b)
    q, k, v, z, xs, bc, dt_t, kmean, cst = outs
    return q, k, v, z, xs, bc, dt_t, kmean.reshape(b, t // MOBA_BLOCK, ATT_WIDTH), cst


def _topk_first(gate, ids, n_valid, axis):
    picks = []
    for r in range(MOBA_TOPK):
        mx = jnp.max(gate, axis=axis, keepdims=True)
        idx = jnp.min(jnp.where(gate == mx, ids, 1e9), axis=axis, keepdims=True)
        idx = jnp.where(r < n_valid, idx, -1.0)
        picks.append(idx)
        gate = jnp.where(ids == idx, NEG_INF, gate)
    return picks


def _attn_prompt_kernel(q_ref, k_ref, v_ref, km_ref, o_ref, ka_scr, vt_scr, acc_scr, *, tq, t):
    i = pl.program_id(2)
    kv = ATTN_KV_BLOCKS * MOBA_BLOCK
    nb = km_ref.shape[0]

    @pl.when(i == 0)
    def _():
        blk_lane = lax.broadcasted_iota(jnp.int32, (MOBA_BLOCK, LANES), 1)
        sub = lax.broadcasted_iota(jnp.int32, (2 * SUBLANES, MOBA_BLOCK), 0)
        ones_rows = jnp.where(sub == 0, 1.0, 0.0).astype(BF16)

        def prep(c, _):
            r0 = pl.multiple_of(c * MOBA_BLOCK, MOBA_BLOCK)
            ka_scr[pl.ds(r0, MOBA_BLOCK), 0:LANES] = k_ref[pl.ds(r0, MOBA_BLOCK), :].astype(BF16)
            ka_scr[pl.ds(r0, MOBA_BLOCK), LANES:2 * LANES] = jnp.where(blk_lane == c, 1.0, 0.0).astype(BF16)
            vt = v_ref[pl.ds(r0, MOBA_BLOCK), :].T.astype(BF16)
            for hh in range(2):
                vt_scr[hh, 0:HEAD_DIM, pl.ds(r0, MOBA_BLOCK)] = vt[hh * HEAD_DIM:(hh + 1) * HEAD_DIM]
                vt_scr[hh, HEAD_DIM:VT_ROWS, pl.ds(r0, MOBA_BLOCK)] = ones_rows
            return 0

        lax.fori_loop(0, t // MOBA_BLOCK, prep, 0)

    n = 2 * tq
    nch = ATTN_TILES_PER_STEP
    km = km_ref[...]
    lane = lax.broadcasted_iota(jnp.int32, (tq, LANES), 1)
    ids = lax.broadcasted_iota(jnp.int32, (nb, n), 0).astype(F32)
    ids_all = lax.broadcasted_iota(jnp.int32, (LANES, n), 0).astype(F32)
    key = lax.broadcasted_iota(jnp.int32, (tq, n), 0)
    qry = lax.broadcasted_iota(jnp.int32, (tq, n), 1) & (tq - 1)
    qas, ms = [], []
    for c in range(nch):
        jown = i * nch + c
        q = q_ref[c * tq:(c + 1) * tq, :]
        qcat = jnp.concatenate([jnp.where(lane < HEAD_DIM, q, 0.0), jnp.where(lane >= HEAD_DIM, q, 0.0)], axis=0)
        gate = _dot_nt(km, qcat, precision=HIGHEST)
        gate = jnp.where(ids < jown.astype(F32), gate, NEG_INF)
        p0, p1, p2 = _topk_first(gate, ids, jown, 0)
        bias = jnp.where((ids_all == p0) | (ids_all == p1) | (ids_all == p2), 0.0, MASKED).T
        qs = qcat * (LOG2_E / math.sqrt(HEAD_DIM))
        qas.append(jnp.concatenate([qs, bias], axis=1).astype(BF16))

        r0 = pl.multiple_of(jown * tq, tq)
        st = jnp.where(key <= qry, _dot_nt(ka_scr[pl.ds(r0, tq), 0:LANES], qs.astype(BF16)), NEG_INF)
        m = jnp.max(st, axis=0, keepdims=True)
        p = jnp.exp2(st - m)
        pb = p.astype(BF16)
        for hh in range(2):
            acc_scr[c, hh] = jnp.dot(vt_scr[hh, :, pl.ds(r0, tq)], pb[:, hh * tq:(hh + 1) * tq],
                                     preferred_element_type=F32)
        ms.append(m)

    def body(j, ms):
        c0 = pl.multiple_of(j * kv, kv)
        sts = [_dot_nt(ka_scr[pl.ds(c0, kv), :], qas[c]) for c in range(nch)]
        out = []
        for c in range(nch):
            m_new = jnp.maximum(ms[c], jnp.max(sts[c], axis=0, keepdims=True))
            pb = jnp.exp2(sts[c] - m_new).astype(BF16)
            alpha = jnp.exp2(ms[c] - m_new)
            for hh in range(2):
                acc_scr[c, hh] = alpha[:, hh * tq:(hh + 1) * tq] * acc_scr[c, hh] + jnp.dot(
                    vt_scr[hh, :, pl.ds(c0, kv)], pb[:, hh * tq:(hh + 1) * tq], preferred_element_type=F32)
            out.append(m_new)
        return tuple(out)

    last_own = i * nch + nch - 1
    lax.fori_loop(0, (last_own + ATTN_KV_BLOCKS - 1) // ATTN_KV_BLOCKS, body, tuple(ms))
    for c in range(nch):
        outs = []
        for hh in range(2):
            acc = acc_scr[c, hh]
            outs.append(acc[0:HEAD_DIM, :] / acc[HEAD_DIM:HEAD_DIM + 1, :])
        o_ref[c * tq:(c + 1) * tq, :] = jnp.concatenate(outs, axis=0).T.astype(BF16)


def _attn_prompt_call(q, k, v, kmean):
    b, t, _ = q.shape
    tq = min(ATTN_Q_TILE, t)
    nb = kmean.shape[1]
    nch = ATTN_TILES_PER_STEP
    assert tq == MOBA_BLOCK and tq & (tq - 1) == 0 and nb <= LANES and t % (ATTN_KV_BLOCKS * MOBA_BLOCK) == 0
    assert t % (nch * tq) == 0
    npair = ATT_WIDTH // LANES
    return pl.pallas_call(
        functools.partial(_attn_prompt_kernel, tq=tq, t=t),
        scratch_shapes=[pltpu.VMEM((t, 2 * LANES), BF16), pltpu.VMEM((2, VT_ROWS, t), BF16),
                        pltpu.VMEM((nch, 2, VT_ROWS, tq), F32)],
        grid=(b, npair, t // (nch * tq)),
        in_specs=[pl.BlockSpec((None, nch * tq, LANES), lambda bi, hp, i: (bi, i, hp)),
                  pl.BlockSpec((None, t, LANES), lambda bi, hp, i: (bi, 0, hp)),
                  pl.BlockSpec((None, t, LANES), lambda bi, hp, i: (bi, 0, hp)),
                  pl.BlockSpec((None, nb, LANES), lambda bi, hp, i: (bi, 0, hp))],
        out_specs=pl.BlockSpec((None, nch * tq, LANES), lambda bi, hp, i: (bi, i, hp)),
        out_shape=jax.ShapeDtypeStruct((b, t, ATT_WIDTH), BF16),
        compiler_params=_cparams(3), name="moba_prompt",
    )(q, k, v, kmean)


def _gated_norm(y, xs, z, dsk, nw):
    yg = (y + dsk * xs) * _silu(z)
    return _rms(yg, nw)


def _ssd_prompt_kernel(xs_ref, bc_ref, dt_ref, z_ref, alog_ref, dsk_ref, nw_ref, y_ref, st_ref, h_scr, *, L, n_sub):
    c = pl.program_id(1)

    @pl.when(c == 0)
    def _():
        h_scr[...] = jnp.zeros(h_scr.shape, F32)

    a = -jnp.exp(alog_ref[...])
    for sub in range(n_sub):
        _ssd_chunk(sub * L, L, a, xs_ref, bc_ref, dt_ref, z_ref, dsk_ref, nw_ref, y_ref, h_scr)
    st_ref[...] = h_scr[...]


def _ssd_chunk(r0, L, a, xs_ref, bc_ref, dt_ref, z_ref, dsk_ref, nw_ref, y_ref, h_scr):
    dt = dt_ref[:, r0:r0 + L]
    lane = lax.broadcasted_iota(jnp.int32, dt.shape, 1)
    acum = dt * a
    s = 1
    while s < L:
        acum = acum + jnp.where(lane >= s, pltpu.roll(acum, s, 1), 0.0)
        s *= 2
    nh = SSD_HEADS
    e_rows = jnp.exp(acum)
    w_rows = jnp.exp(acum[:, L - 1:L] - acum) * dt
    rows = jnp.concatenate([acum, e_rows, w_rows, jnp.zeros((L - 3 * nh, L), F32)], axis=0)
    cols = rows.T

    xs = xs_ref[r0:r0 + L, :]
    bc = bc_ref[r0:r0 + L, :]
    ri = lax.broadcasted_iota(jnp.int32, (L, L), 0)
    ci = lax.broadcasted_iota(jnp.int32, (L, L), 1)
    causal = ri >= ci
    low = lax.broadcasted_iota(jnp.int32, (L, LANES), 1) < SSD_HEADDIM
    low_rows = lax.broadcasted_iota(jnp.int32, (LANES, SSD_STATE), 0) < SSD_HEADDIM
    gs = SSD_GROUPS * SSD_STATE
    heads_per_group = SSD_HEADS // SSD_GROUPS
    ys = []
    for g in range(SSD_GROUPS):
        bg = bc[:, g * SSD_STATE:(g + 1) * SSD_STATE].astype(BF16)
        cg = bc[:, gs + g * SSD_STATE:gs + (g + 1) * SSD_STATE].astype(BF16)
        cbt = _dot_nt(cg, bg)
        for pr in range(heads_per_group // 2):
            hp = g * (heads_per_group // 2) + pr
            xp = xs[:, hp * LANES:(hp + 1) * LANES]
            xpb = xp.astype(BF16)
            hst = h_scr[hp * LANES:(hp + 1) * LANES, :]
            inter = _dot_nt(cg, hst.astype(BF16))
            intra, ecols, wcols, cds = [], [], [], []
            for hh in range(2):
                hd = 2 * hp + hh
                col = cols[:, hd:hd + 1]
                row = acum[hd:hd + 1, :]
                decay = jnp.exp(jnp.where(causal, col - row, NEG_INF))
                m = cbt * decay * dt[hd:hd + 1, :]
                intra.append(jnp.dot(m.astype(BF16), xpb, preferred_element_type=F32))
                last = acum[hd:hd + 1, L - 1:L]
                ecols.append(cols[:, nh + hd:nh + hd + 1])
                wcols.append(cols[:, 2 * nh + hd:2 * nh + hd + 1])
                cds.append(jnp.exp(last))
            ys.append(jnp.where(low, intra[0], intra[1]) + inter * jnp.where(low, ecols[0], ecols[1]))
            xw = (xp * jnp.where(low, wcols[0], wcols[1])).astype(BF16)
            upd = _dot_tn(xw, bg)
            h_scr[hp * LANES:(hp + 1) * LANES, :] = hst * jnp.where(low_rows, cds[0], cds[1]) + upd
    y = jnp.concatenate(ys, axis=1)
    y_ref[r0:r0 + L, :] = _gated_norm(y, xs, z_ref[r0:r0 + L, :], dsk_ref[...], nw_ref[...]).astype(BF16)


def _ssd_prompt_call(xs, bc, dt_t, z, alog_col, dsk_row, nw_row):
    b, t, _ = xs.shape
    L = SSD_CHUNK
    n_sub = SSD_CHUNKS_PER_STEP
    rows = L * n_sub
    assert t % rows == 0
    row_spec = lambda c: pl.BlockSpec((None, rows, c), lambda bi, ci: (bi, ci, 0))
    full = lambda a: pl.BlockSpec(a.shape, lambda bi, ci: (0,) * a.ndim)
    return pl.pallas_call(
        functools.partial(_ssd_prompt_kernel, L=L, n_sub=n_sub),
        grid=(b, t // rows),
        in_specs=[row_spec(SSD_INNER), row_spec(SSD_CONV_DIM - SSD_INNER),
                  pl.BlockSpec((None, SSD_HEADS, rows), lambda bi, ci: (bi, 0, ci)),
                  row_spec(SSD_INNER), full(alog_col), full(dsk_row), full(nw_row)],
        out_specs=[row_spec(SSD_INNER),
                   pl.BlockSpec((None, SSD_INNER, SSD_STATE), lambda bi, ci: (bi, 0, 0))],
        out_shape=[jax.ShapeDtypeStruct((b, t, SSD_INNER), BF16),
                   jax.ShapeDtypeStruct((b, SSD_INNER, SSD_STATE), F32)],
        scratch_shapes=[pltpu.VMEM((SSD_INNER, SSD_STATE), F32)],
        compiler_params=_cparams(2), name="ssd_prompt",
    )(xs, bc, dt_t, z, alog_col, dsk_row, nw_row)


def _ffn_kernel(*refs, tm, decode, mix_out):
    if mix_out:
        (att_ref, y_ref, gt0_ref, gp0_ref, wa_ref, wy_ref), refs = refs[:6], refs[6:]
    if decode:
        (x_ref, sh_ref, sc_ref, gt_ref, g_ref, gp_ref, up_ref, cw_ref, cb_ref, dn_ref, p0_ref, p1_ref,
         o_ref, u_ref, ubuf0_ref, ubuf1_ref, acc_ref, hb_ref) = refs
    else:
        (x_ref, sh_ref, sc_ref, gt_ref, g_ref, gp_ref, up_ref, cw_ref, cb_ref, dn_ref,
         o_ref, cst_ref, carry_ref, ubuf0_ref, ubuf1_ref, acc_ref, hb_ref) = refs
        t = pl.program_id(1)

        @pl.when(t == 0)
        def _():
            carry_ref[...] = jnp.zeros(carry_ref.shape, F32)

    x = x_ref[...]
    if mix_out:
        mixed = (jnp.dot(att_ref[...].astype(BF16), wa_ref[...], preferred_element_type=F32)
                 + jnp.dot(y_ref[...].astype(BF16), wy_ref[...], preferred_element_type=F32))
        x = x + gt0_ref[...] * _rms(mixed, gp0_ref[...])
        o_ref[...] = x
    hb_ref[...] = _norm_mod(x, g_ref[...], sh_ref[...], sc_ref[...]).astype(BF16)
    acc_ref[...] = jnp.zeros(acc_ref.shape, F32)
    fc = FFN_CHUNK
    n_chunks = D_FF // fc
    ubufs = (ubuf0_ref, ubuf1_ref)

    def col_offset(j, half):
        return pl.multiple_of(half * D_FF + j * fc, LANES)

    def up_project(j, ubuf):
        hb = hb_ref[...]
        for half in range(2):
            off, lo = col_offset(j, half), half * fc
            raw = jnp.dot(hb, up_ref[:, pl.ds(off, fc)], preferred_element_type=F32)
            ubuf[SUBLANES:SUBLANES + tm, lo:lo + fc] = raw
            if decode:
                u_ref[:, pl.ds(off, fc)] = raw
            else:
                ubuf[0:SUBLANES, lo:lo + fc] = carry_ref[:, pl.ds(off, fc)]
                carry_ref[:, pl.ds(off, fc)] = raw[tm - SUBLANES:tm, :]

    def conv_gate_down(j, ubuf):
        halves = []
        for half in range(2):
            off, lo = col_offset(j, half), half * fc
            cw = cw_ref[:, pl.ds(off, fc)]
            conv = cb_ref[:, pl.ds(off, fc)] + cw[FFN_CONV - 1:FFN_CONV, :] * ubuf[SUBLANES:SUBLANES + tm, lo:lo + fc]
            if decode:
                conv = conv + cw[1:2, :] * p1_ref[:, pl.ds(off, fc)] + cw[0:1, :] * p0_ref[:, pl.ds(off, fc)]
            else:
                for i in range(1, FFN_CONV):
                    conv = conv + (cw[FFN_CONV - 1 - i:FFN_CONV - i, :]
                                   * ubuf[SUBLANES - i:SUBLANES - i + tm, lo:lo + fc])
            halves.append(conv)
        act = _silu(halves[0]) * halves[1]
        r0 = pl.multiple_of(j * fc, fc)
        acc_ref[...] += jnp.dot(act.astype(BF16), dn_ref[pl.ds(r0, fc), :], preferred_element_type=F32)

    up_project(0, ubufs[0])

    def pair(i, _):
        up_project(2 * i + 1, ubufs[1])
        conv_gate_down(2 * i, ubufs[0])
        up_project(2 * i + 2, ubufs[0])
        conv_gate_down(2 * i + 1, ubufs[1])
        return 0

    assert n_chunks % 2 == 1
    lax.fori_loop(0, n_chunks // 2, pair, 0)
    conv_gate_down(n_chunks - 1, ubufs[0])
    resid = o_ref[...] if mix_out else x_ref[...]
    o_ref[...] = resid + gt_ref[...] * _rms(acc_ref[...], gp_ref[...])
    if not decode:
        cst_ref[...] = carry_ref[...]


def _ffn_call(x, shift, scale, gate, g, gp, up_w, conv_w, conv_b, dn_w, prev=None, mix=None):
    b, t, d = x.shape
    decode = prev is not None
    tm = min(FFN_ROW_TILE, t)
    row_spec = lambda c: pl.BlockSpec((None, tm, c), lambda bi, ti: (bi, ti, 0))
    mod_spec = (pl.BlockSpec((None, tm, d), lambda bi, ti: (bi, ti, 0)) if decode
                else pl.BlockSpec((None, 1, d), lambda bi, ti: (bi, 0, 0)))
    full = lambda a: pl.BlockSpec(a.shape, lambda bi, ti: (0,) * a.ndim)
    in_specs = [row_spec(d), mod_spec, mod_spec, mod_spec, full(g), full(gp), full(up_w), full(conv_w),
                full(conv_b), full(dn_w)]
    args = [x, shift, scale, gate, g, gp, up_w, conv_w, conv_b, dn_w]
    if mix is not None:
        att, y, gate0, gp0, wa, wy = mix
        in_specs = [row_spec(ATT_WIDTH), row_spec(SSD_INNER), mod_spec, full(gp0), full(wa), full(wy)] + in_specs
        args = [att, y, gate0, gp0, wa, wy] + args
    scratch = [pltpu.VMEM((tm + SUBLANES, 2 * FFN_CHUNK), F32)] * 2 + [pltpu.VMEM((tm, d), F32), pltpu.VMEM((tm, d), BF16)]
    if decode:
        assert b == 1 and t == tm
        in_specs += [pl.BlockSpec((tm, 2 * D_FF), lambda bi, ti: (ti, 0))] * 2
        args += list(prev)
        out_specs = [row_spec(d), pl.BlockSpec((tm, 2 * D_FF), lambda bi, ti: (ti, 0))]
        out_shape = [jax.ShapeDtypeStruct((b, t, d), F32), jax.ShapeDtypeStruct((t, 2 * D_FF), F32)]
    else:
        out_specs = [row_spec(d), pl.BlockSpec((None, SUBLANES, 2 * D_FF), lambda bi, ti: (bi, 0, 0))]
        out_shape = [jax.ShapeDtypeStruct((b, t, d), F32), jax.ShapeDtypeStruct((b, SUBLANES, 2 * D_FF), F32)]
        scratch = [pltpu.VMEM((SUBLANES, 2 * D_FF), F32)] + scratch
    return pl.pallas_call(
        functools.partial(_ffn_kernel, tm=tm, decode=decode, mix_out=mix is not None),
        grid=(b, t // tm), in_specs=in_specs, out_specs=out_specs, out_shape=out_shape,
        scratch_shapes=scratch, compiler_params=_cparams(2),
        name=("conv_ffn_sample" if decode else "conv_ffn_prompt") + ("_mix" if mix is not None else ""),
    )(*args)


def _pool_kernel(*refs, tm, decode, pos0):
    if decode:
        (x_ref, sh_ref, sc_ref, gt_ref, g_ref, gp_ref, pw_ref, pb_ref, ps_ref, prev_ref, o_ref, h_ref) = refs
    else:
        (x_ref, sh_ref, sc_ref, gt_ref, g_ref, gp_ref, pw_ref, pb_ref, ps_ref, o_ref, last_ref, buf_ref) = refs
        t = pl.program_id(1)

        @pl.when(t == 0)
        def _():
            buf_ref[0:POOL_MAX, :] = jnp.zeros((POOL_MAX, D_MODEL), F32)

    x = x_ref[...]
    h = _norm_mod(x, g_ref[...], sh_ref[...], sc_ref[...])
    if decode:
        h_ref[...] = h
    else:
        buf_ref[POOL_MAX:POOL_MAX + tm, :] = h
        row = lax.broadcasted_iota(jnp.int32, (tm, 1), 0)
        pos = pos0 + t * tm + row
    ys = []
    if not decode:
        level = buf_ref[...]
        doubled, width, first_lane = [], 1, 0
        for gi, win in enumerate(POOL_WINDOWS):
            lo = gi * POOL_GROUP_DIM
            level = level[:, lo - first_lane:]
            first_lane = lo
            while width < win:
                level = level + pltpu.roll(level, width, 0)
                width *= 2
            doubled.append(level[POOL_MAX:POOL_MAX + tm, 0:POOL_GROUP_DIM])
    for gi, win in enumerate(POOL_WINDOWS):
        lo = gi * POOL_GROUP_DIM
        hg = h[:, lo:lo + POOL_GROUP_DIM]
        if decode:
            wsum = hg
            for i in range(1, win):
                wsum = wsum + prev_ref[POOL_MAX - 1 - i, :, lo:lo + POOL_GROUP_DIM]
        else:
            wsum = doubled[gi]
        if decode:
            cnt = float(min(pos0 + 1, win))
        else:
            cnt = jnp.minimum(pos + 1, win).astype(F32)
        pooled = wsum / cnt - hg
        ys.append(jnp.dot(pooled.astype(BF16), pw_ref[gi], preferred_element_type=F32))
    y = (jnp.concatenate(ys, axis=1) + pb_ref[...]) * ps_ref[...]
    o_ref[...] = x + gt_ref[...] * _rms(y, gp_ref[...])
    if not decode:
        last_ref[...] = h[tm - POOL_MAX:tm, :]
        buf_ref[0:POOL_MAX, :] = h[tm - POOL_MAX:tm, :]


def _pool_call(x, shift, scale, gate, g, gp, pw, pb, ps, prev_t=None, pos0=0):
    b, t, d = x.shape
    decode = prev_t is not None
    tm = min(ROW_TILE, t)
    row_spec = lambda c: pl.BlockSpec((None, tm, c), lambda bi, ti: (bi, ti, 0))
    mod_spec = (pl.BlockSpec((None, tm, d), lambda bi, ti: (bi, ti, 0)) if decode
                else pl.BlockSpec((None, 1, d), lambda bi, ti: (bi, 0, 0)))
    full = lambda a: pl.BlockSpec(a.shape, lambda bi, ti: (0,) * a.ndim)
    in_specs = [row_spec(d), mod_spec, mod_spec, mod_spec, full(g), full(gp), full(pw), full(pb), full(ps)]
    args = [x, shift, scale, gate, g, gp, pw, pb, ps]
    if decode:
        assert b == 1 and t == tm
        in_specs.append(full(prev_t))
        args.append(prev_t)
        out_specs = [row_spec(d), row_spec(d)]
        out_shape = [jax.ShapeDtypeStruct((b, t, d), F32)] * 2
        scratch = []
    else:
        out_specs = [row_spec(d), pl.BlockSpec((None, POOL_MAX, d), lambda bi, ti: (bi, 0, 0))]
        out_shape = [jax.ShapeDtypeStruct((b, t, d), F32), jax.ShapeDtypeStruct((b, POOL_MAX, d), F32)]
        scratch = [pltpu.VMEM((tm + POOL_MAX, d), F32)]
    return pl.pallas_call(
        functools.partial(_pool_kernel, tm=tm, decode=decode, pos0=pos0),
        grid=(b, t // tm), in_specs=in_specs, out_specs=out_specs, out_shape=out_shape,
        scratch_shapes=scratch, compiler_params=_cparams(2),
        name="pool_sample" if decode else "pool_prompt",
    )(*args)


def _mixin_sample_kernel(x_ref, sh_ref, sc_ref, g_ref, w_ref, wdtx_ref, dtbx_ref, cw_ref, cb_ref, prev_ref,
                         q_ref, k_ref, v_ref, z_ref, xs_ref, bc_ref, dtx_ref, raw_ref, *, pos0):
    h = _norm_mod(x_ref[...], g_ref[...], sh_ref[...], sc_ref[...])
    proj = jnp.dot(h.astype(BF16), w_ref[...], preferred_element_type=F32)
    n = h.shape[0]
    cos_t, sin_t, low = _rope_tables(jnp.float32(pos0), (n, LANES))
    for gi in range(ATT_WIDTH // LANES):
        lo = gi * LANES
        q_ref[:, lo:lo + LANES] = _rope_group(proj[:, lo:lo + LANES], cos_t, sin_t, low)
        k_ref[:, lo:lo + LANES] = _rope_group(proj[:, ATT_WIDTH + lo:ATT_WIDTH + lo + LANES], cos_t, sin_t, low)
    v_ref[...] = proj[:, 2 * ATT_WIDTH:3 * ATT_WIDTH]
    z_ref[...] = proj[:, 3 * ATT_WIDTH:3 * ATT_WIDTH + SSD_INNER]
    raw = proj[:, 3 * ATT_WIDTH + SSD_INNER:MAIN_COLS]
    raw_ref[...] = raw
    conv = cb_ref[...] + cw_ref[SSD_CONV - 1:SSD_CONV, :] * raw
    for i in range(SSD_CONV - 1):
        conv = conv + cw_ref[i:i + 1, :] * prev_ref[i]
    act = _silu(conv)
    xs_ref[...] = act[:, :SSD_INNER]
    bc_ref[...] = act[:, SSD_INNER:]
    dtx_ref[...] = _softplus(jnp.dot(h, wdtx_ref[...], preferred_element_type=F32, precision=HIGHEST) + dtbx_ref[...])


def _mixin_sample_call(x, shift, scale, g, w_main, wdt_x, dtb_x, conv_w, conv_b, prev_t, pos0):
    n, d = x.shape
    full = lambda a: pl.BlockSpec(a.shape, lambda i: (0,) * a.ndim)
    args = [x, shift, scale, g, w_main, wdt_x, dtb_x, conv_w, conv_b, prev_t]
    widths = [ATT_WIDTH] * 3 + [SSD_INNER] * 2 + [SSD_CONV_DIM - SSD_INNER, SSD_INNER, SSD_CONV_DIM]
    return pl.pallas_call(
        functools.partial(_mixin_sample_kernel, pos0=pos0),
        grid=(1,),
        in_specs=[full(a) for a in args],
        out_specs=[pl.BlockSpec((n, c), lambda i: (0, 0)) for c in widths],
        out_shape=[jax.ShapeDtypeStruct((n, c), F32) for c in widths],
        compiler_params=_cparams(1), name="mixin_sample",
    )(*args)


_PAGES_PER_BLOCK = MOBA_BLOCK // PAGE_SIZE
_SEL_PAGES = MOBA_TOPK * _PAGES_PER_BLOCK
SAMPLE_HEADS_PER_STEP = 8


def _token_minor(cache):
    return jnp.transpose(cache, (0, 1, 3, 4, 2))


def _past_kmean_kernel(pt_ref, *refs):
    pages, o_ref = refs[:PAGES_PER_STEP], refs[PAGES_PER_STEP]
    step = pl.program_id(1)
    blocks_per_step = PAGES_PER_STEP // _PAGES_PER_BLOCK

    @pl.when(step == 0)
    def _():
        o_ref[...] = jnp.zeros(o_ref.shape, F32)

    acc = o_ref[...]
    lane = lax.broadcasted_iota(jnp.int32, acc.shape, 2)
    for blk in range(blocks_per_step):
        tot = pages[blk * _PAGES_PER_BLOCK][...]
        for r in range(1, _PAGES_PER_BLOCK):
            tot = tot + pages[blk * _PAGES_PER_BLOCK + r][...]
        mean = jnp.sum(tot, axis=-1, keepdims=True) * (1.0 / MOBA_BLOCK)
        acc = jnp.where(lane == step * blocks_per_step + blk, mean, acc)
    o_ref[...] = acc


def _past_kmean_call(pt_flat, cache_kt, n_seq, n_pages):
    steps = n_pages // PAGES_PER_STEP
    nb = n_pages // _PAGES_PER_BLOCK

    def page_map(b, s, pt, *, r):
        return (0, pt[b * n_pages + s * PAGES_PER_STEP + r], 0, 0, 0)

    grid_spec = pltpu.PrefetchScalarGridSpec(
        num_scalar_prefetch=1, grid=(n_seq, steps),
        in_specs=[pl.BlockSpec((None, None, ATT_HEADS, HEAD_DIM, PAGE_SIZE), functools.partial(page_map, r=r))
                  for r in range(PAGES_PER_STEP)],
        out_specs=pl.BlockSpec((None, ATT_HEADS, HEAD_DIM, nb), lambda b, s, pt: (b, 0, 0, 0)))
    return pl.pallas_call(
        _past_kmean_kernel, grid_spec=grid_spec,
        out_shape=jax.ShapeDtypeStruct((n_seq, ATT_HEADS, HEAD_DIM, nb), F32),
        compiler_params=_cparams(2), name="past_kmean",
    )(pt_flat, *([cache_kt] * PAGES_PER_STEP))


def _sample_select_kernel(q_ref, km_ref, sel_ref, *, jown):
    gate = jnp.sum(km_ref[...] * q_ref[...], axis=1, keepdims=True)
    ids = lax.broadcasted_iota(jnp.int32, gate.shape, 2).astype(F32)
    gate = jnp.where(ids < float(jown), gate, NEG_INF)
    picks = _topk_first(gate, ids, jown, 2)
    for r in range(MOBA_TOPK):
        sel_ref[r] = picks[r].astype(jnp.int32)


def _sample_select_call(q4, kmean, jown):
    n, _, _, nb = kmean.shape
    return pl.pallas_call(
        functools.partial(_sample_select_kernel, jown=jown),
        grid=(n,),
        in_specs=[pl.BlockSpec((None, ATT_HEADS, HEAD_DIM, 1), lambda b: (b, 0, 0, 0)),
                  pl.BlockSpec((None, ATT_HEADS, HEAD_DIM, nb), lambda b: (b, 0, 0, 0))],
        out_specs=pl.BlockSpec((None, MOBA_TOPK, ATT_HEADS, 1, 1), lambda b: (b, 0, 0, 0, 0)),
        out_shape=jax.ShapeDtypeStruct((n, MOBA_TOPK, ATT_HEADS, 1, 1), jnp.int32),
        compiler_params=_cparams(1), name="sample_select",
    )(q4, kmean)


def _sample_attn_kernel(pt_ref, sel_ref, q_ref, kn_ref, vn_ref, *refs):
    n_pg = SAMPLE_HEADS_PER_STEP * _SEL_PAGES
    kp, vp, o_ref = refs[:n_pg], refs[n_pg:2 * n_pg], refs[2 * n_pg]
    b, grp = pl.program_id(0), pl.program_id(1)
    scale = 1.0 / math.sqrt(HEAD_DIM)
    for hh in range(SAMPLE_HEADS_PER_STEP):
        h = grp * SAMPLE_HEADS_PER_STEP + hh
        q = q_ref[hh] * scale
        s_self = jnp.sum(q * kn_ref[hh], axis=0, keepdims=True)
        scores = []
        for r in range(MOBA_TOPK):
            valid = sel_ref[(b * MOBA_TOPK + r) * ATT_HEADS + h] >= 0
            for pg in range(_PAGES_PER_BLOCK):
                s = jnp.sum(kp[hh * _SEL_PAGES + r * _PAGES_PER_BLOCK + pg][...] * q, axis=0, keepdims=True)
                scores.append(jnp.where(valid, s, NEG_INF))
        m = s_self
        for s in scores:
            m = jnp.maximum(m, jnp.max(s, axis=1, keepdims=True))
        p_self = jnp.exp(s_self - m)
        l = p_self
        acc = p_self * vn_ref[hh]
        for idx, s in enumerate(scores):
            p = jnp.exp(s - m)
            l = l + jnp.sum(p, axis=1, keepdims=True)
            acc = acc + jnp.sum(vp[hh * _SEL_PAGES + idx][...] * p, axis=1, keepdims=True)
        o_ref[hh] = acc / l


def _sample_attn_call(pages_flat, sel_flat, q4, k4, v4, cache_kt, cache_vt):
    n = q4.shape[0]
    hps = SAMPLE_HEADS_PER_STEP
    head_spec = pl.BlockSpec((None, hps, HEAD_DIM, 1), lambda b, g, pt, sel: (b, g, 0, 0))

    def page_map(b, g, pages, sel, *, hh, r, pg):
        h = g * hps + hh
        return (0, pages[((b * MOBA_TOPK + r) * ATT_HEADS + h) * _PAGES_PER_BLOCK + pg], h, 0, 0)

    page_specs = [pl.BlockSpec((None, None, None, HEAD_DIM, PAGE_SIZE), functools.partial(page_map, hh=hh, r=r, pg=pg))
                  for hh in range(hps) for r in range(MOBA_TOPK) for pg in range(_PAGES_PER_BLOCK)]
    grid_spec = pltpu.PrefetchScalarGridSpec(
        num_scalar_prefetch=2, grid=(n, ATT_HEADS // hps),
        in_specs=[head_spec, head_spec, head_spec] + page_specs + page_specs,
        out_specs=head_spec)
    n_pg = hps * _SEL_PAGES
    return pl.pallas_call(
        _sample_attn_kernel, grid_spec=grid_spec,
        out_shape=jax.ShapeDtypeStruct((n, ATT_HEADS, HEAD_DIM, 1), F32),
        compiler_params=_cparams(2), name="sample_attn",
    )(pages_flat, sel_flat, q4, k4, v4, *([cache_kt] * n_pg), *([cache_vt] * n_pg))


def _ssd_sample_kernel(xs_ref, bc_ref, dtx_ref, z_ref, st_ref, alogx_ref, dsk_ref, nw_ref, y_ref, sto_ref):
    xs, dtx = xs_ref[...], dtx_ref[...]
    decay = jnp.exp(dtx * (-jnp.exp(alogx_ref[...])))
    xdt = xs * dtx
    bc = bc_ref[...]
    rows_per_group = SSD_INNER // SSD_GROUPS
    gs = SSD_GROUPS * SSD_STATE
    eye = (lax.broadcasted_iota(jnp.int32, (rows_per_group, rows_per_group), 0)
           == lax.broadcasted_iota(jnp.int32, (rows_per_group, rows_per_group), 1))

    def to_col(v):
        return jnp.sum(jnp.where(eye, jnp.broadcast_to(v, eye.shape), 0.0), axis=1, keepdims=True)

    def to_row(v):
        return jnp.sum(jnp.where(eye, jnp.broadcast_to(v, eye.shape), 0.0), axis=0, keepdims=True)

    ys = []
    for g in range(SSD_GROUPS):
        lo = g * rows_per_group
        b_row = bc[:, g * SSD_STATE:(g + 1) * SSD_STATE]
        c_row = bc[:, gs + g * SSD_STATE:gs + (g + 1) * SSD_STATE]
        hg = st_ref[lo:lo + rows_per_group, :]
        hnew = to_col(decay[:, lo:lo + rows_per_group]) * hg + to_col(xdt[:, lo:lo + rows_per_group]) * b_row
        sto_ref[lo:lo + rows_per_group, :] = hnew
        ys.append(to_row(jnp.sum(hnew * c_row, axis=1, keepdims=True)))
    y = jnp.concatenate(ys, axis=1)
    y_ref[...] = _gated_norm(y, xs, z_ref[...], dsk_ref[...], nw_ref[...])


def _ssd_sample_call(xs, bc, dtx, z, state, alog_x, dsk_row, nw_row):
    n = xs.shape[0]
    r3 = lambda a: a.reshape(n, 1, a.shape[-1])
    row_spec = lambda c: pl.BlockSpec((None, 1, c), lambda b: (b, 0, 0))
    st_spec = pl.BlockSpec((None, SSD_INNER, SSD_STATE), lambda b: (b, 0, 0))
    full = lambda a: pl.BlockSpec(a.shape, lambda b: (0,) * a.ndim)
    y, st = pl.pallas_call(
        _ssd_sample_kernel,
        grid=(n,),
        in_specs=[row_spec(SSD_INNER), row_spec(SSD_CONV_DIM - SSD_INNER), row_spec(SSD_INNER), row_spec(SSD_INNER),
                  st_spec, full(alog_x), full(dsk_row), full(nw_row)],
        out_specs=[row_spec(SSD_INNER), st_spec],
        out_shape=[jax.ShapeDtypeStruct((n, 1, SSD_INNER), F32), jax.ShapeDtypeStruct((n, SSD_INNER, SSD_STATE), F32)],
        compiler_params=_cparams(1), name="ssd_sample",
    )(r3(xs), r3(bc), r3(dtx), r3(z), state, alog_x, dsk_row, nw_row)
    return y.reshape(n, SSD_INNER), st


def kernel(x_prompt, x_sample, cache_k, cache_v, state_ssm, state_ssd_conv, state_pool, state_ffn_conv, page_table,
           c_prompt, c_sample, ada_w, ada_b, norm_pre, norm_post, mix_in_w, mix_out_w, ssd_conv_w, ssd_conv_b,
           ssd_dt_bias, ssd_a_log, ssd_d, ssd_norm_w, pool_w, pool_b, pool_scale, ffn_up_w, ffn_conv_w,
           ffn_conv_b, ffn_down_w):
    bp, t, d = x_prompt.shape
    ns = x_sample.shape[0]
    n_pages = page_table.shape[1]
    past_len = n_pages * cache_k.shape[2]
    assert x_sample.shape[1] == 1 and cache_k.shape[2] == PAGE_SIZE and past_len % MOBA_BLOCK == 0
    assert t % ROW_TILE == 0 or t < ROW_TILE

    w_main = mix_in_w[0, :, :MAIN_COLS].astype(BF16)
    w_dt = mix_in_w[0, :, MAIN_COLS:]
    wdt_t = w_dt.T
    wdt_x = jnp.repeat(w_dt, SSD_HEADDIM, axis=1)
    dtb_col = ssd_dt_bias[0].reshape(SSD_HEADS, 1)
    dtb_x = jnp.repeat(ssd_dt_bias[0], SSD_HEADDIM).reshape(1, SSD_INNER)
    alog_col = ssd_a_log[0].reshape(SSD_HEADS, 1)
    alog_x = jnp.repeat(ssd_a_log[0], SSD_HEADDIM).reshape(1, SSD_INNER)
    dsk_row = jnp.repeat(ssd_d[0], SSD_HEADDIM).reshape(1, SSD_INNER)
    nw_row = ssd_norm_w[0].reshape(1, SSD_INNER)
    w_att = mix_out_w[0, :ATT_WIDTH].astype(BF16)
    w_ssd = mix_out_w[0, ATT_WIDTH:].astype(BF16)
    conv_w0 = ssd_conv_w[0]
    conv_b0 = ssd_conv_b[0].reshape(1, SSD_CONV_DIM)
    up_w = ffn_up_w.astype(BF16)
    dn_w = ffn_down_w.astype(BF16)
    pw = pool_w[0].astype(BF16)
    pb = pool_b[0].reshape(1, d)
    ps = pool_scale[0].reshape(1, d)

    mods = _ada_call(jnp.concatenate([c_prompt, c_sample], axis=0), ada_w, ada_b)

    def mod(layer, slot, sample):
        m = mods[layer * 2 + slot]
        m = m[bp:] if sample else m[:bp]
        parts = (m[:, :d], m[:, d:2 * d], m[:, 2 * d:])
        return tuple(p[None] for p in parts) if sample else tuple(p[:, None, :] for p in parts)

    npre = lambda layer, slot: norm_pre[layer, slot].reshape(1, d)
    npost = lambda layer, slot: norm_post[layer, slot].reshape(1, d)

    def ffn(x, layer, sample, prev=None, mix=None):
        sh, sc, gt = mod(layer, 1, sample)
        return _ffn_call(x, sh, sc, gt, npre(layer, 1), npost(layer, 1), up_w[layer], ffn_conv_w[layer],
                         ffn_conv_b[layer].reshape(1, 2 * D_FF), dn_w[layer], prev=prev, mix=mix)

    sh, sc, gt = mod(0, 0, False)
    q, k, v, z, xs, bc, dt_t, kmean, cst = _mixin_prompt_call(
        x_prompt, sh, sc, npre(0, 0), w_main, wdt_t, dtb_col, conv_w0, conv_b0)
    att = _attn_prompt_call(q, k, v, kmean)
    y, ssm_p = _ssd_prompt_call(xs, bc, dt_t, z, alog_col, dsk_row, nw_row)
    xp, fst0 = ffn(x_prompt, 0, False, mix=(att, y, gt, npost(0, 0), w_att, w_ssd))
    sh, sc, gt = mod(1, 0, False)
    xp, plast = _pool_call(xp, sh, sc, gt, npre(1, 0), npost(1, 0), pw, pb, ps)
    xp, fst1 = ffn(xp, 1, False)

    k_p = k.reshape(1, bp, t, ATT_HEADS, HEAD_DIM)
    v_p = v.reshape(1, bp, t, ATT_HEADS, HEAD_DIM)
    ssm_p = ssm_p.reshape(1, bp, SSD_HEADS, SSD_HEADDIM, SSD_STATE)
    sconv_p = cst[None, :, SUBLANES - (SSD_CONV - 1):, :]
    pool_p = plast[None, :, 1:, :]
    fconv_p = jnp.stack([fst0, fst1])[:, :, SUBLANES - (FFN_CONV - 1):, :]

    xs0 = x_sample.reshape(ns, d)
    sh, sc, gt = mod(0, 0, True)
    conv_prev_t = jnp.transpose(state_ssd_conv[0], (1, 0, 2))
    qs, ks, vs, zs, xss, bcs, dtx, raw = _mixin_sample_call(
        xs0, sh[0], sc[0], npre(0, 0), w_main, wdt_x, dtb_x, conv_w0, conv_b0, conv_prev_t, past_len)
    pt_flat = page_table.reshape(-1)
    cache_kt, cache_vt = _token_minor(cache_k), _token_minor(cache_v)
    kmean_s = _past_kmean_call(pt_flat, cache_kt, ns, n_pages)
    jown = past_len // MOBA_BLOCK
    as_cols = lambda a: a.reshape(ns, ATT_HEADS, HEAD_DIM, 1)
    sel = _sample_select_call(as_cols(qs), kmean_s, jown)
    sel3 = sel.reshape(ns, MOBA_TOPK, ATT_HEADS)
    page_pos = jnp.maximum(sel3, 0)[..., None] * _PAGES_PER_BLOCK + jnp.arange(_PAGES_PER_BLOCK, dtype=jnp.int32)
    sel_pages = jnp.take_along_axis(page_table, page_pos.reshape(ns, -1), axis=1)
    att_s = _sample_attn_call(sel_pages.reshape(-1), sel3.reshape(-1), as_cols(qs), as_cols(ks), as_cols(vs),
                              cache_kt, cache_vt).reshape(ns, ATT_WIDTH)
    y_s, ssm_s = _ssd_sample_call(xss, bcs, dtx, zs, state_ssm[0].reshape(ns, SSD_INNER, SSD_STATE),
                                  alog_x, dsk_row, nw_row)
    xq, u0 = ffn(xs0[None], 0, True, prev=(state_ffn_conv[0, :, 0], state_ffn_conv[0, :, 1]),
                 mix=(att_s[None], y_s[None], gt, npost(0, 0), w_att, w_ssd))
    sh, sc, gt = mod(1, 0, True)
    pool_prev_t = jnp.transpose(state_pool[0], (1, 0, 2))
    xq, h_pool = _pool_call(xq, sh, sc, gt, npre(1, 0), npost(1, 0), pw, pb, ps, prev_t=pool_prev_t, pos0=past_len)
    xq, u1 = ffn(xq, 1, True, prev=(state_ffn_conv[1, :, 0], state_ffn_conv[1, :, 1]))

    y_sample = xq.reshape(ns, 1, d)
    k_s = ks.reshape(1, ns, 1, ATT_HEADS, HEAD_DIM)
    v_s = vs.reshape(1, ns, 1, ATT_HEADS, HEAD_DIM)
    ssm_s = ssm_s.reshape(1, ns, SSD_HEADS, SSD_HEADDIM, SSD_STATE)
    sconv_s = jnp.concatenate([state_ssd_conv[0, :, 1:], raw[:, None, :]], axis=1)[None]
    pool_s = jnp.concatenate([state_pool[0, :, 1:], h_pool[0][:, None, :]], axis=1)[None]
    fconv_s = jnp.stack([jnp.stack([state_ffn_conv[0, :, 1], u0], axis=1),
                         jnp.stack([state_ffn_conv[1, :, 1], u1], axis=1)])
    return (xp, y_sample, k_p, v_p, ssm_p, sconv_p, pool_p, fconv_p, k_s, v_s, ssm_s, sconv_s, pool_s, fconv_s)
```

```python
import functools
import math

import jax
import jax.numpy as jnp
from jax import lax
from jax.experimental import pallas as pl
from jax.experimental.pallas import tpu as pltpu

F32 = jnp.float32
BF16 = jnp.bfloat16
HIGHEST = lax.Precision.HIGHEST

D_MODEL = 1024
ATT_HEADS = 8
HEAD_DIM = 64
ATT_WIDTH = ATT_HEADS * HEAD_DIM
ROT_DIM = HEAD_DIM // 4
ROPE_THETA = 500000.0
MOBA_BLOCK = 256
MOBA_TOPK = 3
SSD_HEADS = 8
SSD_HEADDIM = 64
SSD_INNER = SSD_HEADS * SSD_HEADDIM
SSD_GROUPS = 2
SSD_STATE = 128
SSD_CONV = 4
SSD_CONV_DIM = SSD_INNER + 2 * SSD_GROUPS * SSD_STATE
POOL_WINDOWS = (2, 4, 8, 16)
POOL_GROUP_DIM = D_MODEL // len(POOL_WINDOWS)
POOL_MAX = 16
D_FF = 2816
FFN_CONV = 3
EPS = 1e-6
PAGE_SIZE = 128
MAIN_COLS = 3 * ATT_WIDTH + SSD_INNER + SSD_CONV_DIM

LANES = 128
SUBLANES = 8
VMEM_LIMIT_BYTES = 56 * 1024 * 1024

ROW_TILE = 512
FFN_ROW_TILE = 512
ATTN_Q_TILE = 256
SSD_CHUNK = 128
SSD_CHUNKS_PER_STEP = 4
FFN_CHUNK = 256
PAGES_PER_STEP = 32

ATTN_KV_BLOCKS = 4
ATTN_TILES_PER_STEP = 4

NEG_INF = float("-inf")
MASKED = -1e30
LOG2_E = 1.4426950408889634
VT_ROWS = HEAD_DIM + 2 * SUBLANES


def _cparams(n_axes, flags=None):
    return pltpu.CompilerParams(dimension_semantics=("arbitrary",) * n_axes,
                                vmem_limit_bytes=VMEM_LIMIT_BYTES, flags=flags)


def _silu(x):
    return x / (1.0 + jnp.exp(-x))


def _softplus(x):
    return jnp.maximum(x, 0.0) + jnp.log1p(jnp.exp(-jnp.abs(x)))


def _rms(x, g):
    return x * lax.rsqrt(jnp.mean(x * x, axis=-1, keepdims=True) + EPS) * g


def _norm_mod(x, g, shift, scale):
    return _rms(x, g) * (1.0 + scale) + shift


def _dot_nt(a, b, precision=None):
    return lax.dot_general(a, b, (((1,), (1,)), ((), ())), preferred_element_type=F32, precision=precision)


def _dot_tn(a, b):
    return lax.dot_general(a, b, (((0,), (0,)), ((), ())), preferred_element_type=F32)


def _rope_tables(pos, shape):
    lane = lax.broadcasted_iota(jnp.int32, shape, 1)
    j = lane & (HEAD_DIM - 1)
    half = ROT_DIM // 2
    inv = jnp.exp((lane & (half - 1)).astype(F32) * (-2.0 * math.log(ROPE_THETA) / ROT_DIM))
    ang = pos * inv
    c, s = jnp.cos(ang), jnp.sin(ang)
    cos_t = jnp.where(j < ROT_DIM, c, 1.0)
    sin_t = jnp.where(j < half, -s, jnp.where(j < ROT_DIM, s, 0.0))
    return cos_t, sin_t, j < half


def _rope_group(xg, cos_t, sin_t, low):
    half = ROT_DIM // 2
    partner = jnp.where(low, pltpu.roll(xg, LANES - half, 1), pltpu.roll(xg, half, 1))
    return xg * cos_t + partner * sin_t


def _ada_kernel(c_ref, w_ref, b_ref, o_ref):
    o_ref[...] = jnp.dot(_silu(c_ref[...]), w_ref[...], preferred_element_type=F32, precision=HIGHEST) + b_ref[...]


def _ada_call(c_all, ada_w, ada_b):
    nl, ns, d, d3 = ada_w.shape
    n = c_all.shape[0]
    tn = 1024
    return pl.pallas_call(
        _ada_kernel,
        grid=(nl * ns, d3 // tn),
        in_specs=[pl.BlockSpec((n, d), lambda i, j: (0, 0)),
                  pl.BlockSpec((None, None, d, tn), lambda i, j: (i // ns, i % ns, 0, j)),
                  pl.BlockSpec((None, 1, tn), lambda i, j: (i, 0, j))],
        out_specs=pl.BlockSpec((None, n, tn), lambda i, j: (i, 0, j)),
        out_shape=jax.ShapeDtypeStruct((nl * ns, n, d3), F32),
        compiler_params=_cparams(2), name="ada_mod",
    )(c_all, ada_w, ada_b.reshape(nl * ns, 1, d3))


def _mixin_prompt_kernel(x_ref, sh_ref, sc_ref, g_ref, w_ref, wdt_ref, dtb_ref, cw_ref, cb_ref,
                         q_ref, k_ref, v_ref, z_ref, xs_ref, bc_ref, dt_ref, km_ref, cst_ref,
                         buf_ref, oc_ref, os_ref, *, tm):
    t = pl.program_id(1)
    lane = lax.broadcasted_iota(jnp.int32, (1, LANES), 1)
    j = lane & (HEAD_DIM - 1)
    half = ROT_DIM // 2
    inv = jnp.exp((lane & (half - 1)).astype(F32) * (-2.0 * math.log(ROPE_THETA) / ROT_DIM))
    rot = jnp.where(j < ROT_DIM, 1.0, 0.0)
    sgn = jnp.where(j < half, -1.0, rot)
    low = j < half

    @pl.when(t == 0)
    def _():
        buf_ref[0:SUBLANES, :] = jnp.zeros((SUBLANES, SSD_CONV_DIM), F32)
        ang = lax.broadcasted_iota(jnp.int32, (tm, LANES), 0).astype(F32) * inv
        oc_ref[...] = jnp.cos(ang)
        os_ref[...] = jnp.sin(ang)

    h = _norm_mod(x_ref[...], g_ref[...], sh_ref[...], sc_ref[...])
    proj = jnp.dot(h.astype(BF16), w_ref[...], preferred_element_type=F32)

    base = (t * tm).astype(F32) * inv
    cb, sb = jnp.cos(base), jnp.sin(base)
    oc, osn = oc_ref[...], os_ref[...]
    cos_t = (cb * rot) * oc - (sb * rot) * osn + (1.0 - rot)
    sin_t = (sb * sgn) * oc + (cb * sgn) * osn
    for gi in range(ATT_WIDTH // LANES):
        lo = gi * LANES
        q_ref[:, lo:lo + LANES] = _rope_group(proj[:, lo:lo + LANES], cos_t, sin_t, low)
        k_ref[:, lo:lo + LANES] = _rope_group(proj[:, ATT_WIDTH + lo:ATT_WIDTH + lo + LANES], cos_t, sin_t, low)
    v_ref[...] = proj[:, 2 * ATT_WIDTH:3 * ATT_WIDTH]
    z_ref[...] = proj[:, 3 * ATT_WIDTH:3 * ATT_WIDTH + SSD_INNER]
    for blk in range(tm // MOBA_BLOCK):
        kb = k_ref[blk * MOBA_BLOCK:(blk + 1) * MOBA_BLOCK, :]
        km_ref[blk] = jnp.sum(kb, axis=0, keepdims=True) * (1.0 / MOBA_BLOCK)

    raw = proj[:, 3 * ATT_WIDTH + SSD_INNER:MAIN_COLS]
    buf_ref[SUBLANES:SUBLANES + tm, :] = raw
    conv = cb_ref[...] + cw_ref[SSD_CONV - 1:SSD_CONV, :] * raw
    for i in range(1, SSD_CONV):
        conv = conv + cw_ref[SSD_CONV - 1 - i:SSD_CONV - i, :] * buf_ref[SUBLANES - i:SUBLANES - i + tm, :]
    act = _silu(conv)
    xs_ref[...] = act[:, :SSD_INNER]
    bc_ref[...] = act[:, SSD_INNER:].astype(BF16)
    cst_ref[...] = raw[tm - SUBLANES:tm, :]
    buf_ref[0:SUBLANES, :] = raw[tm - SUBLANES:tm, :]

    dt_raw = _dot_nt(wdt_ref[...], h, precision=HIGHEST)
    dt_ref[...] = _softplus(dt_raw + dtb_ref[...])


def _mixin_prompt_call(x, shift, scale, g, w_main, wdt_t, dtb_col, conv_w, conv_b):
    b, t, d = x.shape
    tm = min(ROW_TILE, t)
    nt = t // tm
    nbt = tm // MOBA_BLOCK
    row_spec = lambda c: pl.BlockSpec((None, tm, c), lambda bi, ti: (bi, ti, 0))
    mod_spec = pl.BlockSpec((None, 1, d), lambda bi, ti: (bi, 0, 0))
    full = lambda a: pl.BlockSpec(a.shape, lambda bi, ti: (0,) * a.ndim)
    outs = pl.pallas_call(
        functools.partial(_mixin_prompt_kernel, tm=tm),
        grid=(b, nt),
        in_specs=[row_spec(d), mod_spec, mod_spec, full(g), full(w_main), full(wdt_t), full(dtb_col),
                  full(conv_w), full(conv_b)],
        out_specs=[row_spec(ATT_WIDTH), row_spec(ATT_WIDTH), row_spec(ATT_WIDTH), row_spec(SSD_INNER),
                   row_spec(SSD_INNER), row_spec(SSD_CONV_DIM - SSD_INNER),
                   pl.BlockSpec((None, SSD_HEADS, tm), lambda bi, ti: (bi, 0, ti)),
                   pl.BlockSpec((nbt, 1, ATT_WIDTH), lambda bi, ti: (bi * nt + ti, 0, 0)),
                   pl.BlockSpec((None, SUBLANES, SSD_CONV_DIM), lambda bi, ti: (bi, 0, 0))],
        out_shape=[jax.ShapeDtypeStruct((b, t, ATT_WIDTH), F32)] * 3
        + [jax.ShapeDtypeStruct((b, t, SSD_INNER), F32)] * 2
        + [jax.ShapeDtypeStruct((b, t, SSD_CONV_DIM - SSD_INNER), BF16),
           jax.ShapeDtypeStruct((b, SSD_HEADS, t), F32),
           jax.ShapeDtypeStruct((b * nt * nbt, 1, ATT_WIDTH), F32),
           jax.ShapeDtypeStruct((b, SUBLANES, SSD_CONV_DIM), F32)],
        scratch_shapes=[pltpu.VMEM((tm + SUBLANES, SSD_CONV_DIM), F32), pltpu.VMEM((tm, LANES), F32),
                        pltpu.VMEM((tm, LANES), F32)],
        compiler_params=_cparams(2), name="mixin_prompt",
    )(x, shift, scale, g, w_main, wdt_t, dtb_col, conv_w, conv_b)
    q, k, v, z, xs, bc, dt_t, kmean, cst = outs
    return q, k, v, z, xs, bc, dt_t, kmean.reshape(b, t // MOBA_BLOCK, ATT_WIDTH), cst


def _topk_first(gate, ids, n_valid, axis):
    picks = []
    for r in range(MOBA_TOPK):
        mx = jnp.max(gate, axis=axis, keepdims=True)
        idx = jnp.min(jnp.where(gate == mx, ids, 1e9), axis=axis, keepdims=True)
        idx = jnp.where(r < n_valid, idx, -1.0)
        picks.append(idx)
        gate = jnp.where(ids == idx, NEG_INF, gate)
    return picks


def _attn_prompt_kernel(q_ref, k_ref, v_ref, km_ref, o_ref, ka_scr, vt_scr, acc_scr, *, tq, t):
    i = pl.program_id(2)
    kv = ATTN_KV_BLOCKS * MOBA_BLOCK
    nb = km_ref.shape[0]

    @pl.when(i == 0)
    def _():
        blk_lane = lax.broadcasted_iota(jnp.int32, (MOBA_BLOCK, LANES), 1)
        sub = lax.broadcasted_iota(jnp.int32, (2 * SUBLANES, MOBA_BLOCK), 0)
        ones_rows = jnp.where(sub == 0, 1.0, 0.0).astype(BF16)

        def prep(c, _):
            r0 = pl.multiple_of(c * MOBA_BLOCK, MOBA_BLOCK)
            ka_scr[pl.ds(r0, MOBA_BLOCK), 0:LANES] = k_ref[pl.ds(r0, MOBA_BLOCK), :].astype(BF16)
            ka_scr[pl.ds(r0, MOBA_BLOCK), LANES:2 * LANES] = jnp.where(blk_lane == c, 1.0, 0.0).astype(BF16)
            vt = v_ref[pl.ds(r0, MOBA_BLOCK), :].T.astype(BF16)
            for hh in range(2):
                vt_scr[hh, 0:HEAD_DIM, pl.ds(r0, MOBA_BLOCK)] = vt[hh * HEAD_DIM:(hh + 1) * HEAD_DIM]
                vt_scr[hh, HEAD_DIM:VT_ROWS, pl.ds(r0, MOBA_BLOCK)] = ones_rows
            return 0

        lax.fori_loop(0, t // MOBA_BLOCK, prep, 0)

    n = 2 * tq
    nch = ATTN_TILES_PER_STEP
    km = km_ref[...]
    lane = lax.broadcasted_iota(jnp.int32, (tq, LANES), 1)
    ids = lax.broadcasted_iota(jnp.int32, (nb, n), 0).astype(F32)
    ids_all = lax.broadcasted_iota(jnp.int32, (LANES, n), 0).astype(F32)
    key = lax.broadcasted_iota(jnp.int32, (tq, n), 0)
    qry = lax.broadcasted_iota(jnp.int32, (tq, n), 1) & (tq - 1)
    qas, ms = [], []
    for c in range(nch):
        jown = i * nch + c
        q = q_ref[c * tq:(c + 1) * tq, :]
        qcat = jnp.concatenate([jnp.where(lane < HEAD_DIM, q, 0.0), jnp.where(lane >= HEAD_DIM, q, 0.0)], axis=0)
        gate = _dot_nt(km, qcat, precision=HIGHEST)
        gate = jnp.where(ids < jown.astype(F32), gate, NEG_INF)
        p0, p1, p2 = _topk_first(gate, ids, jown, 0)
        bias = jnp.where((ids_all == p0) | (ids_all == p1) | (ids_all == p2), 0.0, MASKED).T
        qs = qcat * (LOG2_E / math.sqrt(HEAD_DIM))
        qas.append(jnp.concatenate([qs, bias], axis=1).astype(BF16))

        r0 = pl.multiple_of(jown * tq, tq)
        st = jnp.where(key <= qry, _dot_nt(ka_scr[pl.ds(r0, tq), 0:LANES], qs.astype(BF16)), NEG_INF)
        m = jnp.max(st, axis=0, keepdims=True)
        p = jnp.exp2(st - m)
        pb = p.astype(BF16)
        for hh in range(2):
            acc_scr[c, hh] = jnp.dot(vt_scr[hh, :, pl.ds(r0, tq)], pb[:, hh * tq:(hh + 1) * tq],
                                     preferred_element_type=F32)
        ms.append(m)

    def scores(c, c0, rows):
        return _dot_nt(ka_scr[pl.ds(c0, rows), :], qas[c])

    def update(c, m, st, c0, rows):
        m_new = jnp.maximum(m, jnp.max(st, axis=0, keepdims=True))
        pb = jnp.exp2(st - m_new).astype(BF16)
        alpha = jnp.exp2(m - m_new)
        for hh in range(2):
            acc_scr[c, hh] = alpha[:, hh * tq:(hh + 1) * tq] * acc_scr[c, hh] + jnp.dot(
                vt_scr[hh, :, pl.ds(c0, rows)], pb[:, hh * tq:(hh + 1) * tq], preferred_element_type=F32)
        return m_new

    def body(j, ms):
        c0 = pl.multiple_of(j * kv, kv)
        sts = [scores(c, c0, kv) for c in range(nch)]
        return tuple(update(c, ms[c], sts[c], c0, kv) for c in range(nch))

    ms = lax.fori_loop(0, (i * nch) // ATTN_KV_BLOCKS, body, tuple(ms))
    base = pl.multiple_of(i * nch * MOBA_BLOCK, MOBA_BLOCK)
    tails = [scores(c, base, c * MOBA_BLOCK) for c in range(1, nch)]
    for c in range(1, nch):
        update(c, ms[c], tails[c - 1], base, c * MOBA_BLOCK)
    for c in range(nch):
        outs = []
        for hh in range(2):
            acc = acc_scr[c, hh]
            outs.append(acc[0:HEAD_DIM, :] / acc[HEAD_DIM:HEAD_DIM + 1, :])
        o_ref[c * tq:(c + 1) * tq, :] = jnp.concatenate(outs, axis=0).T.astype(BF16)


def _attn_prompt_call(q, k, v, kmean):
    b, t, _ = q.shape
    tq = min(ATTN_Q_TILE, t)
    nb = kmean.shape[1]
    nch = ATTN_TILES_PER_STEP
    assert tq == MOBA_BLOCK and tq & (tq - 1) == 0 and nb <= LANES and t % (ATTN_KV_BLOCKS * MOBA_BLOCK) == 0
    assert t % (nch * tq) == 0 and nch % ATTN_KV_BLOCKS == 0
    npair = ATT_WIDTH // LANES
    return pl.pallas_call(
        functools.partial(_attn_prompt_kernel, tq=tq, t=t),
        scratch_shapes=[pltpu.VMEM((t, 2 * LANES), BF16), pltpu.VMEM((2, VT_ROWS, t), BF16),
                        pltpu.VMEM((nch, 2, VT_ROWS, tq), F32)],
        grid=(b, npair, t // (nch * tq)),
        in_specs=[pl.BlockSpec((None, nch * tq, LANES), lambda bi, hp, i: (bi, i, hp)),
                  pl.BlockSpec((None, t, LANES), lambda bi, hp, i: (bi, 0, hp)),
                  pl.BlockSpec((None, t, LANES), lambda bi, hp, i: (bi, 0, hp)),
                  pl.BlockSpec((None, nb, LANES), lambda bi, hp, i: (bi, 0, hp))],
        out_specs=pl.BlockSpec((None, nch * tq, LANES), lambda bi, hp, i: (bi, i, hp)),
        out_shape=jax.ShapeDtypeStruct((b, t, ATT_WIDTH), BF16),
        compiler_params=_cparams(3), name="moba_prompt",
    )(q, k, v, kmean)


def _gated_norm(y, xs, z, dsk, nw):
    yg = (y + dsk * xs) * _silu(z)
    return _rms(yg, nw)


def _ssd_prompt_kernel(xs_ref, bc_ref, dt_ref, z_ref, alog_ref, dsk_ref, nw_ref, y_ref, st_ref, h_scr, *, L, n_sub):
    c = pl.program_id(1)

    @pl.when(c == 0)
    def _():
        h_scr[...] = jnp.zeros(h_scr.shape, F32)

    a = -jnp.exp(alog_ref[...])
    for sub in range(n_sub):
        _ssd_chunk(sub * L, L, a, xs_ref, bc_ref, dt_ref, z_ref, dsk_ref, nw_ref, y_ref, h_scr)
    st_ref[...] = h_scr[...]


def _ssd_chunk(r0, L, a, xs_ref, bc_ref, dt_ref, z_ref, dsk_ref, nw_ref, y_ref, h_scr):
    dt = dt_ref[:, r0:r0 + L]
    lane = lax.broadcasted_iota(jnp.int32, dt.shape, 1)
    acum = dt * a
    s = 1
    while s < L:
        acum = acum + jnp.where(lane >= s, pltpu.roll(acum, s, 1), 0.0)
        s *= 2
    nh = SSD_HEADS
    e_rows = jnp.exp(acum)
    w_rows = jnp.exp(acum[:, L - 1:L] - acum) * dt
    rows = jnp.concatenate([acum, e_rows, w_rows, jnp.zeros((L - 3 * nh, L), F32)], axis=0)
    cols = rows.T

    xs = xs_ref[r0:r0 + L, :]
    bc = bc_ref[r0:r0 + L, :]
    ri = lax.broadcasted_iota(jnp.int32, (L, L), 0)
    ci = lax.broadcasted_iota(jnp.int32, (L, L), 1)
    causal = ri >= ci
    low = lax.broadcasted_iota(jnp.int32, (L, LANES), 1) < SSD_HEADDIM
    low_rows = lax.broadcasted_iota(jnp.int32, (LANES, SSD_STATE), 0) < SSD_HEADDIM
    gs = SSD_GROUPS * SSD_STATE
    heads_per_group = SSD_HEADS // SSD_GROUPS
    ys = []
    for g in range(SSD_GROUPS):
        bg = bc[:, g * SSD_STATE:(g + 1) * SSD_STATE].astype(BF16)
        cg = bc[:, gs + g * SSD_STATE:gs + (g + 1) * SSD_STATE].astype(BF16)
        cbt = _dot_nt(cg, bg)
        for pr in range(heads_per_group // 2):
            hp = g * (heads_per_group // 2) + pr
            xp = xs[:, hp * LANES:(hp + 1) * LANES]
            xpb = xp.astype(BF16)
            hst = h_scr[hp * LANES:(hp + 1) * LANES, :]
            inter = _dot_nt(cg, hst.astype(BF16))
            intra, ecols, wcols, cds = [], [], [], []
            for hh in range(2):
                hd = 2 * hp + hh
                col = cols[:, hd:hd + 1]
                row = acum[hd:hd + 1, :]
                decay = jnp.exp(jnp.where(causal, col - row, NEG_INF))
                m = cbt * decay * dt[hd:hd + 1, :]
                intra.append(jnp.dot(m.astype(BF16), xpb, preferred_element_type=F32))
                last = acum[hd:hd + 1, L - 1:L]
                ecols.append(cols[:, nh + hd:nh + hd + 1])
                wcols.append(cols[:, 2 * nh + hd:2 * nh + hd + 1])
                cds.append(jnp.exp(last))
            ys.append(jnp.where(low, intra[0], intra[1]) + inter * jnp.where(low, ecols[0], ecols[1]))
            xw = (xp * jnp.where(low, wcols[0], wcols[1])).astype(BF16)
            upd = _dot_tn(xw, bg)
            h_scr[hp * LANES:(hp + 1) * LANES, :] = hst * jnp.where(low_rows, cds[0], cds[1]) + upd
    y = jnp.concatenate(ys, axis=1)
    y_ref[r0:r0 + L, :] = _gated_norm(y, xs, z_ref[r0:r0 + L, :], dsk_ref[...], nw_ref[...]).astype(BF16)


def _ssd_prompt_call(xs, bc, dt_t, z, alog_col, dsk_row, nw_row):
    b, t, _ = xs.shape
    L = SSD_CHUNK
    n_sub = SSD_CHUNKS_PER_STEP
    rows = L * n_sub
    assert t % rows == 0
    row_spec = lambda c: pl.BlockSpec((None, rows, c), lambda bi, ci: (bi, ci, 0))
    full = lambda a: pl.BlockSpec(a.shape, lambda bi, ci: (0,) * a.ndim)
    return pl.pallas_call(
        functools.partial(_ssd_prompt_kernel, L=L, n_sub=n_sub),
        grid=(b, t // rows),
        in_specs=[row_spec(SSD_INNER), row_spec(SSD_CONV_DIM - SSD_INNER),
                  pl.BlockSpec((None, SSD_HEADS, rows), lambda bi, ci: (bi, 0, ci)),
                  row_spec(SSD_INNER), full(alog_col), full(dsk_row), full(nw_row)],
        out_specs=[row_spec(SSD_INNER),
                   pl.BlockSpec((None, SSD_INNER, SSD_STATE), lambda bi, ci: (bi, 0, 0))],
        out_shape=[jax.ShapeDtypeStruct((b, t, SSD_INNER), BF16),
                   jax.ShapeDtypeStruct((b, SSD_INNER, SSD_STATE), F32)],
        scratch_shapes=[pltpu.VMEM((SSD_INNER, SSD_STATE), F32)],
        compiler_params=_cparams(2), name="ssd_prompt",
    )(xs, bc, dt_t, z, alog_col, dsk_row, nw_row)


def _ffn_kernel(*refs, tm, decode, mix_out):
    if mix_out:
        (att_ref, y_ref, gt0_ref, gp0_ref, wa_ref, wy_ref), refs = refs[:6], refs[6:]
    if decode:
        (x_ref, sh_ref, sc_ref, gt_ref, g_ref, gp_ref, up_ref, cw_ref, cb_ref, dn_ref, p0_ref, p1_ref,
         o_ref, u_ref, ubuf0_ref, ubuf1_ref, acc_ref, hb_ref) = refs
    else:
        (x_ref, sh_ref, sc_ref, gt_ref, g_ref, gp_ref, up_ref, cw_ref, cb_ref, dn_ref,
         o_ref, cst_ref, carry_ref, ubuf0_ref, ubuf1_ref, acc_ref, hb_ref) = refs
        t = pl.program_id(1)

        @pl.when(t == 0)
        def _():
            carry_ref[...] = jnp.zeros(carry_ref.shape, F32)

    x = x_ref[...]
    if mix_out:
        mixed = (jnp.dot(att_ref[...].astype(BF16), wa_ref[...], preferred_element_type=F32)
                 + jnp.dot(y_ref[...].astype(BF16), wy_ref[...], preferred_element_type=F32))
        x = x + gt0_ref[...] * _rms(mixed, gp0_ref[...])
        o_ref[...] = x
    hb_ref[...] = _norm_mod(x, g_ref[...], sh_ref[...], sc_ref[...]).astype(BF16)
    acc_ref[...] = jnp.zeros(acc_ref.shape, F32)
    fc = FFN_CHUNK
    n_chunks = D_FF // fc
    ubufs = (ubuf0_ref, ubuf1_ref)

    def col_offset(j, half):
        return pl.multiple_of(half * D_FF + j * fc, LANES)

    def up_project(j, ubuf):
        hb = hb_ref[...]
        for half in range(2):
            off, lo = col_offset(j, half), half * fc
            raw = jnp.dot(hb, up_ref[:, pl.ds(off, fc)], preferred_element_type=F32)
            ubuf[SUBLANES:SUBLANES + tm, lo:lo + fc] = raw
            if decode:
                u_ref[:, pl.ds(off, fc)] = raw
            else:
                ubuf[0:SUBLANES, lo:lo + fc] = carry_ref[:, pl.ds(off, fc)]
                carry_ref[:, pl.ds(off, fc)] = raw[tm - SUBLANES:tm, :]

    def conv_gate_down(j, ubuf):
        halves = []
        for half in range(2):
            off, lo = col_offset(j, half), half * fc
            cw = cw_ref[:, pl.ds(off, fc)]
            conv = cb_ref[:, pl.ds(off, fc)] + cw[FFN_CONV - 1:FFN_CONV, :] * ubuf[SUBLANES:SUBLANES + tm, lo:lo + fc]
            if decode:
                conv = conv + cw[1:2, :] * p1_ref[:, pl.ds(off, fc)] + cw[0:1, :] * p0_ref[:, pl.ds(off, fc)]
            else:
                for i in range(1, FFN_CONV):
                    conv = conv + (cw[FFN_CONV - 1 - i:FFN_CONV - i, :]
                                   * ubuf[SUBLANES - i:SUBLANES - i + tm, lo:lo + fc])
            halves.append(conv)
        act = _silu(halves[0]) * halves[1]
        r0 = pl.multiple_of(j * fc, fc)
        acc_ref[...] += jnp.dot(act.astype(BF16), dn_ref[pl.ds(r0, fc), :], preferred_element_type=F32)

    up_project(0, ubufs[0])

    def pair(i, _):
        up_project(2 * i + 1, ubufs[1])
        conv_gate_down(2 * i, ubufs[0])
        up_project(2 * i + 2, ubufs[0])
        conv_gate_down(2 * i + 1, ubufs[1])
        return 0

    assert n_chunks % 2 == 1
    lax.fori_loop(0, n_chunks // 2, pair, 0)
    conv_gate_down(n_chunks - 1, ubufs[0])
    resid = o_ref[...] if mix_out else x_ref[...]
    o_ref[...] = resid + gt_ref[...] * _rms(acc_ref[...], gp_ref[...])
    if not decode:
        cst_ref[...] = carry_ref[...]


def _ffn_call(x, shift, scale, gate, g, gp, up_w, conv_w, conv_b, dn_w, prev=None, mix=None):
    b, t, d = x.shape
    decode = prev is not None
    tm = min(FFN_ROW_TILE, t)
    row_spec = lambda c: pl.BlockSpec((None, tm, c), lambda bi, ti: (bi, ti, 0))
    mod_spec = (pl.BlockSpec((None, tm, d), lambda bi, ti: (bi, ti, 0)) if decode
                else pl.BlockSpec((None, 1, d), lambda bi, ti: (bi, 0, 0)))
    full = lambda a: pl.BlockSpec(a.shape, lambda bi, ti: (0,) * a.ndim)
    in_specs = [row_spec(d), mod_spec, mod_spec, mod_spec, full(g), full(gp), full(up_w), full(conv_w),
                full(conv_b), full(dn_w)]
    args = [x, shift, scale, gate, g, gp, up_w, conv_w, conv_b, dn_w]
    if mix is not None:
        att, y, gate0, gp0, wa, wy = mix
        in_specs = [row_spec(ATT_WIDTH), row_spec(SSD_INNER), mod_spec, full(gp0), full(wa), full(wy)] + in_specs
        args = [att, y, gate0, gp0, wa, wy] + args
    scratch = [pltpu.VMEM((tm + SUBLANES, 2 * FFN_CHUNK), F32)] * 2 + [pltpu.VMEM((tm, d), F32), pltpu.VMEM((tm, d), BF16)]
    if decode:
        assert b == 1 and t == tm
        in_specs += [pl.BlockSpec((tm, 2 * D_FF), lambda bi, ti: (ti, 0))] * 2
        args += list(prev)
        out_specs = [row_spec(d), pl.BlockSpec((tm, 2 * D_FF), lambda bi, ti: (ti, 0))]
        out_shape = [jax.ShapeDtypeStruct((b, t, d), F32), jax.ShapeDtypeStruct((t, 2 * D_FF), F32)]
    else:
        out_specs = [row_spec(d), pl.BlockSpec((None, SUBLANES, 2 * D_FF), lambda bi, ti: (bi, 0, 0))]
        out_shape = [jax.ShapeDtypeStruct((b, t, d), F32), jax.ShapeDtypeStruct((b, SUBLANES, 2 * D_FF), F32)]
        scratch = [pltpu.VMEM((SUBLANES, 2 * D_FF), F32)] + scratch
    return pl.pallas_call(
        functools.partial(_ffn_kernel, tm=tm, decode=decode, mix_out=mix is not None),
        grid=(b, t // tm), in_specs=in_specs, out_specs=out_specs, out_shape=out_shape,
        scratch_shapes=scratch, compiler_params=_cparams(2),
        name=("conv_ffn_sample" if decode else "conv_ffn_prompt") + ("_mix" if mix is not None else ""),
    )(*args)


def _pool_kernel(*refs, tm, decode, pos0):
    if decode:
        (x_ref, sh_ref, sc_ref, gt_ref, g_ref, gp_ref, pw_ref, pb_ref, ps_ref, prev_ref, o_ref, h_ref) = refs
    else:
        (x_ref, sh_ref, sc_ref, gt_ref, g_ref, gp_ref, pw_ref, pb_ref, ps_ref, o_ref, last_ref, buf_ref) = refs
        t = pl.program_id(1)

        @pl.when(t == 0)
        def _():
            buf_ref[0:POOL_MAX, :] = jnp.zeros((POOL_MAX, D_MODEL), F32)

    x = x_ref[...]
    h = _norm_mod(x, g_ref[...], sh_ref[...], sc_ref[...])
    if decode:
        h_ref[...] = h
    else:
        buf_ref[POOL_MAX:POOL_MAX + tm, :] = h
        row = lax.broadcasted_iota(jnp.int32, (tm, 1), 0)
        pos = pos0 + t * tm + row
    ys = []
    if not decode:
        level = buf_ref[...]
        doubled, width, first_lane = [], 1, 0
        for gi, win in enumerate(POOL_WINDOWS):
            lo = gi * POOL_GROUP_DIM
            level = level[:, lo - first_lane:]
            first_lane = lo
            while width < win:
                level = level + pltpu.roll(level, width, 0)
                width *= 2
            doubled.append(level[POOL_MAX:POOL_MAX + tm, 0:POOL_GROUP_DIM])
    for gi, win in enumerate(POOL_WINDOWS):
        lo = gi * POOL_GROUP_DIM
        hg = h[:, lo:lo + POOL_GROUP_DIM]
        if decode:
            wsum = hg
            for i in range(1, win):
                wsum = wsum + prev_ref[POOL_MAX - 1 - i, :, lo:lo + POOL_GROUP_DIM]
        else:
            wsum = doubled[gi]
        if decode:
            cnt = float(min(pos0 + 1, win))
        else:
            cnt = jnp.minimum(pos + 1, win).astype(F32)
        pooled = wsum / cnt - hg
        ys.append(jnp.dot(pooled.astype(BF16), pw_ref[gi], preferred_element_type=F32))
    y = (jnp.concatenate(ys, axis=1) + pb_ref[...]) * ps_ref[...]
    o_ref[...] = x + gt_ref[...] * _rms(y, gp_ref[...])
    if not decode:
        last_ref[...] = h[tm - POOL_MAX:tm, :]
        buf_ref[0:POOL_MAX, :] = h[tm - POOL_MAX:tm, :]


def _pool_call(x, shift, scale, gate, g, gp, pw, pb, ps, prev_t=None, pos0=0):
    b, t, d = x.shape
    decode = prev_t is not None
    tm = min(ROW_TILE, t)
    row_spec = lambda c: pl.BlockSpec((None, tm, c), lambda bi, ti: (bi, ti, 0))
    mod_spec = (pl.BlockSpec((None, tm, d), lambda bi, ti: (bi, ti, 0)) if decode
                else pl.BlockSpec((None, 1, d), lambda bi, ti: (bi, 0, 0)))
    full = lambda a: pl.BlockSpec(a.shape, lambda bi, ti: (0,) * a.ndim)
    in_specs = [row_spec(d), mod_spec, mod_spec, mod_spec, full(g), full(gp), full(pw), full(pb), full(ps)]
    args = [x, shift, scale, gate, g, gp, pw, pb, ps]
    if decode:
        assert b == 1 and t == tm
        in_specs.append(full(prev_t))
        args.append(prev_t)
        out_specs = [row_spec(d), row_spec(d)]
        out_shape = [jax.ShapeDtypeStruct((b, t, d), F32)] * 2
        scratch = []
    else:
        out_specs = [row_spec(d), pl.BlockSpec((None, POOL_MAX, d), lambda bi, ti: (bi, 0, 0))]
        out_shape = [jax.ShapeDtypeStruct((b, t, d), F32), jax.ShapeDtypeStruct((b, POOL_MAX, d), F32)]
        scratch = [pltpu.VMEM((tm + POOL_MAX, d), F32)]
    return pl.pallas_call(
        functools.partial(_pool_kernel, tm=tm, decode=decode, pos0=pos0),
        grid=(b, t // tm), in_specs=in_specs, out_specs=out_specs, out_shape=out_shape,
        scratch_shapes=scratch, compiler_params=_cparams(2),
        name="pool_sample" if decode else "pool_prompt",
    )(*args)


def _mixin_sample_kernel(x_ref, sh_ref, sc_ref, g_ref, w_ref, wdtx_ref, dtbx_ref, cw_ref, cb_ref, prev_ref,
                         q_ref, k_ref, v_ref, z_ref, xs_ref, bc_ref, dtx_ref, raw_ref, *, pos0):
    h = _norm_mod(x_ref[...], g_ref[...], sh_ref[...], sc_ref[...])
    proj = jnp.dot(h.astype(BF16), w_ref[...], preferred_element_type=F32)
    n = h.shape[0]
    cos_t, sin_t, low = _rope_tables(jnp.float32(pos0), (n, LANES))
    for gi in range(ATT_WIDTH // LANES):
        lo = gi * LANES
        q_ref[:, lo:lo + LANES] = _rope_group(proj[:, lo:lo + LANES], cos_t, sin_t, low)
        k_ref[:, lo:lo + LANES] = _rope_group(proj[:, ATT_WIDTH + lo:ATT_WIDTH + lo + LANES], cos_t, sin_t, low)
    v_ref[...] = proj[:, 2 * ATT_WIDTH:3 * ATT_WIDTH]
    z_ref[...] = proj[:, 3 * ATT_WIDTH:3 * ATT_WIDTH + SSD_INNER]
    raw = proj[:, 3 * ATT_WIDTH + SSD_INNER:MAIN_COLS]
    raw_ref[...] = raw
    conv = cb_ref[...] + cw_ref[SSD_CONV - 1:SSD_CONV, :] * raw
    for i in range(SSD_CONV - 1):
        conv = conv + cw_ref[i:i + 1, :] * prev_ref[i]
    act = _silu(conv)
    xs_ref[...] = act[:, :SSD_INNER]
    bc_ref[...] = act[:, SSD_INNER:]
    dtx_ref[...] = _softplus(jnp.dot(h, wdtx_ref[...], preferred_element_type=F32, precision=HIGHEST) + dtbx_ref[...])


def _mixin_sample_call(x, shift, scale, g, w_main, wdt_x, dtb_x, conv_w, conv_b, prev_t, pos0):
    n, d = x.shape
    full = lambda a: pl.BlockSpec(a.shape, lambda i: (0,) * a.ndim)
    args = [x, shift, scale, g, w_main, wdt_x, dtb_x, conv_w, conv_b, prev_t]
    widths = [ATT_WIDTH] * 3 + [SSD_INNER] * 2 + [SSD_CONV_DIM - SSD_INNER, SSD_INNER, SSD_CONV_DIM]
    return pl.pallas_call(
        functools.partial(_mixin_sample_kernel, pos0=pos0),
        grid=(1,),
        in_specs=[full(a) for a in args],
        out_specs=[pl.BlockSpec((n, c), lambda i: (0, 0)) for c in widths],
        out_shape=[jax.ShapeDtypeStruct((n, c), F32) for c in widths],
        compiler_params=_cparams(1), name="mixin_sample",
    )(*args)


_PAGES_PER_BLOCK = MOBA_BLOCK // PAGE_SIZE
_SEL_PAGES = MOBA_TOPK * _PAGES_PER_BLOCK
SAMPLE_HEADS_PER_STEP = 8


def _token_minor(cache):
    return jnp.transpose(cache, (0, 1, 3, 4, 2))


def _past_kmean_kernel(pt_ref, *refs):
    pages, o_ref = refs[:PAGES_PER_STEP], refs[PAGES_PER_STEP]
    step = pl.program_id(1)
    blocks_per_step = PAGES_PER_STEP // _PAGES_PER_BLOCK

    @pl.when(step == 0)
    def _():
        o_ref[...] = jnp.zeros(o_ref.shape, F32)

    acc = o_ref[...]
    lane = lax.broadcasted_iota(jnp.int32, acc.shape, 2)
    for blk in range(blocks_per_step):
        tot = pages[blk * _PAGES_PER_BLOCK][...]
        for r in range(1, _PAGES_PER_BLOCK):
            tot = tot + pages[blk * _PAGES_PER_BLOCK + r][...]
        mean = jnp.sum(tot, axis=-1, keepdims=True) * (1.0 / MOBA_BLOCK)
        acc = jnp.where(lane == step * blocks_per_step + blk, mean, acc)
    o_ref[...] = acc


def _past_kmean_call(pt_flat, cache_kt, n_seq, n_pages):
    steps = n_pages // PAGES_PER_STEP
    nb = n_pages // _PAGES_PER_BLOCK

    def page_map(b, s, pt, *, r):
        return (0, pt[b * n_pages + s * PAGES_PER_STEP + r], 0, 0, 0)

    grid_spec = pltpu.PrefetchScalarGridSpec(
        num_scalar_prefetch=1, grid=(n_seq, steps),
        in_specs=[pl.BlockSpec((None, None, ATT_HEADS, HEAD_DIM, PAGE_SIZE), functools.partial(page_map, r=r))
                  for r in range(PAGES_PER_STEP)],
        out_specs=pl.BlockSpec((None, ATT_HEADS, HEAD_DIM, nb), lambda b, s, pt: (b, 0, 0, 0)))
    return pl.pallas_call(
        _past_kmean_kernel, grid_spec=grid_spec,
        out_shape=jax.ShapeDtypeStruct((n_seq, ATT_HEADS, HEAD_DIM, nb), F32),
        compiler_params=_cparams(2), name="past_kmean",
    )(pt_flat, *([cache_kt] * PAGES_PER_STEP))


def _sample_select_kernel(q_ref, km_ref, sel_ref, *, jown):
    gate = jnp.sum(km_ref[...] * q_ref[...], axis=1, keepdims=True)
    ids = lax.broadcasted_iota(jnp.int32, gate.shape, 2).astype(F32)
    gate = jnp.where(ids < float(jown), gate, NEG_INF)
    picks = _topk_first(gate, ids, jown, 2)
    for r in range(MOBA_TOPK):
        sel_ref[r] = picks[r].astype(jnp.int32)


def _sample_select_call(q4, kmean, jown):
    n, _, _, nb = kmean.shape
    return pl.pallas_call(
        functools.partial(_sample_select_kernel, jown=jown),
        grid=(n,),
        in_specs=[pl.BlockSpec((None, ATT_HEADS, HEAD_DIM, 1), lambda b: (b, 0, 0, 0)),
                  pl.BlockSpec((None, ATT_HEADS, HEAD_DIM, nb), lambda b: (b, 0, 0, 0))],
        out_specs=pl.BlockSpec((None, MOBA_TOPK, ATT_HEADS, 1, 1), lambda b: (b, 0, 0, 0, 0)),
        out_shape=jax.ShapeDtypeStruct((n, MOBA_TOPK, ATT_HEADS, 1, 1), jnp.int32),
        compiler_params=_cparams(1), name="sample_select",
    )(q4, kmean)


def _sample_attn_kernel(pt_ref, sel_ref, q_ref, kn_ref, vn_ref, *refs):
    n_pg = SAMPLE_HEADS_PER_STEP * _SEL_PAGES
    kp, vp, o_ref = refs[:n_pg], refs[n_pg:2 * n_pg], refs[2 * n_pg]
    b, grp = pl.program_id(0), pl.program_id(1)
    scale = 1.0 / math.sqrt(HEAD_DIM)
    for hh in range(SAMPLE_HEADS_PER_STEP):
        h = grp * SAMPLE_HEADS_PER_STEP + hh
        q = q_ref[hh] * scale
        s_self = jnp.sum(q * kn_ref[hh], axis=0, keepdims=True)
        scores = []
        for r in range(MOBA_TOPK):
            valid = sel_ref[(b * MOBA_TOPK + r) * ATT_HEADS + h] >= 0
            for pg in range(_PAGES_PER_BLOCK):
                s = jnp.sum(kp[hh * _SEL_PAGES + r * _PAGES_PER_BLOCK + pg][...] * q, axis=0, keepdims=True)
                scores.append(jnp.where(valid, s, NEG_INF))
        m = s_self
        for s in scores:
            m = jnp.maximum(m, jnp.max(s, axis=1, keepdims=True))
        p_self = jnp.exp(s_self - m)
        l = p_self
        acc = p_self * vn_ref[hh]
        for idx, s in enumerate(scores):
            p = jnp.exp(s - m)
            l = l + jnp.sum(p, axis=1, keepdims=True)
            acc = acc + jnp.sum(vp[hh * _SEL_PAGES + idx][...] * p, axis=1, keepdims=True)
        o_ref[hh] = acc / l


def _sample_attn_call(pages_flat, sel_flat, q4, k4, v4, cache_kt, cache_vt):
    n = q4.shape[0]
    hps = SAMPLE_HEADS_PER_STEP
    head_spec = pl.BlockSpec((None, hps, HEAD_DIM, 1), lambda b, g, pt, sel: (b, g, 0, 0))

    def page_map(b, g, pages, sel, *, hh, r, pg):
        h = g * hps + hh
        return (0, pages[((b * MOBA_TOPK + r) * ATT_HEADS + h) * _PAGES_PER_BLOCK + pg], h, 0, 0)

    page_specs = [pl.BlockSpec((None, None, None, HEAD_DIM, PAGE_SIZE), functools.partial(page_map, hh=hh, r=r, pg=pg))
                  for hh in range(hps) for r in range(MOBA_TOPK) for pg in range(_PAGES_PER_BLOCK)]
    grid_spec = pltpu.PrefetchScalarGridSpec(
        num_scalar_prefetch=2, grid=(n, ATT_HEADS // hps),
        in_specs=[head_spec, head_spec, head_spec] + page_specs + page_specs,
        out_specs=head_spec)
    n_pg = hps * _SEL_PAGES
    return pl.pallas_call(
        _sample_attn_kernel, grid_spec=grid_spec,
        out_shape=jax.ShapeDtypeStruct((n, ATT_HEADS, HEAD_DIM, 1), F32),
        compiler_params=_cparams(2), name="sample_attn",
    )(pages_flat, sel_flat, q4, k4, v4, *([cache_kt] * n_pg), *([cache_vt] * n_pg))


def _ssd_sample_kernel(xs_ref, bc_ref, dtx_ref, z_ref, st_ref, alogx_ref, dsk_ref, nw_ref, y_ref, sto_ref):
    xs, dtx = xs_ref[...], dtx_ref[...]
    decay = jnp.exp(dtx * (-jnp.exp(alogx_ref[...])))
    xdt = xs * dtx
    bc = bc_ref[...]
    rows_per_group = SSD_INNER // SSD_GROUPS
    gs = SSD_GROUPS * SSD_STATE
    eye = (lax.broadcasted_iota(jnp.int32, (rows_per_group, rows_per_group), 0)
           == lax.broadcasted_iota(jnp.int32, (rows_per_group, rows_per_group), 1))

    def to_col(v):
        return jnp.sum(jnp.where(eye, jnp.broadcast_to(v, eye.shape), 0.0), axis=1, keepdims=True)

    def to_row(v):
        return jnp.sum(jnp.where(eye, jnp.broadcast_to(v, eye.shape), 0.0), axis=0, keepdims=True)

    ys = []
    for g in range(SSD_GROUPS):
        lo = g * rows_per_group
        b_row = bc[:, g * SSD_STATE:(g + 1) * SSD_STATE]
        c_row = bc[:, gs + g * SSD_STATE:gs + (g + 1) * SSD_STATE]
        hg = st_ref[lo:lo + rows_per_group, :]
        hnew = to_col(decay[:, lo:lo + rows_per_group]) * hg + to_col(xdt[:, lo:lo + rows_per_group]) * b_row
        sto_ref[lo:lo + rows_per_group, :] = hnew
        ys.append(to_row(jnp.sum(hnew * c_row, axis=1, keepdims=True)))
    y = jnp.concatenate(ys, axis=1)
    y_ref[...] = _gated_norm(y, xs, z_ref[...], dsk_ref[...], nw_ref[...])


def _ssd_sample_call(xs, bc, dtx, z, state, alog_x, dsk_row, nw_row):
    n = xs.shape[0]
    r3 = lambda a: a.reshape(n, 1, a.shape[-1])
    row_spec = lambda c: pl.BlockSpec((None, 1, c), lambda b: (b, 0, 0))
    st_spec = pl.BlockSpec((None, SSD_INNER, SSD_STATE), lambda b: (b, 0, 0))
    full = lambda a: pl.BlockSpec(a.shape, lambda b: (0,) * a.ndim)
    y, st = pl.pallas_call(
        _ssd_sample_kernel,
        grid=(n,),
        in_specs=[row_spec(SSD_INNER), row_spec(SSD_CONV_DIM - SSD_INNER), row_spec(SSD_INNER), row_spec(SSD_INNER),
                  st_spec, full(alog_x), full(dsk_row), full(nw_row)],
        out_specs=[row_spec(SSD_INNER), st_spec],
        out_shape=[jax.ShapeDtypeStruct((n, 1, SSD_INNER), F32), jax.ShapeDtypeStruct((n, SSD_INNER, SSD_STATE), F32)],
        compiler_params=_cparams(1), name="ssd_sample",
    )(r3(xs), r3(bc), r3(dtx), r3(z), state, alog_x, dsk_row, nw_row)
    return y.reshape(n, SSD_INNER), st


def kernel(x_prompt, x_sample, cache_k, cache_v, state_ssm, state_ssd_conv, state_pool, state_ffn_conv, page_table,
           c_prompt, c_sample, ada_w, ada_b, norm_pre, norm_post, mix_in_w, mix_out_w, ssd_conv_w, ssd_conv_b,
           ssd_dt_bias, ssd_a_log, ssd_d, ssd_norm_w, pool_w, pool_b, pool_scale, ffn_up_w, ffn_conv_w,
           ffn_conv_b, ffn_down_w):
    bp, t, d = x_prompt.shape
    ns = x_sample.shape[0]
    n_pages = page_table.shape[1]
    past_len = n_pages * cache_k.shape[2]
    assert x_sample.shape[1] == 1 and cache_k.shape[2] == PAGE_SIZE and past_len % MOBA_BLOCK == 0
    assert t % ROW_TILE == 0 or t < ROW_TILE

    w_main = mix_in_w[0, :, :MAIN_COLS].astype(BF16)
    w_dt = mix_in_w[0, :, MAIN_COLS:]
    wdt_t = w_dt.T
    wdt_x = jnp.repeat(w_dt, SSD_HEADDIM, axis=1)
    dtb_col = ssd_dt_bias[0].reshape(SSD_HEADS, 1)
    dtb_x = jnp.repeat(ssd_dt_bias[0], SSD_HEADDIM).reshape(1, SSD_INNER)
    alog_col = ssd_a_log[0].reshape(SSD_HEADS, 1)
    alog_x = jnp.repeat(ssd_a_log[0], SSD_HEADDIM).reshape(1, SSD_INNER)
    dsk_row = jnp.repeat(ssd_d[0], SSD_HEADDIM).reshape(1, SSD_INNER)
    nw_row = ssd_norm_w[0].reshape(1, SSD_INNER)
    w_att = mix_out_w[0, :ATT_WIDTH].astype(BF16)
    w_ssd = mix_out_w[0, ATT_WIDTH:].astype(BF16)
    conv_w0 = ssd_conv_w[0]
    conv_b0 = ssd_conv_b[0].reshape(1, SSD_CONV_DIM)
    up_w = ffn_up_w.astype(BF16)
    dn_w = ffn_down_w.astype(BF16)
    pw = pool_w[0].astype(BF16)
    pb = pool_b[0].reshape(1, d)
    ps = pool_scale[0].reshape(1, d)

    mods = _ada_call(jnp.concatenate([c_prompt, c_sample], axis=0), ada_w, ada_b)

    def mod(layer, slot, sample):
        m = mods[layer * 2 + slot]
        m = m[bp:] if sample else m[:bp]
        parts = (m[:, :d], m[:, d:2 * d], m[:, 2 * d:])
        return tuple(p[None] for p in parts) if sample else tuple(p[:, None, :] for p in parts)

    npre = lambda layer, slot: norm_pre[layer, slot].reshape(1, d)
    npost = lambda layer, slot: norm_post[layer, slot].reshape(1, d)

    def ffn(x, layer, sample, prev=None, mix=None):
        sh, sc, gt = mod(layer, 1, sample)
        return _ffn_call(x, sh, sc, gt, npre(layer, 1), npost(layer, 1), up_w[layer], ffn_conv_w[layer],
                         ffn_conv_b[layer].reshape(1, 2 * D_FF), dn_w[layer], prev=prev, mix=mix)

    sh, sc, gt = mod(0, 0, False)
    q, k, v, z, xs, bc, dt_t, kmean, cst = _mixin_prompt_call(
        x_prompt, sh, sc, npre(0, 0), w_main, wdt_t, dtb_col, conv_w0, conv_b0)
    att = _attn_prompt_call(q, k, v, kmean)
    y, ssm_p = _ssd_prompt_call(xs, bc, dt_t, z, alog_col, dsk_row, nw_row)
    xp, fst0 = ffn(x_prompt, 0, False, mix=(att, y, gt, npost(0, 0), w_att, w_ssd))
    sh, sc, gt = mod(1, 0, False)
    xp, plast = _pool_call(xp, sh, sc, gt, npre(1, 0), npost(1, 0), pw, pb, ps)
    xp, fst1 = ffn(xp, 1, False)

    k_p = k.reshape(1, bp, t, ATT_HEADS, HEAD_DIM)
    v_p = v.reshape(1, bp, t, ATT_HEADS, HEAD_DIM)
    ssm_p = ssm_p.reshape(1, bp, SSD_HEADS, SSD_HEADDIM, SSD_STATE)
    sconv_p = cst[None, :, SUBLANES - (SSD_CONV - 1):, :]
    pool_p = plast[None, :, 1:, :]
    fconv_p = jnp.stack([fst0, fst1])[:, :, SUBLANES - (FFN_CONV - 1):, :]

    xs0 = x_sample.reshape(ns, d)
    sh, sc, gt = mod(0, 0, True)
    conv_prev_t = jnp.transpose(state_ssd_conv[0], (1, 0, 2))
    qs, ks, vs, zs, xss, bcs, dtx, raw = _mixin_sample_call(
        xs0, sh[0], sc[0], npre(0, 0), w_main, wdt_x, dtb_x, conv_w0, conv_b0, conv_prev_t, past_len)
    pt_flat = page_table.reshape(-1)
    cache_kt, cache_vt = _token_minor(cache_k), _token_minor(cache_v)
    kmean_s = _past_kmean_call(pt_flat, cache_kt, ns, n_pages)
    jown = past_len // MOBA_BLOCK
    as_cols = lambda a: a.reshape(ns, ATT_HEADS, HEAD_DIM, 1)
    sel = _sample_select_call(as_cols(qs), kmean_s, jown)
    sel3 = sel.reshape(ns, MOBA_TOPK, ATT_HEADS)
    page_pos = jnp.maximum(sel3, 0)[..., None] * _PAGES_PER_BLOCK + jnp.arange(_PAGES_PER_BLOCK, dtype=jnp.int32)
    sel_pages = jnp.take_along_axis(page_table, page_pos.reshape(ns, -1), axis=1)
    att_s = _sample_attn_call(sel_pages.reshape(-1), sel3.reshape(-1), as_cols(qs), as_cols(ks), as_cols(vs),
                              cache_kt, cache_vt).reshape(ns, ATT_WIDTH)
    y_s, ssm_s = _ssd_sample_call(xss, bcs, dtx, zs, state_ssm[0].reshape(ns, SSD_INNER, SSD_STATE),
                                  alog_x, dsk_row, nw_row)
    xq, u0 = ffn(xs0[None], 0, True, prev=(state_ffn_conv[0, :, 0], state_ffn_conv[0, :, 1]),
                 mix=(att_s[None], y_s[None], gt, npost(0, 0), w_att, w_ssd))
    sh, sc, gt = mod(1, 0, True)
    pool_prev_t = jnp.transpose(state_pool[0], (1, 0, 2))
    xq, h_pool = _pool_call(xq, sh, sc, gt, npre(1, 0), npost(1, 0), pw, pb, ps, prev_t=pool_prev_t, pos0=past_len)
    xq, u1 = ffn(xq, 1, True, prev=(state_ffn_conv[1, :, 0], state_ffn_conv[1, :, 1]))

    y_sample = xq.reshape(ns, 1, d)
    k_s = ks.reshape(1, ns, 1, ATT_HEADS, HEAD_DIM)
    v_s = vs.reshape(1, ns, 1, ATT_HEADS, HEAD_DIM)
    ssm_s = ssm_s.reshape(1, ns, SSD_HEADS, SSD_HEADDIM, SSD_STATE)
    sconv_s = jnp.concatenate([state_ssd_conv[0, :, 1:], raw[:, None, :]], axis=1)[None]
    pool_s = jnp.concatenate([state_pool[0, :, 1:], h_pool[0][:, None, :]], axis=1)[None]
    fconv_s = jnp.stack([jnp.stack([state_ffn_conv[0, :, 1], u0], axis=1),
                         jnp.stack([state_ffn_conv[1, :, 1], u1], axis=1)])
    return (xp, y_sample, k_p, v_p, ssm_p, sconv_p, pool_p, fconv_p, k_s, v_s, ssm_s, sconv_s, pool_s, fconv_s)
```

```python
import functools
import math

import jax
import jax.numpy as jnp
from jax import lax
from jax.experimental import pallas as pl
from jax.experimental.pallas import tpu as pltpu

F32 = jnp.float32
BF16 = jnp.bfloat16
HIGHEST = lax.Precision.HIGHEST

D_MODEL = 1024
ATT_HEADS = 8
HEAD_DIM = 64
ATT_WIDTH = ATT_HEADS * HEAD_DIM
ROT_DIM = HEAD_DIM // 4
ROPE_THETA = 500000.0
MOBA_BLOCK = 256
MOBA_TOPK = 3
SSD_HEADS = 8
SSD_HEADDIM = 64
SSD_INNER = SSD_HEADS * SSD_HEADDIM
SSD_GROUPS = 2
SSD_STATE = 128
SSD_CONV = 4
SSD_CONV_DIM = SSD_INNER + 2 * SSD_GROUPS * SSD_STATE
POOL_WINDOWS = (2, 4, 8, 16)
POOL_GROUP_DIM = D_MODEL // len(POOL_WINDOWS)
POOL_MAX = 16
D_FF = 2816
FFN_CONV = 3
EPS = 1e-6
PAGE_SIZE = 128
MAIN_COLS = 3 * ATT_WIDTH + SSD_INNER + SSD_CONV_DIM

LANES = 128
SUBLANES = 8
VMEM_LIMIT_BYTES = 56 * 1024 * 1024

ROW_TILE = 512
FFN_ROW_TILE = 512
ATTN_Q_TILE = 256
SSD_CHUNK = 128
SSD_CHUNKS_PER_STEP = 4
FFN_CHUNK = 256
PAGES_PER_STEP = 32

ATTN_KV_BLOCKS = 4
ATTN_TILES_PER_STEP = 8

NEG_INF = float("-inf")
MASKED = -1e30
LOG2_E = 1.4426950408889634
VT_ROWS = HEAD_DIM + 2 * SUBLANES


def _cparams(n_axes, flags=None):
    return pltpu.CompilerParams(dimension_semantics=("arbitrary",) * n_axes,
                                vmem_limit_bytes=VMEM_LIMIT_BYTES, flags=flags)


def _silu(x):
    return x / (1.0 + jnp.exp(-x))


def _softplus(x):
    return jnp.maximum(x, 0.0) + jnp.log1p(jnp.exp(-jnp.abs(x)))


def _rms(x, g):
    return x * lax.rsqrt(jnp.mean(x * x, axis=-1, keepdims=True) + EPS) * g


def _norm_mod(x, g, shift, scale):
    return _rms(x, g) * (1.0 + scale) + shift


def _dot_nt(a, b, precision=None):
    return lax.dot_general(a, b, (((1,), (1,)), ((), ())), preferred_element_type=F32, precision=precision)


def _dot_tn(a, b):
    return lax.dot_general(a, b, (((0,), (0,)), ((), ())), preferred_element_type=F32)


def _rope_tables(pos, shape):
    lane = lax.broadcasted_iota(jnp.int32, shape, 1)
    j = lane & (HEAD_DIM - 1)
    half = ROT_DIM // 2
    inv = jnp.exp((lane & (half - 1)).astype(F32) * (-2.0 * math.log(ROPE_THETA) / ROT_DIM))
    ang = pos * inv
    c, s = jnp.cos(ang), jnp.sin(ang)
    cos_t = jnp.where(j < ROT_DIM, c, 1.0)
    sin_t = jnp.where(j < half, -s, jnp.where(j < ROT_DIM, s, 0.0))
    return cos_t, sin_t, j < half


def _rope_group(xg, cos_t, sin_t, low):
    half = ROT_DIM // 2
    partner = jnp.where(low, pltpu.roll(xg, LANES - half, 1), pltpu.roll(xg, half, 1))
    return xg * cos_t + partner * sin_t


def _ada_kernel(c_ref, w_ref, b_ref, o_ref):
    o_ref[...] = jnp.dot(_silu(c_ref[...]), w_ref[...], preferred_element_type=F32, precision=HIGHEST) + b_ref[...]


def _ada_call(c_all, ada_w, ada_b):
    nl, ns, d, d3 = ada_w.shape
    n = c_all.shape[0]
    tn = 1024
    return pl.pallas_call(
        _ada_kernel,
        grid=(nl * ns, d3 // tn),
        in_specs=[pl.BlockSpec((n, d), lambda i, j: (0, 0)),
                  pl.BlockSpec((None, None, d, tn), lambda i, j: (i // ns, i % ns, 0, j)),
                  pl.BlockSpec((None, 1, tn), lambda i, j: (i, 0, j))],
        out_specs=pl.BlockSpec((None, n, tn), lambda i, j: (i, 0, j)),
        out_shape=jax.ShapeDtypeStruct((nl * ns, n, d3), F32),
        compiler_params=_cparams(2), name="ada_mod",
    )(c_all, ada_w, ada_b.reshape(nl * ns, 1, d3))


def _mixin_prompt_kernel(x_ref, sh_ref, sc_ref, g_ref, w_ref, wdt_ref, dtb_ref, cw_ref, cb_ref,
                         q_ref, k_ref, v_ref, z_ref, xs_ref, bc_ref, dt_ref, km_ref, cst_ref,
                         buf_ref, oc_ref, os_ref, *, tm):
    t = pl.program_id(1)
    lane = lax.broadcasted_iota(jnp.int32, (1, LANES), 1)
    j = lane & (HEAD_DIM - 1)
    half = ROT_DIM // 2
    inv = jnp.exp((lane & (half - 1)).astype(F32) * (-2.0 * math.log(ROPE_THETA) / ROT_DIM))
    rot = jnp.where(j < ROT_DIM, 1.0, 0.0)
    sgn = jnp.where(j < half, -1.0, rot)
    low = j < half

    @pl.when(t == 0)
    def _():
        buf_ref[0:SUBLANES, :] = jnp.zeros((SUBLANES, SSD_CONV_DIM), F32)
        ang = lax.broadcasted_iota(jnp.int32, (tm, LANES), 0).astype(F32) * inv
        oc_ref[...] = jnp.cos(ang)
        os_ref[...] = jnp.sin(ang)

    h = _norm_mod(x_ref[...], g_ref[...], sh_ref[...], sc_ref[...])
    proj = jnp.dot(h.astype(BF16), w_ref[...], preferred_element_type=F32)

    base = (t * tm).astype(F32) * inv
    cb, sb = jnp.cos(base), jnp.sin(base)
    oc, osn = oc_ref[...], os_ref[...]
    cos_t = (cb * rot) * oc - (sb * rot) * osn + (1.0 - rot)
    sin_t = (sb * sgn) * oc + (cb * sgn) * osn
    for gi in range(ATT_WIDTH // LANES):
        lo = gi * LANES
        q_ref[:, lo:lo + LANES] = _rope_group(proj[:, lo:lo + LANES], cos_t, sin_t, low)
        k_ref[:, lo:lo + LANES] = _rope_group(proj[:, ATT_WIDTH + lo:ATT_WIDTH + lo + LANES], cos_t, sin_t, low)
    v_ref[...] = proj[:, 2 * ATT_WIDTH:3 * ATT_WIDTH]
    z_ref[...] = proj[:, 3 * ATT_WIDTH:3 * ATT_WIDTH + SSD_INNER]
    for blk in range(tm // MOBA_BLOCK):
        kb = k_ref[blk * MOBA_BLOCK:(blk + 1) * MOBA_BLOCK, :]
        km_ref[blk] = jnp.sum(kb, axis=0, keepdims=True) * (1.0 / MOBA_BLOCK)

    raw = proj[:, 3 * ATT_WIDTH + SSD_INNER:MAIN_COLS]
    buf_ref[SUBLANES:SUBLANES + tm, :] = raw
    conv = cb_ref[...] + cw_ref[SSD_CONV - 1:SSD_CONV, :] * raw
    for i in range(1, SSD_CONV):
        conv = conv + cw_ref[SSD_CONV - 1 - i:SSD_CONV - i, :] * buf_ref[SUBLANES - i:SUBLANES - i + tm, :]
    act = _silu(conv)
    xs_ref[...] = act[:, :SSD_INNER]
    bc_ref[...] = act[:, SSD_INNER:].astype(BF16)
    cst_ref[...] = raw[tm - SUBLANES:tm, :]
    buf_ref[0:SUBLANES, :] = raw[tm - SUBLANES:tm, :]

    dt_raw = _dot_nt(wdt_ref[...], h, precision=HIGHEST)
    dt_ref[...] = _softplus(dt_raw + dtb_ref[...])


def _mixin_prompt_call(x, shift, scale, g, w_main, wdt_t, dtb_col, conv_w, conv_b):
    b, t, d = x.shape
    tm = min(ROW_TILE, t)
    nt = t // tm
    nbt = tm // MOBA_BLOCK
    row_spec = lambda c: pl.BlockSpec((None, tm, c), lambda bi, ti: (bi, ti, 0))
    mod_spec = pl.BlockSpec((None, 1, d), lambda bi, ti: (bi, 0, 0))
    full = lambda a: pl.BlockSpec(a.shape, lambda bi, ti: (0,) * a.ndim)
    outs = pl.pallas_call(
        functools.partial(_mixin_prompt_kernel, tm=tm),
        grid=(b, nt),
        in_specs=[row_spec(d), mod_spec, mod_spec, full(g), full(w_main), full(wdt_t), full(dtb_col),
                  full(conv_w), full(conv_b)],
        out_specs=[row_spec(ATT_WIDTH), row_spec(ATT_WIDTH), row_spec(ATT_WIDTH), row_spec(SSD_INNER),
                   row_spec(SSD_INNER), row_spec(SSD_CONV_DIM - SSD_INNER),
                   pl.BlockSpec((None, SSD_HEADS, tm), lambda bi, ti: (bi, 0, ti)),
                   pl.BlockSpec((nbt, 1, ATT_WIDTH), lambda bi, ti: (bi * nt + ti, 0, 0)),
                   pl.BlockSpec((None, SUBLANES, SSD_CONV_DIM), lambda bi, ti: (bi, 0, 0))],
        out_shape=[jax.ShapeDtypeStruct((b, t, ATT_WIDTH), F32)] * 3
        + [jax.ShapeDtypeStruct((b, t, SSD_INNER), F32)] * 2
        + [jax.ShapeDtypeStruct((b, t, SSD_CONV_DIM - SSD_INNER), BF16),
           jax.ShapeDtypeStruct((b, SSD_HEADS, t), F32),
           jax.ShapeDtypeStruct((b * nt * nbt, 1, ATT_WIDTH), F32),
           jax.ShapeDtypeStruct((b, SUBLANES, SSD_CONV_DIM), F32)],
        scratch_shapes=[pltpu.VMEM((tm + SUBLANES, SSD_CONV_DIM), F32), pltpu.VMEM((tm, LANES), F32),
                        pltpu.VMEM((tm, LANES), F32)],
        compiler_params=_cparams(2), name="mixin_prompt",
    )(x, shift, scale, g, w_main, wdt_t, dtb_col, conv_w, conv_b)
    q, k, v, z, xs, bc, dt_t, kmean, cst = outs
    return q, k, v, z, xs, bc, dt_t, kmean.reshape(b, t // MOBA_BLOCK, ATT_WIDTH), cst


def _topk_first(gate, ids, n_valid, axis):
    picks = []
    for r in range(MOBA_TOPK):
        mx = jnp.max(gate, axis=axis, keepdims=True)
        idx = jnp.min(jnp.where(gate == mx, ids, 1e9), axis=axis, keepdims=True)
        idx = jnp.where(r < n_valid, idx, -1.0)
        picks.append(idx)
        gate = jnp.where(ids == idx, NEG_INF, gate)
    return picks


def _attn_prompt_kernel(q_ref, k_ref, v_ref, km_ref, o_ref, ka_scr, vt_scr, acc_scr, *, tq, t):
    i = pl.program_id(2)
    kv = ATTN_KV_BLOCKS * MOBA_BLOCK
    nb = km_ref.shape[0]

    @pl.when(i == 0)
    def _():
        blk_lane = lax.broadcasted_iota(jnp.int32, (MOBA_BLOCK, LANES), 1)
        sub = lax.broadcasted_iota(jnp.int32, (2 * SUBLANES, MOBA_BLOCK), 0)
        ones_rows = jnp.where(sub == 0, 1.0, 0.0).astype(BF16)

        def prep(c, _):
            r0 = pl.multiple_of(c * MOBA_BLOCK, MOBA_BLOCK)
            ka_scr[pl.ds(r0, MOBA_BLOCK), 0:LANES] = k_ref[pl.ds(r0, MOBA_BLOCK), :].astype(BF16)
            ka_scr[pl.ds(r0, MOBA_BLOCK), LANES:2 * LANES] = jnp.where(blk_lane == c, 1.0, 0.0).astype(BF16)
            vt = v_ref[pl.ds(r0, MOBA_BLOCK), :].T.astype(BF16)
            for hh in range(2):
                vt_scr[hh, 0:HEAD_DIM, pl.ds(r0, MOBA_BLOCK)] = vt[hh * HEAD_DIM:(hh + 1) * HEAD_DIM]
                vt_scr[hh, HEAD_DIM:VT_ROWS, pl.ds(r0, MOBA_BLOCK)] = ones_rows
            return 0

        lax.fori_loop(0, t // MOBA_BLOCK, prep, 0)

    n = 2 * tq
    nch = ATTN_TILES_PER_STEP
    km = km_ref[...]
    lane = lax.broadcasted_iota(jnp.int32, (tq, LANES), 1)
    ids = lax.broadcasted_iota(jnp.int32, (nb, n), 0).astype(F32)
    ids_all = lax.broadcasted_iota(jnp.int32, (LANES, n), 0).astype(F32)
    key = lax.broadcasted_iota(jnp.int32, (tq, n), 0)
    qry = lax.broadcasted_iota(jnp.int32, (tq, n), 1) & (tq - 1)
    qas, ms = [], []
    for c in range(nch):
        jown = i * nch + c
        q = q_ref[c * tq:(c + 1) * tq, :]
        qcat = jnp.concatenate([jnp.where(lane < HEAD_DIM, q, 0.0), jnp.where(lane >= HEAD_DIM, q, 0.0)], axis=0)
        gate = _dot_nt(km, qcat, precision=HIGHEST)
        gate = jnp.where(ids < jown.astype(F32), gate, NEG_INF)
        p0, p1, p2 = _topk_first(gate, ids, jown, 0)
        bias = jnp.where((ids_all == p0) | (ids_all == p1) | (ids_all == p2), 0.0, MASKED).T
        qs = qcat * (LOG2_E / math.sqrt(HEAD_DIM))
        qas.append(jnp.concatenate([qs, bias], axis=1).astype(BF16))

        r0 = pl.multiple_of(jown * tq, tq)
        st = jnp.where(key <= qry, _dot_nt(ka_scr[pl.ds(r0, tq), 0:LANES], qs.astype(BF16)), NEG_INF)
        m = jnp.max(st, axis=0, keepdims=True)
        p = jnp.exp2(st - m)
        pb = p.astype(BF16)
        for hh in range(2):
            acc_scr[c, hh] = jnp.dot(vt_scr[hh, :, pl.ds(r0, tq)], pb[:, hh * tq:(hh + 1) * tq],
                                     preferred_element_type=F32)
        ms.append(m)

    def scores(c, c0, rows):
        return _dot_nt(ka_scr[pl.ds(c0, rows), :], qas[c])

    def update(c, m, st, c0, rows):
        m_new = jnp.maximum(m, jnp.max(st, axis=0, keepdims=True))
        pb = jnp.exp2(st - m_new).astype(BF16)
        alpha = jnp.exp2(m - m_new)
        for hh in range(2):
            acc_scr[c, hh] = alpha[:, hh * tq:(hh + 1) * tq] * acc_scr[c, hh] + jnp.dot(
                vt_scr[hh, :, pl.ds(c0, rows)], pb[:, hh * tq:(hh + 1) * tq], preferred_element_type=F32)
        return m_new

    def body(j, ms):
        c0 = pl.multiple_of(j * kv, kv)
        sts = [scores(c, c0, kv) for c in range(nch)]
        return tuple(update(c, ms[c], sts[c], c0, kv) for c in range(nch))

    ms = lax.fori_loop(0, (i * nch) // ATTN_KV_BLOCKS, body, tuple(ms))
    base = pl.multiple_of(i * nch * MOBA_BLOCK, MOBA_BLOCK)
    tails = [scores(c, base, c * MOBA_BLOCK) for c in range(1, nch)]
    for c in range(1, nch):
        update(c, ms[c], tails[c - 1], base, c * MOBA_BLOCK)
    for c in range(nch):
        outs = []
        for hh in range(2):
            acc = acc_scr[c, hh]
            outs.append(acc[0:HEAD_DIM, :] / acc[HEAD_DIM:HEAD_DIM + 1, :])
        o_ref[c * tq:(c + 1) * tq, :] = jnp.concatenate(outs, axis=0).T.astype(BF16)


def _attn_prompt_call(q, k, v, kmean):
    b, t, _ = q.shape
    tq = min(ATTN_Q_TILE, t)
    nb = kmean.shape[1]
    nch = ATTN_TILES_PER_STEP
    assert tq == MOBA_BLOCK and tq & (tq - 1) == 0 and nb <= LANES and t % (ATTN_KV_BLOCKS * MOBA_BLOCK) == 0
    assert t % (nch * tq) == 0 and nch % ATTN_KV_BLOCKS == 0
    npair = ATT_WIDTH // LANES
    return pl.pallas_call(
        functools.partial(_attn_prompt_kernel, tq=tq, t=t),
        scratch_shapes=[pltpu.VMEM((t, 2 * LANES), BF16), pltpu.VMEM((2, VT_ROWS, t), BF16),
                        pltpu.VMEM((nch, 2, VT_ROWS, tq), F32)],
        grid=(b, npair, t // (nch * tq)),
        in_specs=[pl.BlockSpec((None, nch * tq, LANES), lambda bi, hp, i: (bi, i, hp)),
                  pl.BlockSpec((None, t, LANES), lambda bi, hp, i: (bi, 0, hp)),
                  pl.BlockSpec((None, t, LANES), lambda bi, hp, i: (bi, 0, hp)),
                  pl.BlockSpec((None, nb, LANES), lambda bi, hp, i: (bi, 0, hp))],
        out_specs=pl.BlockSpec((None, nch * tq, LANES), lambda bi, hp, i: (bi, i, hp)),
        out_shape=jax.ShapeDtypeStruct((b, t, ATT_WIDTH), BF16),
        compiler_params=_cparams(3), name="moba_prompt",
    )(q, k, v, kmean)


def _gated_norm(y, xs, z, dsk, nw):
    yg = (y + dsk * xs) * _silu(z)
    return _rms(yg, nw)


def _ssd_prompt_kernel(xs_ref, bc_ref, dt_ref, z_ref, alog_ref, dsk_ref, nw_ref, y_ref, st_ref, h_scr, *, L, n_sub):
    c = pl.program_id(1)

    @pl.when(c == 0)
    def _():
        h_scr[...] = jnp.zeros(h_scr.shape, F32)

    a = -jnp.exp(alog_ref[...])
    for sub in range(n_sub):
        _ssd_chunk(sub * L, L, a, xs_ref, bc_ref, dt_ref, z_ref, dsk_ref, nw_ref, y_ref, h_scr)
    st_ref[...] = h_scr[...]


def _ssd_chunk(r0, L, a, xs_ref, bc_ref, dt_ref, z_ref, dsk_ref, nw_ref, y_ref, h_scr):
    dt = dt_ref[:, r0:r0 + L]
    lane = lax.broadcasted_iota(jnp.int32, dt.shape, 1)
    acum = dt * a
    s = 1
    while s < L:
        acum = acum + jnp.where(lane >= s, pltpu.roll(acum, s, 1), 0.0)
        s *= 2
    nh = SSD_HEADS
    e_rows = jnp.exp(acum)
    w_rows = jnp.exp(acum[:, L - 1:L] - acum) * dt
    rows = jnp.concatenate([acum, e_rows, w_rows, jnp.zeros((L - 3 * nh, L), F32)], axis=0)
    cols = rows.T

    xs = xs_ref[r0:r0 + L, :]
    bc = bc_ref[r0:r0 + L, :]
    ri = lax.broadcasted_iota(jnp.int32, (L, L), 0)
    ci = lax.broadcasted_iota(jnp.int32, (L, L), 1)
    causal = ri >= ci
    low = lax.broadcasted_iota(jnp.int32, (L, LANES), 1) < SSD_HEADDIM
    low_rows = lax.broadcasted_iota(jnp.int32, (LANES, SSD_STATE), 0) < SSD_HEADDIM
    gs = SSD_GROUPS * SSD_STATE
    heads_per_group = SSD_HEADS // SSD_GROUPS
    ys = []
    for g in range(SSD_GROUPS):
        bg = bc[:, g * SSD_STATE:(g + 1) * SSD_STATE].astype(BF16)
        cg = bc[:, gs + g * SSD_STATE:gs + (g + 1) * SSD_STATE].astype(BF16)
        cbt = _dot_nt(cg, bg)
        for pr in range(heads_per_group // 2):
            hp = g * (heads_per_group // 2) + pr
            xp = xs[:, hp * LANES:(hp + 1) * LANES]
            xpb = xp.astype(BF16)
            hst = h_scr[hp * LANES:(hp + 1) * LANES, :]
            inter = _dot_nt(cg, hst.astype(BF16))
            intra, ecols, wcols, cds = [], [], [], []
            for hh in range(2):
                hd = 2 * hp + hh
                col = cols[:, hd:hd + 1]
                row = acum[hd:hd + 1, :]
                decay = jnp.exp(jnp.where(causal, col - row, NEG_INF))
                m = cbt * decay * dt[hd:hd + 1, :]
                intra.append(jnp.dot(m.astype(BF16), xpb, preferred_element_type=F32))
                last = acum[hd:hd + 1, L - 1:L]
                ecols.append(cols[:, nh + hd:nh + hd + 1])
                wcols.append(cols[:, 2 * nh + hd:2 * nh + hd + 1])
                cds.append(jnp.exp(last))
            ys.append(jnp.where(low, intra[0], intra[1]) + inter * jnp.where(low, ecols[0], ecols[1]))
            xw = (xp * jnp.where(low, wcols[0], wcols[1])).astype(BF16)
            upd = _dot_tn(xw, bg)
            h_scr[hp * LANES:(hp + 1) * LANES, :] = hst * jnp.where(low_rows, cds[0], cds[1]) + upd
    y = jnp.concatenate(ys, axis=1)
    y_ref[r0:r0 + L, :] = _gated_norm(y, xs, z_ref[r0:r0 + L, :], dsk_ref[...], nw_ref[...]).astype(BF16)


def _ssd_prompt_call(xs, bc, dt_t, z, alog_col, dsk_row, nw_row):
    b, t, _ = xs.shape
    L = SSD_CHUNK
    n_sub = SSD_CHUNKS_PER_STEP
    rows = L * n_sub
    assert t % rows == 0
    row_spec = lambda c: pl.BlockSpec((None, rows, c), lambda bi, ci: (bi, ci, 0))
    full = lambda a: pl.BlockSpec(a.shape, lambda bi, ci: (0,) * a.ndim)
    return pl.pallas_call(
        functools.partial(_ssd_prompt_kernel, L=L, n_sub=n_sub),
        grid=(b, t // rows),
        in_specs=[row_spec(SSD_INNER), row_spec(SSD_CONV_DIM - SSD_INNER),
                  pl.BlockSpec((None, SSD_HEADS, rows), lambda bi, ci: (bi, 0, ci)),
                  row_spec(SSD_INNER), full(alog_col), full(dsk_row), full(nw_row)],
        out_specs=[row_spec(SSD_INNER),
                   pl.BlockSpec((None, SSD_INNER, SSD_STATE), lambda bi, ci: (bi, 0, 0))],
        out_shape=[jax.ShapeDtypeStruct((b, t, SSD_INNER), BF16),
                   jax.ShapeDtypeStruct((b, SSD_INNER, SSD_STATE), F32)],
        scratch_shapes=[pltpu.VMEM((SSD_INNER, SSD_STATE), F32)],
        compiler_params=_cparams(2), name="ssd_prompt",
    )(xs, bc, dt_t, z, alog_col, dsk_row, nw_row)


def _ffn_kernel(*refs, tm, decode, mix_out):
    if mix_out:
        (att_ref, y_ref, gt0_ref, gp0_ref, wa_ref, wy_ref), refs = refs[:6], refs[6:]
    if decode:
        (x_ref, sh_ref, sc_ref, gt_ref, g_ref, gp_ref, up_ref, cw_ref, cb_ref, dn_ref, p0_ref, p1_ref,
         o_ref, u_ref, ubuf0_ref, ubuf1_ref, acc_ref, hb_ref) = refs
    else:
        (x_ref, sh_ref, sc_ref, gt_ref, g_ref, gp_ref, up_ref, cw_ref, cb_ref, dn_ref,
         o_ref, cst_ref, carry_ref, ubuf0_ref, ubuf1_ref, acc_ref, hb_ref) = refs
        t = pl.program_id(1)

        @pl.when(t == 0)
        def _():
            carry_ref[...] = jnp.zeros(carry_ref.shape, F32)

    x = x_ref[...]
    if mix_out:
        mixed = (jnp.dot(att_ref[...].astype(BF16), wa_ref[...], preferred_element_type=F32)
                 + jnp.dot(y_ref[...].astype(BF16), wy_ref[...], preferred_element_type=F32))
        x = x + gt0_ref[...] * _rms(mixed, gp0_ref[...])
        o_ref[...] = x
    hb_ref[...] = _norm_mod(x, g_ref[...], sh_ref[...], sc_ref[...]).astype(BF16)
    acc_ref[...] = jnp.zeros(acc_ref.shape, F32)
    fc = FFN_CHUNK
    n_chunks = D_FF // fc
    ubufs = (ubuf0_ref, ubuf1_ref)

    def col_offset(j, half):
        return pl.multiple_of(half * D_FF + j * fc, LANES)

    def up_project(j, ubuf):
        hb = hb_ref[...]
        for half in range(2):
            off, lo = col_offset(j, half), half * fc
            raw = jnp.dot(hb, up_ref[:, pl.ds(off, fc)], preferred_element_type=F32)
            ubuf[SUBLANES:SUBLANES + tm, lo:lo + fc] = raw
            if decode:
                u_ref[:, pl.ds(off, fc)] = raw
            else:
                ubuf[0:SUBLANES, lo:lo + fc] = carry_ref[:, pl.ds(off, fc)]
                carry_ref[:, pl.ds(off, fc)] = raw[tm - SUBLANES:tm, :]

    def conv_gate_down(j, ubuf):
        halves = []
        for half in range(2):
            off, lo = col_offset(j, half), half * fc
            cw = cw_ref[:, pl.ds(off, fc)]
            conv = cb_ref[:, pl.ds(off, fc)] + cw[FFN_CONV - 1:FFN_CONV, :] * ubuf[SUBLANES:SUBLANES + tm, lo:lo + fc]
            if decode:
                conv = conv + cw[1:2, :] * p1_ref[:, pl.ds(off, fc)] + cw[0:1, :] * p0_ref[:, pl.ds(off, fc)]
            else:
                for i in range(1, FFN_CONV):
                    conv = conv + (cw[FFN_CONV - 1 - i:FFN_CONV - i, :]
                                   * ubuf[SUBLANES - i:SUBLANES - i + tm, lo:lo + fc])
            halves.append(conv)
        act = _silu(halves[0]) * halves[1]
        r0 = pl.multiple_of(j * fc, fc)
        acc_ref[...] += jnp.dot(act.astype(BF16), dn_ref[pl.ds(r0, fc), :], preferred_element_type=F32)

    up_project(0, ubufs[0])

    def pair(i, _):
        up_project(2 * i + 1, ubufs[1])
        conv_gate_down(2 * i, ubufs[0])
        up_project(2 * i + 2, ubufs[0])
        conv_gate_down(2 * i + 1, ubufs[1])
        return 0

    assert n_chunks % 2 == 1
    lax.fori_loop(0, n_chunks // 2, pair, 0)
    conv_gate_down(n_chunks - 1, ubufs[0])
    resid = o_ref[...] if mix_out else x_ref[...]
    o_ref[...] = resid + gt_ref[...] * _rms(acc_ref[...], gp_ref[...])
    if not decode:
        cst_ref[...] = carry_ref[...]


def _ffn_call(x, shift, scale, gate, g, gp, up_w, conv_w, conv_b, dn_w, prev=None, mix=None):
    b, t, d = x.shape
    decode = prev is not None
    tm = min(FFN_ROW_TILE, t)
    row_spec = lambda c: pl.BlockSpec((None, tm, c), lambda bi, ti: (bi, ti, 0))
    mod_spec = (pl.BlockSpec((None, tm, d), lambda bi, ti: (bi, ti, 0)) if decode
                else pl.BlockSpec((None, 1, d), lambda bi, ti: (bi, 0, 0)))
    full = lambda a: pl.BlockSpec(a.shape, lambda bi, ti: (0,) * a.ndim)
    in_specs = [row_spec(d), mod_spec, mod_spec, mod_spec, full(g), full(gp), full(up_w), full(conv_w),
                full(conv_b), full(dn_w)]
    args = [x, shift, scale, gate, g, gp, up_w, conv_w, conv_b, dn_w]
    if mix is not None:
        att, y, gate0, gp0, wa, wy = mix
        in_specs = [row_spec(ATT_WIDTH), row_spec(SSD_INNER), mod_spec, full(gp0), full(wa), full(wy)] + in_specs
        args = [att, y, gate0, gp0, wa, wy] + args
    scratch = [pltpu.VMEM((tm + SUBLANES, 2 * FFN_CHUNK), F32)] * 2 + [pltpu.VMEM((tm, d), F32), pltpu.VMEM((tm, d), BF16)]
    if decode:
        assert b == 1 and t == tm
        in_specs += [pl.BlockSpec((tm, 2 * D_FF), lambda bi, ti: (ti, 0))] * 2
        args += list(prev)
        out_specs = [row_spec(d), pl.BlockSpec((tm, 2 * D_FF), lambda bi, ti: (ti, 0))]
        out_shape = [jax.ShapeDtypeStruct((b, t, d), F32), jax.ShapeDtypeStruct((t, 2 * D_FF), F32)]
    else:
        out_specs = [row_spec(d), pl.BlockSpec((None, SUBLANES, 2 * D_FF), lambda bi, ti: (bi, 0, 0))]
        out_shape = [jax.ShapeDtypeStruct((b, t, d), F32), jax.ShapeDtypeStruct((b, SUBLANES, 2 * D_FF), F32)]
        scratch = [pltpu.VMEM((SUBLANES, 2 * D_FF), F32)] + scratch
    return pl.pallas_call(
        functools.partial(_ffn_kernel, tm=tm, decode=decode, mix_out=mix is not None),
        grid=(b, t // tm), in_specs=in_specs, out_specs=out_specs, out_shape=out_shape,
        scratch_shapes=scratch, compiler_params=_cparams(2),
        name=("conv_ffn_sample" if decode else "conv_ffn_prompt") + ("_mix" if mix is not None else ""),
    )(*args)


def _pool_kernel(*refs, tm, decode, pos0):
    if decode:
        (x_ref, sh_ref, sc_ref, gt_ref, g_ref, gp_ref, pw_ref, pb_ref, ps_ref, prev_ref, o_ref, h_ref) = refs
    else:
        (x_ref, sh_ref, sc_ref, gt_ref, g_ref, gp_ref, pw_ref, pb_ref, ps_ref, o_ref, last_ref, buf_ref) = refs
        t = pl.program_id(1)

        @pl.when(t == 0)
        def _():
            buf_ref[0:POOL_MAX, :] = jnp.zeros((POOL_MAX, D_MODEL), F32)

    x = x_ref[...]
    h = _norm_mod(x, g_ref[...], sh_ref[...], sc_ref[...])
    if decode:
        h_ref[...] = h
    else:
        buf_ref[POOL_MAX:POOL_MAX + tm, :] = h
        row = lax.broadcasted_iota(jnp.int32, (tm, 1), 0)
        pos = pos0 + t * tm + row
    ys = []
    if not decode:
        level = buf_ref[...]
        doubled, width, first_lane = [], 1, 0
        for gi, win in enumerate(POOL_WINDOWS):
            lo = gi * POOL_GROUP_DIM
            level = level[:, lo - first_lane:]
            first_lane = lo
            while width < win:
                level = level + pltpu.roll(level, width, 0)
                width *= 2
            doubled.append(level[POOL_MAX:POOL_MAX + tm, 0:POOL_GROUP_DIM])
    for gi, win in enumerate(POOL_WINDOWS):
        lo = gi * POOL_GROUP_DIM
        hg = h[:, lo:lo + POOL_GROUP_DIM]
        if decode:
            wsum = hg
            for i in range(1, win):
                wsum = wsum + prev_ref[POOL_MAX - 1 - i, :, lo:lo + POOL_GROUP_DIM]
        else:
            wsum = doubled[gi]
        if decode:
            cnt = float(min(pos0 + 1, win))
        else:
            cnt = jnp.minimum(pos + 1, win).astype(F32)
        pooled = wsum / cnt - hg
        ys.append(jnp.dot(pooled.astype(BF16), pw_ref[gi], preferred_element_type=F32))
    y = (jnp.concatenate(ys, axis=1) + pb_ref[...]) * ps_ref[...]
    o_ref[...] = x + gt_ref[...] * _rms(y, gp_ref[...])
    if not decode:
        last_ref[...] = h[tm - POOL_MAX:tm, :]
        buf_ref[0:POOL_MAX, :] = h[tm - POOL_MAX:tm, :]


def _pool_call(x, shift, scale, gate, g, gp, pw, pb, ps, prev_t=None, pos0=0):
    b, t, d = x.shape
    decode = prev_t is not None
    tm = min(ROW_TILE, t)
    row_spec = lambda c: pl.BlockSpec((None, tm, c), lambda bi, ti: (bi, ti, 0))
    mod_spec = (pl.BlockSpec((None, tm, d), lambda bi, ti: (bi, ti, 0)) if decode
                else pl.BlockSpec((None, 1, d), lambda bi, ti: (bi, 0, 0)))
    full = lambda a: pl.BlockSpec(a.shape, lambda bi, ti: (0,) * a.ndim)
    in_specs = [row_spec(d), mod_spec, mod_spec, mod_spec, full(g), full(gp), full(pw), full(pb), full(ps)]
    args = [x, shift, scale, gate, g, gp, pw, pb, ps]
    if decode:
        assert b == 1 and t == tm
        in_specs.append(full(prev_t))
        args.append(prev_t)
        out_specs = [row_spec(d), row_spec(d)]
        out_shape = [jax.ShapeDtypeStruct((b, t, d), F32)] * 2
        scratch = []
    else:
        out_specs = [row_spec(d), pl.BlockSpec((None, POOL_MAX, d), lambda bi, ti: (bi, 0, 0))]
        out_shape = [jax.ShapeDtypeStruct((b, t, d), F32), jax.ShapeDtypeStruct((b, POOL_MAX, d), F32)]
        scratch = [pltpu.VMEM((tm + POOL_MAX, d), F32)]
    return pl.pallas_call(
        functools.partial(_pool_kernel, tm=tm, decode=decode, pos0=pos0),
        grid=(b, t // tm), in_specs=in_specs, out_specs=out_specs, out_shape=out_shape,
        scratch_shapes=scratch, compiler_params=_cparams(2),
        name="pool_sample" if decode else "pool_prompt",
    )(*args)


def _mixin_sample_kernel(x_ref, sh_ref, sc_ref, g_ref, w_ref, wdtx_ref, dtbx_ref, cw_ref, cb_ref, prev_ref,
                         q_ref, k_ref, v_ref, z_ref, xs_ref, bc_ref, dtx_ref, raw_ref, *, pos0):
    h = _norm_mod(x_ref[...], g_ref[...], sh_ref[...], sc_ref[...])
    proj = jnp.dot(h.astype(BF16), w_ref[...], preferred_element_type=F32)
    n = h.shape[0]
    cos_t, sin_t, low = _rope_tables(jnp.float32(pos0), (n, LANES))
    for gi in range(ATT_WIDTH // LANES):
        lo = gi * LANES
        q_ref[:, lo:lo + LANES] = _rope_group(proj[:, lo:lo + LANES], cos_t, sin_t, low)
        k_ref[:, lo:lo + LANES] = _rope_group(proj[:, ATT_WIDTH + lo:ATT_WIDTH + lo + LANES], cos_t, sin_t, low)
    v_ref[...] = proj[:, 2 * ATT_WIDTH:3 * ATT_WIDTH]
    z_ref[...] = proj[:, 3 * ATT_WIDTH:3 * ATT_WIDTH + SSD_INNER]
    raw = proj[:, 3 * ATT_WIDTH + SSD_INNER:MAIN_COLS]
    raw_ref[...] = raw
    conv = cb_ref[...] + cw_ref[SSD_CONV - 1:SSD_CONV, :] * raw
    for i in range(SSD_CONV - 1):
        conv = conv + cw_ref[i:i + 1, :] * prev_ref[i]
    act = _silu(conv)
    xs_ref[...] = act[:, :SSD_INNER]
    bc_ref[...] = act[:, SSD_INNER:]
    dtx_ref[...] = _softplus(jnp.dot(h, wdtx_ref[...], preferred_element_type=F32, precision=HIGHEST) + dtbx_ref[...])


def _mixin_sample_call(x, shift, scale, g, w_main, wdt_x, dtb_x, conv_w, conv_b, prev_t, pos0):
    n, d = x.shape
    full = lambda a: pl.BlockSpec(a.shape, lambda i: (0,) * a.ndim)
    args = [x, shift, scale, g, w_main, wdt_x, dtb_x, conv_w, conv_b, prev_t]
    widths = [ATT_WIDTH] * 3 + [SSD_INNER] * 2 + [SSD_CONV_DIM - SSD_INNER, SSD_INNER, SSD_CONV_DIM]
    return pl.pallas_call(
        functools.partial(_mixin_sample_kernel, pos0=pos0),
        grid=(1,),
        in_specs=[full(a) for a in args],
        out_specs=[pl.BlockSpec((n, c), lambda i: (0, 0)) for c in widths],
        out_shape=[jax.ShapeDtypeStruct((n, c), F32) for c in widths],
        compiler_params=_cparams(1), name="mixin_sample",
    )(*args)


_PAGES_PER_BLOCK = MOBA_BLOCK // PAGE_SIZE
_SEL_PAGES = MOBA_TOPK * _PAGES_PER_BLOCK
SAMPLE_HEADS_PER_STEP = 8


def _token_minor(cache):
    return jnp.transpose(cache, (0, 1, 3, 4, 2))


def _past_kmean_kernel(pt_ref, *refs):
    pages, o_ref = refs[:PAGES_PER_STEP], refs[PAGES_PER_STEP]
    step = pl.program_id(1)
    blocks_per_step = PAGES_PER_STEP // _PAGES_PER_BLOCK

    @pl.when(step == 0)
    def _():
        o_ref[...] = jnp.zeros(o_ref.shape, F32)

    acc = o_ref[...]
    lane = lax.broadcasted_iota(jnp.int32, acc.shape, 2)
    for blk in range(blocks_per_step):
        tot = pages[blk * _PAGES_PER_BLOCK][...]
        for r in range(1, _PAGES_PER_BLOCK):
            tot = tot + pages[blk * _PAGES_PER_BLOCK + r][...]
        mean = jnp.sum(tot, axis=-1, keepdims=True) * (1.0 / MOBA_BLOCK)
        acc = jnp.where(lane == step * blocks_per_step + blk, mean, acc)
    o_ref[...] = acc


def _past_kmean_call(pt_flat, cache_kt, n_seq, n_pages):
    steps = n_pages // PAGES_PER_STEP
    nb = n_pages // _PAGES_PER_BLOCK

    def page_map(b, s, pt, *, r):
        return (0, pt[b * n_pages + s * PAGES_PER_STEP + r], 0, 0, 0)

    grid_spec = pltpu.PrefetchScalarGridSpec(
        num_scalar_prefetch=1, grid=(n_seq, steps),
        in_specs=[pl.BlockSpec((None, None, ATT_HEADS, HEAD_DIM, PAGE_SIZE), functools.partial(page_map, r=r))
                  for r in range(PAGES_PER_STEP)],
        out_specs=pl.BlockSpec((None, ATT_HEADS, HEAD_DIM, nb), lambda b, s, pt: (b, 0, 0, 0)))
    return pl.pallas_call(
        _past_kmean_kernel, grid_spec=grid_spec,
        out_shape=jax.ShapeDtypeStruct((n_seq, ATT_HEADS, HEAD_DIM, nb), F32),
        compiler_params=_cparams(2), name="past_kmean",
    )(pt_flat, *([cache_kt] * PAGES_PER_STEP))


def _sample_select_kernel(q_ref, km_ref, sel_ref, *, jown):
    gate = jnp.sum(km_ref[...] * q_ref[...], axis=1, keepdims=True)
    ids = lax.broadcasted_iota(jnp.int32, gate.shape, 2).astype(F32)
    gate = jnp.where(ids < float(jown), gate, NEG_INF)
    picks = _topk_first(gate, ids, jown, 2)
    for r in range(MOBA_TOPK):
        sel_ref[r] = picks[r].astype(jnp.int32)


def _sample_select_call(q4, kmean, jown):
    n, _, _, nb = kmean.shape
    return pl.pallas_call(
        functools.partial(_sample_select_kernel, jown=jown),
        grid=(n,),
        in_specs=[pl.BlockSpec((None, ATT_HEADS, HEAD_DIM, 1), lambda b: (b, 0, 0, 0)),
                  pl.BlockSpec((None, ATT_HEADS, HEAD_DIM, nb), lambda b: (b, 0, 0, 0))],
        out_specs=pl.BlockSpec((None, MOBA_TOPK, ATT_HEADS, 1, 1), lambda b: (b, 0, 0, 0, 0)),
        out_shape=jax.ShapeDtypeStruct((n, MOBA_TOPK, ATT_HEADS, 1, 1), jnp.int32),
        compiler_params=_cparams(1), name="sample_select",
    )(q4, kmean)


def _sample_attn_kernel(pt_ref, sel_ref, q_ref, kn_ref, vn_ref, *refs):
    n_pg = SAMPLE_HEADS_PER_STEP * _SEL_PAGES
    kp, vp, o_ref = refs[:n_pg], refs[n_pg:2 * n_pg], refs[2 * n_pg]
    b, grp = pl.program_id(0), pl.program_id(1)
    scale = 1.0 / math.sqrt(HEAD_DIM)
    for hh in range(SAMPLE_HEADS_PER_STEP):
        h = grp * SAMPLE_HEADS_PER_STEP + hh
        q = q_ref[hh] * scale
        s_self = jnp.sum(q * kn_ref[hh], axis=0, keepdims=True)
        scores = []
        for r in range(MOBA_TOPK):
            valid = sel_ref[(b * MOBA_TOPK + r) * ATT_HEADS + h] >= 0
            for pg in range(_PAGES_PER_BLOCK):
                s = jnp.sum(kp[hh * _SEL_PAGES + r * _PAGES_PER_BLOCK + pg][...] * q, axis=0, keepdims=True)
                scores.append(jnp.where(valid, s, NEG_INF))
        m = s_self
        for s in scores:
            m = jnp.maximum(m, jnp.max(s, axis=1, keepdims=True))
        p_self = jnp.exp(s_self - m)
        l = p_self
        acc = p_self * vn_ref[hh]
        for idx, s in enumerate(scores):
            p = jnp.exp(s - m)
            l = l + jnp.sum(p, axis=1, keepdims=True)
            acc = acc + jnp.sum(vp[hh * _SEL_PAGES + idx][...] * p, axis=1, keepdims=True)
        o_ref[hh] = acc / l


def _sample_attn_call(pages_flat, sel_flat, q4, k4, v4, cache_kt, cache_vt):
    n = q4.shape[0]
    hps = SAMPLE_HEADS_PER_STEP
    head_spec = pl.BlockSpec((None, hps, HEAD_DIM, 1), lambda b, g, pt, sel: (b, g, 0, 0))

    def page_map(b, g, pages, sel, *, hh, r, pg):
        h = g * hps + hh
        return (0, pages[((b * MOBA_TOPK + r) * ATT_HEADS + h) * _PAGES_PER_BLOCK + pg], h, 0, 0)

    page_specs = [pl.BlockSpec((None, None, None, HEAD_DIM, PAGE_SIZE), functools.partial(page_map, hh=hh, r=r, pg=pg))
                  for hh in range(hps) for r in range(MOBA_TOPK) for pg in range(_PAGES_PER_BLOCK)]
    grid_spec = pltpu.PrefetchScalarGridSpec(
        num_scalar_prefetch=2, grid=(n, ATT_HEADS // hps),
        in_specs=[head_spec, head_spec, head_spec] + page_specs + page_specs,
        out_specs=head_spec)
    n_pg = hps * _SEL_PAGES
    return pl.pallas_call(
        _sample_attn_kernel, grid_spec=grid_spec,
        out_shape=jax.ShapeDtypeStruct((n, ATT_HEADS, HEAD_DIM, 1), F32),
        compiler_params=_cparams(2), name="sample_attn",
    )(pages_flat, sel_flat, q4, k4, v4, *([cache_kt] * n_pg), *([cache_vt] * n_pg))


def _ssd_sample_kernel(xs_ref, bc_ref, dtx_ref, z_ref, st_ref, alogx_ref, dsk_ref, nw_ref, y_ref, sto_ref):
    xs, dtx = xs_ref[...], dtx_ref[...]
    decay = jnp.exp(dtx * (-jnp.exp(alogx_ref[...])))
    xdt = xs * dtx
    bc = bc_ref[...]
    rows_per_group = SSD_INNER // SSD_GROUPS
    gs = SSD_GROUPS * SSD_STATE
    eye = (lax.broadcasted_iota(jnp.int32, (rows_per_group, rows_per_group), 0)
           == lax.broadcasted_iota(jnp.int32, (rows_per_group, rows_per_group), 1))

    def to_col(v):
        return jnp.sum(jnp.where(eye, jnp.broadcast_to(v, eye.shape), 0.0), axis=1, keepdims=True)

    def to_row(v):
        return jnp.sum(jnp.where(eye, jnp.broadcast_to(v, eye.shape), 0.0), axis=0, keepdims=True)

    ys = []
    for g in range(SSD_GROUPS):
        lo = g * rows_per_group
        b_row = bc[:, g * SSD_STATE:(g + 1) * SSD_STATE]
        c_row = bc[:, gs + g * SSD_STATE:gs + (g + 1) * SSD_STATE]
        hg = st_ref[lo:lo + rows_per_group, :]
        hnew = to_col(decay[:, lo:lo + rows_per_group]) * hg + to_col(xdt[:, lo:lo + rows_per_group]) * b_row
        sto_ref[lo:lo + rows_per_group, :] = hnew
        ys.append(to_row(jnp.sum(hnew * c_row, axis=1, keepdims=True)))
    y = jnp.concatenate(ys, axis=1)
    y_ref[...] = _gated_norm(y, xs, z_ref[...], dsk_ref[...], nw_ref[...])


def _ssd_sample_call(xs, bc, dtx, z, state, alog_x, dsk_row, nw_row):
    n = xs.shape[0]
    r3 = lambda a: a.reshape(n, 1, a.shape[-1])
    row_spec = lambda c: pl.BlockSpec((None, 1, c), lambda b: (b, 0, 0))
    st_spec = pl.BlockSpec((None, SSD_INNER, SSD_STATE), lambda b: (b, 0, 0))
    full = lambda a: pl.BlockSpec(a.shape, lambda b: (0,) * a.ndim)
    y, st = pl.pallas_call(
        _ssd_sample_kernel,
        grid=(n,),
        in_specs=[row_spec(SSD_INNER), row_spec(SSD_CONV_DIM - SSD_INNER), row_spec(SSD_INNER), row_spec(SSD_INNER),
                  st_spec, full(alog_x), full(dsk_row), full(nw_row)],
        out_specs=[row_spec(SSD_INNER), st_spec],
        out_shape=[jax.ShapeDtypeStruct((n, 1, SSD_INNER), F32), jax.ShapeDtypeStruct((n, SSD_INNER, SSD_STATE), F32)],
        compiler_params=_cparams(1), name="ssd_sample",
    )(r3(xs), r3(bc), r3(dtx), r3(z), state, alog_x, dsk_row, nw_row)
    return y.reshape(n, SSD_INNER), st


def kernel(x_prompt, x_sample, cache_k, cache_v, state_ssm, state_ssd_conv, state_pool, state_ffn_conv, page_table,
           c_prompt, c_sample, ada_w, ada_b, norm_pre, norm_post, mix_in_w, mix_out_w, ssd_conv_w, ssd_conv_b,
           ssd_dt_bias, ssd_a_log, ssd_d, ssd_norm_w, pool_w, pool_b, pool_scale, ffn_up_w, ffn_conv_w,
           ffn_conv_b, ffn_down_w):
    bp, t, d = x_prompt.shape
    ns = x_sample.shape[0]
    n_pages = page_table.shape[1]
    past_len = n_pages * cache_k.shape[2]
    assert x_sample.shape[1] == 1 and cache_k.shape[2] == PAGE_SIZE and past_len % MOBA_BLOCK == 0
    assert t % ROW_TILE == 0 or t < ROW_TILE

    w_main = mix_in_w[0, :, :MAIN_COLS].astype(BF16)
    w_dt = mix_in_w[0, :, MAIN_COLS:]
    wdt_t = w_dt.T
    wdt_x = jnp.repeat(w_dt, SSD_HEADDIM, axis=1)
    dtb_col = ssd_dt_bias[0].reshape(SSD_HEADS, 1)
    dtb_x = jnp.repeat(ssd_dt_bias[0], SSD_HEADDIM).reshape(1, SSD_INNER)
    alog_col = ssd_a_log[0].reshape(SSD_HEADS, 1)
    alog_x = jnp.repeat(ssd_a_log[0], SSD_HEADDIM).reshape(1, SSD_INNER)
    dsk_row = jnp.repeat(ssd_d[0], SSD_HEADDIM).reshape(1, SSD_INNER)
    nw_row = ssd_norm_w[0].reshape(1, SSD_INNER)
    w_att = mix_out_w[0, :ATT_WIDTH].astype(BF16)
    w_ssd = mix_out_w[0, ATT_WIDTH:].astype(BF16)
    conv_w0 = ssd_conv_w[0]
    conv_b0 = ssd_conv_b[0].reshape(1, SSD_CONV_DIM)
    up_w = ffn_up_w.astype(BF16)
    dn_w = ffn_down_w.astype(BF16)
    pw = pool_w[0].astype(BF16)
    pb = pool_b[0].reshape(1, d)
    ps = pool_scale[0].reshape(1, d)

    mods = _ada_call(jnp.concatenate([c_prompt, c_sample], axis=0), ada_w, ada_b)

    def mod(layer, slot, sample):
        m = mods[layer * 2 + slot]
        m = m[bp:] if sample else m[:bp]
        parts = (m[:, :d], m[:, d:2 * d], m[:, 2 * d:])
        return tuple(p[None] for p in parts) if sample else tuple(p[:, None, :] for p in parts)

    npre = lambda layer, slot: norm_pre[layer, slot].reshape(1, d)
    npost = lambda layer, slot: norm_post[layer, slot].reshape(1, d)

    def ffn(x, layer, sample, prev=None, mix=None):
        sh, sc, gt = mod(layer, 1, sample)
        return _ffn_call(x, sh, sc, gt, npre(layer, 1), npost(layer, 1), up_w[layer], ffn_conv_w[layer],
                         ffn_conv_b[layer].reshape(1, 2 * D_FF), dn_w[layer], prev=prev, mix=mix)

    sh, sc, gt = mod(0, 0, False)
    q, k, v, z, xs, bc, dt_t, kmean, cst = _mixin_prompt_call(
        x_prompt, sh, sc, npre(0, 0), w_main, wdt_t, dtb_col, conv_w0, conv_b0)
    att = _attn_prompt_call(q, k, v, kmean)
    y, ssm_p = _ssd_prompt_call(xs, bc, dt_t, z, alog_col, dsk_row, nw_row)
    xp, fst0 = ffn(x_prompt, 0, False, mix=(att, y, gt, npost(0, 0), w_att, w_ssd))
    sh, sc, gt = mod(1, 0, False)
    xp, plast = _pool_call(xp, sh, sc, gt, npre(1, 0), npost(1, 0), pw, pb, ps)
    xp, fst1 = ffn(xp, 1, False)

    k_p = k.reshape(1, bp, t, ATT_HEADS, HEAD_DIM)
    v_p = v.reshape(1, bp, t, ATT_HEADS, HEAD_DIM)
    ssm_p = ssm_p.reshape(1, bp, SSD_HEADS, SSD_HEADDIM, SSD_STATE)
    sconv_p = cst[None, :, SUBLANES - (SSD_CONV - 1):, :]
    pool_p = plast[None, :, 1:, :]
    fconv_p = jnp.stack([fst0, fst1])[:, :, SUBLANES - (FFN_CONV - 1):, :]

    xs0 = x_sample.reshape(ns, d)
    sh, sc, gt = mod(0, 0, True)
    conv_prev_t = jnp.transpose(state_ssd_conv[0], (1, 0, 2))
    qs, ks, vs, zs, xss, bcs, dtx, raw = _mixin_sample_call(
        xs0, sh[0], sc[0], npre(0, 0), w_main, wdt_x, dtb_x, conv_w0, conv_b0, conv_prev_t, past_len)
    pt_flat = page_table.reshape(-1)
    cache_kt, cache_vt = _token_minor(cache_k), _token_minor(cache_v)
    kmean_s = _past_kmean_call(pt_flat, cache_kt, ns, n_pages)
    jown = past_len // MOBA_BLOCK
    as_cols = lambda a: a.reshape(ns, ATT_HEADS, HEAD_DIM, 1)
    sel = _sample_select_call(as_cols(qs), kmean_s, jown)
    sel3 = sel.reshape(ns, MOBA_TOPK, ATT_HEADS)
    page_pos = jnp.maximum(sel3, 0)[..., None] * _PAGES_PER_BLOCK + jnp.arange(_PAGES_PER_BLOCK, dtype=jnp.int32)
    sel_pages = jnp.take_along_axis(page_table, page_pos.reshape(ns, -1), axis=1)
    att_s = _sample_attn_call(sel_pages.reshape(-1), sel3.reshape(-1), as_cols(qs), as_cols(ks), as_cols(vs),
                              cache_kt, cache_vt).reshape(ns, ATT_WIDTH)
    y_s, ssm_s = _ssd_sample_call(xss, bcs, dtx, zs, state_ssm[0].reshape(ns, SSD_INNER, SSD_STATE),
                                  alog_x, dsk_row, nw_row)
    xq, u0 = ffn(xs0[None], 0, True, prev=(state_ffn_conv[0, :, 0], state_ffn_conv[0, :, 1]),
                 mix=(att_s[None], y_s[None], gt, npost(0, 0), w_att, w_ssd))
    sh, sc, gt = mod(1, 0, True)
    pool_prev_t = jnp.transpose(state_pool[0], (1, 0, 2))
    xq, h_pool = _pool_call(xq, sh, sc, gt, npre(1, 0), npost(1, 0), pw, pb, ps, prev_t=pool_prev_t, pos0=past_len)
    xq, u1 = ffn(xq, 1, True, prev=(state_ffn_conv[1, :, 0], state_ffn_conv[1, :, 1]))

    y_sample = xq.reshape(ns, 1, d)
    k_s = ks.reshape(1, ns, 1, ATT_HEADS, HEAD_DIM)
    v_s = vs.reshape(1, ns, 1, ATT_HEADS, HEAD_DIM)
    ssm_s = ssm_s.reshape(1, ns, SSD_HEADS, SSD_HEADDIM, SSD_STATE)
    sconv_s = jnp.concatenate([state_ssd_conv[0, :, 1:], raw[:, None, :]], axis=1)[None]
    pool_s = jnp.concatenate([state_pool[0, :, 1:], h_pool[0][:, None, :]], axis=1)[None]
    fconv_s = jnp.stack([jnp.stack([state_ffn_conv[0, :, 1], u0], axis=1),
                         jnp.stack([state_ffn_conv[1, :, 1], u1], axis=1)])
    return (xp, y_sample, k_p, v_p, ssm_p, sconv_p, pool_p, fconv_p, k_s, v_s, ssm_s, sconv_s, pool_s, fconv_s)
```
